```python
import jax
import jax.numpy as jnp
from jax import lax
import numpy as np

D_MODEL = 1024
BATCH = 8
SEQ = 8192
DEPTH = 4

FOX_HEADS = 8
FOX_HEAD_DIM = 64
FOX_WIDTH = FOX_HEADS * FOX_HEAD_DIM
Q_BLOCK = 128
GLA_HEADS = 4
GLA_KEY_DIM = 64
GLA_VAL_DIM = 128
GLA_K_WIDTH = GLA_HEADS * GLA_KEY_DIM
GLA_V_WIDTH = GLA_HEADS * GLA_VAL_DIM
GLA_GATE_RANK = 16
GLA_TAU = 16.0
GLA_CHUNK = 64
POOL_WINDOWS = (2, 4, 8, 16)
POOL_GROUPS = len(POOL_WINDOWS)
POOL_GROUP_DIM = 128
POOL_WIDTH = POOL_GROUPS * POOL_GROUP_DIM
N_BRANCHES = 3
RMS_EPS = 1e-6
IN_SPLITS = (FOX_WIDTH, FOX_WIDTH, FOX_WIDTH, FOX_HEADS,
             GLA_K_WIDTH, GLA_K_WIDTH, GLA_V_WIDTH, GLA_GATE_RANK,
             POOL_WIDTH,
             FOX_WIDTH, GLA_V_WIDTH, POOL_WIDTH,
             N_BRANCHES * D_MODEL)
IN_WIDTH = sum(IN_SPLITS)

kernel_name = 'hybrid_fox_gla_pool_gated_block'


def rms_norm(x, g):
    xf = x.astype(jnp.float32)
    y = xf * lax.rsqrt(jnp.mean(xf * xf, axis=-1, keepdims=True) + RMS_EPS)
    return (y * g.astype(jnp.float32)).astype(x.dtype)


def fox_attention(q, k, v, log_f):
    b, s, h, dh = q.shape
    nb = s // Q_BLOCK
    f_cum = jnp.cumsum(log_f.astype(jnp.float32), axis=1).transpose(0, 2, 1)
    kt = k.astype(jnp.float32).transpose(0, 2, 1, 3)
    vt = v.astype(jnp.float32).transpose(0, 2, 1, 3)
    qb = (q.astype(jnp.float32) * dh ** -0.5).reshape(b, nb, Q_BLOCK, h, dh).transpose(1, 0, 3, 2, 4)
    fb = f_cum.reshape(b, h, nb, Q_BLOCK).transpose(2, 0, 1, 3)
    key_pos = jnp.arange(s)

    def one_block(args):
        qi, fi, blk = args
        logits = jnp.einsum('bhqd,bhkd->bhqk', qi, kt)
        logits = logits + fi[..., :, None] - f_cum[:, :, None, :]
        q_pos = blk * Q_BLOCK + jnp.arange(Q_BLOCK)
        causal = key_pos[None, :] <= q_pos[:, None]
        probs = jax.nn.softmax(jnp.where(causal, logits, -jnp.inf), axis=-1)
        return jnp.einsum('bhqk,bhkd->bhqd', probs, vt)

    out = lax.map(one_block, (qb, fb, jnp.arange(nb)))
    return out.transpose(1, 0, 3, 2, 4).reshape(b, s, h * dh).astype(q.dtype)


def gla_chunked(q, k, v, log_a):
    b, s, h, dk = q.shape
    dv = v.shape[-1]
    nc = s // GLA_CHUNK

    def to_chunks(t):
        return t.astype(jnp.float32).reshape(b, nc, GLA_CHUNK, h, t.shape[-1]).transpose(1, 0, 3, 2, 4)

    qc = to_chunks(q) * dk ** -0.5
    kc, vc, ac = to_chunks(k), to_chunks(v), to_chunks(log_a)
    causal = jnp.tril(jnp.ones((GLA_CHUNK, GLA_CHUNK), dtype=bool))[:, :, None]

    def step(state, inp):
        qi, ki, vi, ai = inp
        cum = jnp.cumsum(ai, axis=2)
        last = cum[:, :, -1:, :]
        o_inter = jnp.einsum('bhtd,bhde->bhte', qi * jnp.exp(cum), state)
        rel = cum[:, :, :, None, :] - cum[:, :, None, :, :]
        decay = jnp.exp(jnp.where(causal, rel, -jnp.inf))
        scores = jnp.einsum('bhtd,bhsd,bhtsd->bhts', qi, ki, decay)
        o_intra = jnp.einsum('bhts,bhse->bhte', scores, vi)
        new_state = (jnp.exp(last[:, :, 0, :])[..., None] * state
                     + jnp.einsum('bhsd,bhse->bhde', ki * jnp.exp(last - cum), vi))
        return new_state, o_inter + o_intra

    state0 = jnp.zeros((b, h, dk, dv), jnp.float32)
    _, out = lax.scan(step, state0, (qc, kc, vc, ac))
    return out.transpose(1, 0, 3, 2, 4).reshape(b, s, h, dv)


def pool_mixer(u, w_pool, pool_scale):
    s = u.shape[1]
    uf = u.astype(jnp.float32)
    csum = jnp.cumsum(uf, axis=1)
    count = jnp.arange(1, s + 1, dtype=jnp.float32)
    outs = []
    for g, w in enumerate(POOL_WINDOWS):
        lo, hi = g * POOL_GROUP_DIM, (g + 1) * POOL_GROUP_DIM
        cs = csum[..., lo:hi]
        lagged = jnp.pad(cs[:, :s - w], ((0, 0), (w, 0), (0, 0)))
        mean = (cs - lagged) / jnp.minimum(count, float(w))[None, :, None]
        diff = (mean - uf[..., lo:hi]).astype(u.dtype)
        outs.append(jnp.einsum('bsc,cd->bsd', diff, w_pool[g]))
    return jnp.concatenate(outs, axis=-1) * pool_scale


def hybrid_layer(x, c, w_ada, b_ada, g_pre, g_post, w_in, b_forget, w_gla_gate, b_gla_gate,
                 g_gla_norm, w_pool, pool_scale, w_br_fox, w_br_gla, w_br_pool, w_out):
    b, s, _ = x.shape
    mod = jnp.einsum('bd,de->be', jax.nn.silu(c), w_ada) + b_ada
    shift, scale, gate = jnp.split(mod[:, None, :], 3, axis=-1)
    h = rms_norm(x, g_pre) * (1.0 + scale) + shift

    proj = jnp.einsum('bsd,de->bse', h, w_in)
    split_idx = np.cumsum(IN_SPLITS)[:-1].tolist()
    (fq, fk, fv, ff, gq, gk, gv, g_lr, pu,
     z_fox, z_gla, z_pool, merge) = jnp.split(proj, split_idx, axis=-1)

    fox_shape = (b, s, FOX_HEADS, FOX_HEAD_DIM)
    log_f = jax.nn.log_sigmoid((ff + b_forget).astype(jnp.float32))
    o_fox = fox_attention(fq.reshape(fox_shape), fk.reshape(fox_shape), fv.reshape(fox_shape), log_f)

    gate_logits = jnp.einsum('bsr,rk->bsk', g_lr, w_gla_gate) + b_gla_gate
    log_a = jax.nn.log_sigmoid(gate_logits.astype(jnp.float32)) / GLA_TAU
    k_shape = (b, s, GLA_HEADS, GLA_KEY_DIM)
    o_gla = gla_chunked(gq.reshape(k_shape), gk.reshape(k_shape),
                        gv.reshape(b, s, GLA_HEADS, GLA_VAL_DIM), log_a.reshape(k_shape))
    o_gla = rms_norm(o_gla, g_gla_norm.reshape(GLA_HEADS, GLA_VAL_DIM)).reshape(b, s, GLA_V_WIDTH).astype(x.dtype)

    o_pool = pool_mixer(pu, w_pool, pool_scale).astype(x.dtype)

    y_fox = jnp.einsum('bse,ed->bsd', o_fox * jax.nn.silu(z_fox), w_br_fox)
    y_gla = jnp.einsum('bse,ed->bsd', o_gla * jax.nn.silu(z_gla), w_br_gla)
    y_pool = jnp.einsum('bse,ed->bsd', o_pool * jax.nn.silu(z_pool), w_br_pool)
    m_fox, m_gla, m_pool = jnp.split(merge, 3, axis=-1)
    merged = jax.nn.sigmoid(m_fox) * y_fox + jax.nn.sigmoid(m_gla) * y_gla + jax.nn.sigmoid(m_pool) * y_pool
    out = jnp.einsum('bsd,de->bse', merged, w_out)

    return x + gate * rms_norm(out, g_post)


def setup_inputs(seed: int = 0) -> dict:
    key = jax.random.key(seed)
    ks = jax.random.split(key, 18)
    f32 = jnp.float32

    def nrm(k, shape, std):
        return jax.random.normal(k, shape, f32) * std

    return {
        'x': nrm(ks[0], (BATCH, SEQ, D_MODEL), 1.0),
        'c': nrm(ks[1], (BATCH, D_MODEL), 1.0),
        'w_ada': nrm(ks[2], (DEPTH, D_MODEL, 3 * D_MODEL), 0.5 * D_MODEL ** -0.5),
        'b_ada': nrm(ks[3], (DEPTH, 3 * D_MODEL), 0.01),
        'g_pre': 1.0 + nrm(ks[4], (DEPTH, D_MODEL), 0.1),
        'g_post': 1.0 + nrm(ks[5], (DEPTH, D_MODEL), 0.1),
        'w_in': nrm(ks[6], (DEPTH, D_MODEL, IN_WIDTH), D_MODEL ** -0.5),
        'b_forget': 2.0 + nrm(ks[7], (DEPTH, FOX_HEADS), 0.5),
        'w_gla_gate': nrm(ks[8], (DEPTH, GLA_GATE_RANK, GLA_K_WIDTH), GLA_GATE_RANK ** -0.5),
        'b_gla_gate': nrm(ks[9], (DEPTH, GLA_K_WIDTH), 0.1),
        'g_gla_norm': 1.0 + nrm(ks[10], (DEPTH, GLA_V_WIDTH), 0.1),
        'w_pool': nrm(ks[11], (DEPTH, POOL_GROUPS, POOL_GROUP_DIM, POOL_GROUP_DIM), POOL_GROUP_DIM ** -0.5),
        'pool_scale': 1.0 + nrm(ks[12], (DEPTH, POOL_WIDTH), 0.1),
        'w_br_fox': nrm(ks[13], (DEPTH, FOX_WIDTH, D_MODEL), FOX_WIDTH ** -0.5),
        'w_br_gla': nrm(ks[14], (DEPTH, GLA_V_WIDTH, D_MODEL), GLA_V_WIDTH ** -0.5),
        'w_br_pool': nrm(ks[15], (DEPTH, POOL_WIDTH, D_MODEL), POOL_WIDTH ** -0.5),
        'w_out': nrm(ks[16], (DEPTH, D_MODEL, D_MODEL), D_MODEL ** -0.5),
    }


def reference(x, c, w_ada, b_ada, g_pre, g_post, w_in, b_forget, w_gla_gate, b_gla_gate,
              g_gla_norm, w_pool, pool_scale, w_br_fox, w_br_gla, w_br_pool, w_out):
    h = x
    for i in range(DEPTH):
        h = hybrid_layer(h, c, w_ada[i], b_ada[i], g_pre[i], g_post[i], w_in[i], b_forget[i],
                         w_gla_gate[i], b_gla_gate[i], g_gla_norm[i], w_pool[i], pool_scale[i],
                         w_br_fox[i], w_br_gla[i], w_br_pool[i], w_out[i])
    return h
```

```python
import functools

import numpy as np
import jax
import jax.numpy as jnp
from jax import lax
from jax.experimental import pallas as pl
from jax.experimental.pallas import tpu as pltpu

F32 = jnp.float32
BF16 = jnp.bfloat16

D_MODEL = 1024
FOX_HEADS = 8
FOX_HEAD_DIM = 64
FOX_WIDTH = FOX_HEADS * FOX_HEAD_DIM
GLA_HEADS = 4
GLA_KEY_DIM = 64
GLA_VAL_DIM = 128
GLA_K_WIDTH = GLA_HEADS * GLA_KEY_DIM
GLA_V_WIDTH = GLA_HEADS * GLA_VAL_DIM
GLA_GATE_RANK = 16
GLA_TAU = 16.0
POOL_WINDOWS = (2, 4, 8, 16)
POOL_GROUP_DIM = 128
POOL_WIDTH = len(POOL_WINDOWS) * POOL_GROUP_DIM
N_BRANCHES = 3
RMS_EPS = 1e-6
IN_SPLITS = (FOX_WIDTH, FOX_WIDTH, FOX_WIDTH, FOX_HEADS,
             GLA_K_WIDTH, GLA_K_WIDTH, GLA_V_WIDTH, GLA_GATE_RANK,
             POOL_WIDTH, FOX_WIDTH, GLA_V_WIDTH, POOL_WIDTH, N_BRANCHES * D_MODEL)
_OFF = np.concatenate([[0], np.cumsum(IN_SPLITS)]).tolist()

LANES = 128
SMALL_WIDTH = LANES
POOL_HALO = 16
MASK_VALUE = -1e30
V7X_VMEM_BYTES = 64 * 1024 * 1024

IN_TOKENS = 512
FORGET_TOKENS = 512
FOX_TILE = 256
GLA_TOKENS = 256
GLA_CHUNK = 64
MERGE_TOKENS = 256


def _vmem_limit(block_bytes, resident_bytes, temp_bytes):
    need = 2 * block_bytes + resident_bytes + temp_bytes
    return int(min(max(need, 16 * 1024 * 1024), V7X_VMEM_BYTES - 8 * 1024 * 1024))


def _resident(shape):
    zeros = (0,) * len(shape)
    return pl.BlockSpec(shape, lambda *_: zeros, pipeline_mode=pl.Buffered(1))


def _div_pow2(i, n):
    shift = n.bit_length() - 1
    assert 1 << shift == n
    return jnp.right_shift(i, shift)


def _log_sigmoid(x):
    return jnp.minimum(x, 0.0) - jnp.log1p(jnp.exp(-jnp.abs(x)))


def _sigmoid(x):
    return jax.nn.sigmoid(x)


def _silu(x):
    return x * _sigmoid(x)


def _dot(a, b):
    return jnp.dot(a, b, preferred_element_type=F32)


def _dot_nt(a, b):
    return lax.dot_general(a, b, (((1,), (1,)), ((), ())), preferred_element_type=F32)


def _dot_tn(a, b):
    return lax.dot_general(a, b, (((0,), (0,)), ((), ())), preferred_element_type=F32)


def _ones_dot_f32(ones_bf16, x):
    hi = x.astype(BF16)
    r1 = x - hi.astype(F32)
    mid = r1.astype(BF16)
    lo = (r1 - mid.astype(F32)).astype(BF16)
    return _dot(ones_bf16, hi) + _dot(ones_bf16, mid) + _dot(ones_bf16, lo)


def _modulated_norm(x, g, mod):
    y = x * lax.rsqrt(jnp.mean(x * x, axis=-1, keepdims=True) + RMS_EPS) * g
    shift = mod[:, 0:D_MODEL]
    scale = mod[:, D_MODEL:2 * D_MODEL]
    return (y * (1.0 + scale) + shift).astype(BF16)


def _mod_kernel(c_ref, w_ref, b_ref, o_ref):
    c = c_ref[...]
    o_ref[0] = jnp.dot(_silu(c), w_ref[0], preferred_element_type=F32,
                       precision=lax.Precision.HIGHEST) + b_ref[0]


def _adaln_mod(c, w_ada, b_ada):
    depth, d, d3 = w_ada.shape
    b = c.shape[0]
    n_col = d3 // d
    return pl.pallas_call(
        _mod_kernel,
        grid=(depth, n_col),
        in_specs=[pl.BlockSpec((b, d), lambda l, j: (0, 0)),
                  pl.BlockSpec((1, d, d), lambda l, j: (l, 0, j)),
                  pl.BlockSpec((1, 1, d), lambda l, j: (l, 0, j))],
        out_specs=pl.BlockSpec((1, b, d), lambda l, j: (l, 0, j)),
        out_shape=jax.ShapeDtypeStruct((depth, b, d3), F32),
        compiler_params=pltpu.CompilerParams(
            dimension_semantics=("parallel", "parallel"),
            vmem_limit_bytes=_vmem_limit(d * d * 4 + 2 * b * d * 4, 0, 4 * b * d * 4)),
        name="adaln_mod",
    )(c, w_ada, b_ada.reshape(depth, 1, d3))


_A_FOX_Q = 0
_A_FOX_V = _A_FOX_Q + FOX_WIDTH
_A_GLA_Q = _A_FOX_V + FOX_WIDTH
_A_GLA_K = _A_GLA_Q + GLA_K_WIDTH
_A_GLA_V = _A_GLA_K + GLA_K_WIDTH
_A_WIDTH = _A_GLA_V + GLA_V_WIDTH


def _in_proj_kernel(x_ref, mod_ref, g_ref, wa_ref, wkt_ref, ws_ref,
                    fq_ref, fv_ref, gq_ref, gk_ref, gv_ref, fkt_ref, small_ref):
    h = _modulated_norm(x_ref[0], g_ref[...], mod_ref[0])
    fq_ref[0] = _dot(h, wa_ref[:, _A_FOX_Q:_A_FOX_V]).astype(BF16)
    fv_ref[0] = _dot(h, wa_ref[:, _A_FOX_V:_A_GLA_Q]).astype(BF16)
    gq_ref[0] = _dot(h, wa_ref[:, _A_GLA_Q:_A_GLA_K]).astype(BF16)
    gk_ref[0] = _dot(h, wa_ref[:, _A_GLA_K:_A_GLA_V]).astype(BF16)
    gv_ref[0] = _dot(h, wa_ref[:, _A_GLA_V:_A_WIDTH]).astype(BF16)
    fkt_ref[0] = _dot_nt(wkt_ref[...], h).astype(BF16)
    small_ref[0] = _dot(h, ws_ref[...])


def _in_proj(x, mod, g_pre, w_a, w_kt, w_small):
    b, s, d = x.shape
    t = min(IN_TOKENS, s)
    tok = lambda width: pl.BlockSpec((1, t, width), lambda bi, i: (bi, i, 0))
    out_shapes = (
        jax.ShapeDtypeStruct((b, s, FOX_WIDTH), BF16),
        jax.ShapeDtypeStruct((b, s, FOX_WIDTH), BF16),
        jax.ShapeDtypeStruct((b, s, GLA_K_WIDTH), BF16),
        jax.ShapeDtypeStruct((b, s, GLA_K_WIDTH), BF16),
        jax.ShapeDtypeStruct((b, s, GLA_V_WIDTH), BF16),
        jax.ShapeDtypeStruct((b, FOX_WIDTH, s), BF16),
        jax.ShapeDtypeStruct((b, s, SMALL_WIDTH), F32),
    )
    out_specs = (tok(FOX_WIDTH), tok(FOX_WIDTH), tok(GLA_K_WIDTH), tok(GLA_K_WIDTH), tok(GLA_V_WIDTH),
                 pl.BlockSpec((1, FOX_WIDTH, t), lambda bi, i: (bi, 0, i)), tok(SMALL_WIDTH))
    n_out = _A_WIDTH + FOX_WIDTH
    block_bytes = t * d * 4 + t * n_out * 2 + t * SMALL_WIDTH * 4
    weight_bytes = d * (n_out + SMALL_WIDTH) * 2
    return pl.pallas_call(
        _in_proj_kernel,
        grid=(b, s // t),
        in_specs=[tok(d),
                  pl.BlockSpec((1, 1, 3 * d), lambda bi, i: (bi, 0, 0)),
                  _resident((1, d)), _resident(w_a.shape), _resident(w_kt.shape), _resident(w_small.shape)],
        out_specs=out_specs,
        out_shape=out_shapes,
        compiler_params=pltpu.CompilerParams(
            dimension_semantics=("parallel", "parallel"),
            vmem_limit_bytes=_vmem_limit(block_bytes, weight_bytes, 6 * t * d * 4)),
        name="in_proj",
    )(x, mod, g_pre, w_a, w_kt, w_small)


def _forget_kernel(small_ref, b_ref, fcol_ref, frow_ref, carry_ref):
    @pl.when(pl.program_id(1) == 0)
    def _():
        carry_ref[...] = jnp.zeros_like(carry_ref)

    t = small_ref.shape[1]
    log_f = _log_sigmoid(small_ref[0] + b_ref[...])
    row = lax.broadcasted_iota(jnp.int32, (t, t), 0)
    col = lax.broadcasted_iota(jnp.int32, (t, t), 1)
    tri = jnp.where(col <= row, 1.0, 0.0).astype(BF16)
    cs = _ones_dot_f32(tri, log_f) + carry_ref[...]
    carry_ref[...] = cs[t - 1:t, :]
    fcol_ref[0] = cs
    frow_ref[0] = cs.T[0:FOX_HEADS, :]


def _forget_cumsum(small, b_forget_pad):
    b, s, _ = small.shape
    t = min(FORGET_TOKENS, s)
    return pl.pallas_call(
        _forget_kernel,
        grid=(b, s // t),
        in_specs=[pl.BlockSpec((1, t, SMALL_WIDTH), lambda bi, i: (bi, i, 0)),
                  pl.BlockSpec((1, SMALL_WIDTH), lambda bi, i: (0, 0))],
        out_specs=(pl.BlockSpec((1, t, SMALL_WIDTH), lambda bi, i: (bi, i, 0)),
                   pl.BlockSpec((1, FOX_HEADS, t), lambda bi, i: (bi, 0, i))),
        out_shape=(jax.ShapeDtypeStruct((b, s, SMALL_WIDTH), F32),
                   jax.ShapeDtypeStruct((b, FOX_HEADS, s), F32)),
        scratch_shapes=[pltpu.VMEM((1, SMALL_WIDTH), F32)],
        compiler_params=pltpu.CompilerParams(
            dimension_semantics=("parallel", "arbitrary"),
            vmem_limit_bytes=_vmem_limit(3 * t * SMALL_WIDTH * 4, 0, 2 * t * t * 4 + 8 * t * SMALL_WIDTH * 4)),
        name="forget_cumsum",
    )(small, b_forget_pad)


def _fox_kernel(q_ref, kt_ref, v_ref, fcol_ref, frow_ref, o_ref):
    t = q_ref.shape[1]
    hp = pl.program_id(1)
    i = pl.program_id(2)
    q = q_ref[0]
    lane = lax.broadcasted_iota(jnp.int32, (t, LANES), 1)
    low = lane < FOX_HEAD_DIM
    zero = jnp.zeros_like(q)
    q_heads = (jnp.where(low, q, zero), jnp.where(low, zero, q))
    fcol = fcol_ref[0]
    f_q = tuple(jnp.sum(jnp.where(lane == 2 * hp + h, fcol, 0.0), axis=1, keepdims=True) for h in range(2))

    def step(j, carry, diagonal):
        k0 = pl.multiple_of(j * t, t)
        kt = kt_ref[0, :, pl.ds(k0, t)]
        v = v_ref[0, pl.ds(k0, t), :]
        out = []
        for h in range(2):
            m, l, acc = carry[h]
            f_k = frow_ref[0, pl.ds(2 * hp + h, 1), pl.ds(k0, t)]
            s = _dot(q_heads[h], kt) + (f_q[h] - f_k)
            if diagonal:
                r = lax.broadcasted_iota(jnp.int32, (t, t), 0)
                c = lax.broadcasted_iota(jnp.int32, (t, t), 1)
                s = jnp.where(c <= r, s, MASK_VALUE)
            m_new = jnp.maximum(m, jnp.max(s, axis=1, keepdims=True))
            alpha = jnp.exp(m - m_new)
            p = jnp.exp(s - m_new)
            l = alpha * l + jnp.sum(p, axis=1, keepdims=True)
            acc = alpha * acc + _dot(p.astype(BF16), v)
            out.append((m_new, l, acc))
        return tuple(out)

    head0 = (jnp.full((t, 1), MASK_VALUE, F32), jnp.zeros((t, 1), F32), jnp.zeros((t, LANES), F32))
    carry = lax.fori_loop(0, i, lambda j, c: step(j, c, False), (head0, head0))
    (_, l0, a0), (_, l1, a1) = step(i, carry, True)
    o_ref[0] = jnp.where(low, a0 * (1.0 / l0), a1 * (1.0 / l1)).astype(o_ref.dtype)


def _fox_attention(fq, fkt, fv, fcol, frow):
    b, s, _ = fq.shape
    t = min(FOX_TILE, s)
    pairs = FOX_WIDTH // LANES
    block_bytes = 2 * t * LANES * 2 + t * LANES * 4 + 2 * s * LANES * 2 + FOX_HEADS * s * 4
    return pl.pallas_call(
        _fox_kernel,
        grid=(b, pairs, s // t),
        in_specs=[pl.BlockSpec((1, t, LANES), lambda bi, hp, i: (bi, i, hp)),
                  pl.BlockSpec((1, LANES, s), lambda bi, hp, i: (bi, hp, 0)),
                  pl.BlockSpec((1, s, LANES), lambda bi, hp, i: (bi, 0, hp)),
                  pl.BlockSpec((1, t, SMALL_WIDTH), lambda bi, hp, i: (bi, i, 0)),
                  pl.BlockSpec((1, FOX_HEADS, s), lambda bi, hp, i: (bi, 0, 0))],
        out_specs=pl.BlockSpec((1, t, LANES), lambda bi, hp, i: (bi, i, hp)),
        out_shape=jax.ShapeDtypeStruct((b, s, FOX_WIDTH), BF16),
        compiler_params=pltpu.CompilerParams(
            dimension_semantics=("parallel", "parallel", "parallel"),
            vmem_limit_bytes=_vmem_limit(block_bytes, 0, 16 * t * t * 4)),
        name="fox_attention",
    )(fq, fkt, fv, fcol, frow)


def _gla_kernel(q_ref, k_ref, v_ref, small_ref, wg_ref, bg_ref, gn_ref, o_ref, state_ref):
    @pl.when(pl.program_id(1) == 0)
    def _():
        state_ref[...] = jnp.zeros_like(state_ref)

    t = q_ref.shape[1]
    c = GLA_CHUNK
    half = c // 2
    gate_logits = _dot(small_ref[0].astype(BF16), wg_ref[...]) + bg_ref[...]
    log_a = _log_sigmoid(gate_logits) * (1.0 / GLA_TAU)

    row = lax.broadcasted_iota(jnp.int32, (t, t), 0)
    col = lax.broadcasted_iota(jnp.int32, (t, t), 1)
    same_chunk = _div_pow2(row, c) == _div_pow2(col, c)
    tri = jnp.where(same_chunk & (col <= row), 1.0, 0.0).astype(BF16)
    cum_all = _ones_dot_f32(tri, log_a)

    crow = lax.broadcasted_iota(jnp.int32, (c, GLA_K_WIDTH), 0)
    first = crow < half
    lane = lax.broadcasted_iota(jnp.int32, (c, GLA_K_WIDTH), 1)
    head_lanes = [_div_pow2(lane, GLA_KEY_DIM) == h for h in range(GLA_HEADS)]
    sr = lax.broadcasted_iota(jnp.int32, (GLA_HEADS * c, c), 0) & (c - 1)
    sc = lax.broadcasted_iota(jnp.int32, (GLA_HEADS * c, c), 1)
    near = (_div_pow2(sr, half) == _div_pow2(sc, half)) & (sc <= sr)
    srow = lax.broadcasted_iota(jnp.int32, state_ref.shape, 0)
    slane = lax.broadcasted_iota(jnp.int32, state_ref.shape, 1)
    own_head = _div_pow2(srow, GLA_VAL_DIM) == _div_pow2(slane, GLA_KEY_DIM)

    def stack_heads(x):
        return jnp.concatenate([jnp.where(m, x, 0.0) for m in head_lanes], axis=0).astype(BF16)

    for ci in range(t // c):
        r0 = ci * c
        cum = cum_all[r0:r0 + c]
        q = q_ref[0, r0:r0 + c, :].astype(F32)
        k = k_ref[0, r0:r0 + c, :].astype(F32)
        v = v_ref[0, r0:r0 + c, :]
        last = cum[c - 1:c]
        ref_far = cum[half - 1:half]
        ref_near = jnp.where(first, cum[half // 2:half // 2 + 1], cum[half + half // 2:half + half // 2 + 1])
        q_far = jnp.where(first, 0.0, q * jnp.exp(cum - ref_far))
        k_far = jnp.where(first, k * jnp.exp(ref_far - cum), 0.0)
        q_near = q * jnp.exp(cum - ref_near)
        k_near = k * jnp.exp(ref_near - cum)
        scores = (_dot_nt(stack_heads(q_far), k_far.astype(BF16))
                  + jnp.where(near, _dot_nt(stack_heads(q_near), k_near.astype(BF16)), 0.0))
        state = state_ref[...]
        o = _dot_nt((q * jnp.exp(cum)).astype(BF16), state.astype(BF16))
        p = scores.astype(BF16)
        for h in range(GLA_HEADS):
            lo, hi = h * GLA_VAL_DIM, (h + 1) * GLA_VAL_DIM
            o_h = o[:, lo:hi] + _dot(p[h * c:(h + 1) * c], v[:, lo:hi])
            y = o_h * lax.rsqrt(jnp.mean(o_h * o_h, axis=-1, keepdims=True) + RMS_EPS) * gn_ref[:, lo:hi]
            o_ref[0, r0:r0 + c, lo:hi] = y.astype(o_ref.dtype)
        k_tail = (k * jnp.exp(last - cum)).astype(BF16)
        update = _dot_tn(v, k_tail)
        state_ref[...] = state * jnp.exp(last) + jnp.where(own_head, update, 0.0)


def _gla(gq, gk, gv, small, w_gate_pad, b_gate, g_norm):
    b, s, _ = gq.shape
    t = min(GLA_TOKENS, s)
    tok = lambda width: pl.BlockSpec((1, t, width), lambda bi, i: (bi, i, 0))
    block_bytes = t * (2 * GLA_K_WIDTH * 2 + 2 * GLA_V_WIDTH * 2 + SMALL_WIDTH * 4)
    state_bytes = GLA_V_WIDTH * GLA_K_WIDTH * 4
    return pl.pallas_call(
        _gla_kernel,
        grid=(b, s // t),
        in_specs=[tok(GLA_K_WIDTH), tok(GLA_K_WIDTH), tok(GLA_V_WIDTH), tok(SMALL_WIDTH),
                  _resident(w_gate_pad.shape), _resident((1, GLA_K_WIDTH)), _resident((1, GLA_V_WIDTH))],
        out_specs=tok(GLA_V_WIDTH),
        out_shape=jax.ShapeDtypeStruct((b, s, GLA_V_WIDTH), BF16),
        scratch_shapes=[pltpu.VMEM((GLA_V_WIDTH, GLA_K_WIDTH), F32)],
        compiler_params=pltpu.CompilerParams(
            dimension_semantics=("parallel", "arbitrary"),
            vmem_limit_bytes=_vmem_limit(block_bytes, state_bytes, 8 * state_bytes + 16 * t * GLA_K_WIDTH * 4)),
        name="gla",
    )(gq, gk, gv, small, w_gate_pad, b_gate, g_norm)


_B_POOL = 0
_B_Z = _B_POOL + POOL_WIDTH
_B_MERGE = _B_Z + FOX_WIDTH + GLA_V_WIDTH + POOL_WIDTH
_B_WIDTH = _B_MERGE + N_BRANCHES * D_MODEL


def _merge_kernel(x_ref, mod_ref, gpre_ref, gpost_ref, ofox_ref, ogla_ref, wb_ref, wpool_ref, pscale_ref,
                  wbr_ref, wout_ref, o_ref, u_ref):
    t = x_ref.shape[1]
    i = pl.program_id(1)
    x = x_ref[0]
    mod = mod_ref[0]
    h = _modulated_norm(x, gpre_ref[...], mod)

    @pl.when(i == 0)
    def _():
        u_ref[0:POOL_HALO, :] = jnp.zeros((POOL_HALO, POOL_WIDTH), F32)

    @pl.when(i > 0)
    def _():
        u_ref[0:POOL_HALO, :] = u_ref[t:t + POOL_HALO, :]

    u = _dot(h, wb_ref[:, _B_POOL:_B_Z])
    u_ref[POOL_HALO:POOL_HALO + t, :] = u
    count = (i * t + 1 + lax.broadcasted_iota(jnp.int32, (t, 1), 0)).astype(F32)
    pooled = []
    for g, w in enumerate(POOL_WINDOWS):
        lo, hi = g * POOL_GROUP_DIM, (g + 1) * POOL_GROUP_DIM
        window = u[:, lo:hi]
        for back in range(1, w):
            window = window + u_ref[pl.ds(POOL_HALO - back, t), lo:hi]
        mean = window / jnp.minimum(count, float(w))
        diff = (mean - u[:, lo:hi]).astype(BF16)
        pooled.append(_dot(diff, wpool_ref[g]) * pscale_ref[:, lo:hi])
    o_pool = jnp.concatenate(pooled, axis=-1)

    branches = (ofox_ref[0].astype(F32), ogla_ref[0].astype(F32), o_pool)
    merged = jnp.zeros((t, D_MODEL), F32)
    for br, o_br in enumerate(branches):
        z = _dot(h, wb_ref[:, _B_Z + br * FOX_WIDTH:_B_Z + (br + 1) * FOX_WIDTH])
        y = _dot((o_br * _silu(z)).astype(BF16), wbr_ref[br])
        m = _dot(h, wb_ref[:, _B_MERGE + br * D_MODEL:_B_MERGE + (br + 1) * D_MODEL])
        merged = merged + _sigmoid(m) * y
    out = _dot(merged.astype(BF16), wout_ref[...])
    out = out * lax.rsqrt(jnp.mean(out * out, axis=-1, keepdims=True) + RMS_EPS) * gpost_ref[...]
    gate = mod[:, 2 * D_MODEL:3 * D_MODEL]
    o_ref[0] = x + gate * out


def _merge(x, mod, g_pre, g_post, o_fox, o_gla, w_b, w_pool, pool_scale, w_br, w_out):
    b, s, d = x.shape
    t = min(MERGE_TOKENS, s)
    tok = lambda width: pl.BlockSpec((1, t, width), lambda bi, i: (bi, i, 0))
    block_bytes = 2 * t * d * 4 + 2 * t * FOX_WIDTH * 2
    weight_bytes = (w_b.size + w_pool.size + w_br.size + w_out.size) * 2
    return pl.pallas_call(
        _merge_kernel,
        grid=(b, s // t),
        in_specs=[tok(d),
                  pl.BlockSpec((1, 1, 3 * d), lambda bi, i: (bi, 0, 0)),
                  _resident((1, d)), _resident((1, d)),
                  tok(FOX_WIDTH), tok(GLA_V_WIDTH),
                  _resident(w_b.shape), _resident(w_pool.shape), _resident((1, POOL_WIDTH)),
                  _resident(w_br.shape), _resident(w_out.shape)],
        out_specs=tok(d),
        out_shape=jax.ShapeDtypeStruct((b, s, d), F32),
        scratch_shapes=[pltpu.VMEM((POOL_HALO + t, POOL_WIDTH), F32)],
        compiler_params=pltpu.CompilerParams(
            dimension_semantics=("parallel", "arbitrary"),
            vmem_limit_bytes=_vmem_limit(block_bytes, weight_bytes, 12 * t * d * 4)),
        name="merge",
    )(x, mod, g_pre, g_post, o_fox, o_gla, w_b, w_pool, pool_scale, w_br, w_out)


def _layer(x, mod, g_pre, g_post, w_in, b_forget, w_gla_gate, b_gla_gate, g_gla_norm, w_pool, pool_scale,
           w_br_fox, w_br_gla, w_br_pool, w_out):
    b, s, d = x.shape
    seg = lambda n: w_in[:, _OFF[n]:_OFF[n + 1]]
    w_a = jnp.concatenate([seg(0) * FOX_HEAD_DIM ** -0.5, seg(2), seg(4) * GLA_KEY_DIM ** -0.5, seg(5), seg(6)],
                          axis=1).astype(BF16)
    w_kt = seg(1).T.astype(BF16)
    pad = jnp.zeros((d, SMALL_WIDTH - FOX_HEADS - GLA_GATE_RANK), w_in.dtype)
    w_small = jnp.concatenate([seg(3), seg(7), pad], axis=1).astype(BF16)
    w_b = w_in[:, _OFF[8]:_OFF[13]].astype(BF16)
    b_forget_pad = jnp.zeros((1, SMALL_WIDTH), F32).at[0, :FOX_HEADS].set(b_forget)
    w_gate_pad = jnp.zeros((SMALL_WIDTH, GLA_K_WIDTH), F32).at[FOX_HEADS:FOX_HEADS + GLA_GATE_RANK].set(
        w_gla_gate).astype(BF16)
    w_br = jnp.stack([w_br_fox, w_br_gla, w_br_pool]).astype(BF16)
    mod3 = mod.reshape(b, 1, 3 * d)
    g_pre = g_pre.reshape(1, d)

    fq, fv, gq, gk, gv, fkt, small = _in_proj(x, mod3, g_pre, w_a, w_kt, w_small)
    fcol, frow = _forget_cumsum(small, b_forget_pad)
    o_fox = _fox_attention(fq, fkt, fv, fcol, frow)
    o_gla = _gla(gq, gk, gv, small, w_gate_pad, b_gla_gate.reshape(1, GLA_K_WIDTH),
                 g_gla_norm.reshape(1, GLA_V_WIDTH))
    return _merge(x, mod3, g_pre, g_post.reshape(1, d), o_fox, o_gla, w_b, w_pool.astype(BF16),
                  pool_scale.reshape(1, POOL_WIDTH), w_br, w_out.astype(BF16))


def kernel(x, c, w_ada, b_ada, g_pre, g_post, w_in, b_forget, w_gla_gate, b_gla_gate, g_gla_norm, w_pool,
           pool_scale, w_br_fox, w_br_gla, w_br_pool, w_out):
    mods = _adaln_mod(c, w_ada, b_ada)
    h = x
    for i in range(w_in.shape[0]):
        h = _layer(h, mods[i], g_pre[i], g_post[i], w_in[i], b_forget[i], w_gla_gate[i], b_gla_gate[i],
                   g_gla_norm[i], w_pool[i], pool_scale[i], w_br_fox[i], w_br_gla[i], w_br_pool[i], w_out[i])
    return h
```

```python
import functools

import numpy as np
import jax
import jax.numpy as jnp
from jax import lax
from jax.experimental import pallas as pl
from jax.experimental.pallas import tpu as pltpu

F32 = jnp.float32
BF16 = jnp.bfloat16

D_MODEL = 1024
FOX_HEADS = 8
FOX_HEAD_DIM = 64
FOX_WIDTH = FOX_HEADS * FOX_HEAD_DIM
GLA_HEADS = 4
GLA_KEY_DIM = 64
GLA_VAL_DIM = 128
GLA_K_WIDTH = GLA_HEADS * GLA_KEY_DIM
GLA_V_WIDTH = GLA_HEADS * GLA_VAL_DIM
GLA_GATE_RANK = 16
GLA_TAU = 16.0
POOL_WINDOWS = (2, 4, 8, 16)
POOL_GROUP_DIM = 128
POOL_WIDTH = len(POOL_WINDOWS) * POOL_GROUP_DIM
N_BRANCHES = 3
RMS_EPS = 1e-6
IN_SPLITS = (FOX_WIDTH, FOX_WIDTH, FOX_WIDTH, FOX_HEADS,
             GLA_K_WIDTH, GLA_K_WIDTH, GLA_V_WIDTH, GLA_GATE_RANK,
             POOL_WIDTH, FOX_WIDTH, GLA_V_WIDTH, POOL_WIDTH, N_BRANCHES * D_MODEL)
_OFF = np.concatenate([[0], np.cumsum(IN_SPLITS)]).tolist()

LANES = 128
SMALL_WIDTH = LANES
POOL_HALO = 16
MASK_VALUE = -1e30
V7X_VMEM_BYTES = 64 * 1024 * 1024

IN_TOKENS = 512
FORGET_TOKENS = 512
FOX_TILE = 256
GLA_TOKENS = 256
GLA_CHUNK = 64
MERGE_TOKENS = 256


def _vmem_limit(block_bytes, resident_bytes, temp_bytes):
    need = 2 * block_bytes + resident_bytes + temp_bytes
    return int(min(max(need, 16 * 1024 * 1024), V7X_VMEM_BYTES - 8 * 1024 * 1024))


def _resident(shape):
    zeros = (0,) * len(shape)
    return pl.BlockSpec(shape, lambda *_: zeros, pipeline_mode=pl.Buffered(1))


def _div_pow2(i, n):
    shift = n.bit_length() - 1
    assert 1 << shift == n
    return jnp.right_shift(i, shift)


def _log_sigmoid(x):
    return jnp.minimum(x, 0.0) - jnp.log1p(jnp.exp(-jnp.abs(x)))


def _sigmoid(x):
    return jax.nn.sigmoid(x)


def _silu(x):
    return x * _sigmoid(x)


def _dot(a, b):
    return jnp.dot(a, b, preferred_element_type=F32)


def _dot_nt(a, b):
    return lax.dot_general(a, b, (((1,), (1,)), ((), ())), preferred_element_type=F32)


def _dot_tn(a, b):
    return lax.dot_general(a, b, (((0,), (0,)), ((), ())), preferred_element_type=F32)


def _split3(x):
    hi = x.astype(BF16)
    r1 = x - hi.astype(F32)
    mid = r1.astype(BF16)
    lo = (r1 - mid.astype(F32)).astype(BF16)
    return hi, mid, lo


def _ones_dot_f32(ones_bf16, x):
    hi, mid, lo = _split3(x)
    return _dot(ones_bf16, hi) + _dot(ones_bf16, mid) + _dot(ones_bf16, lo)


def _modulated_norm(x, g, mod):
    y = x * lax.rsqrt(jnp.mean(x * x, axis=-1, keepdims=True) + RMS_EPS) * g
    shift = mod[:, 0:D_MODEL]
    scale = mod[:, D_MODEL:2 * D_MODEL]
    return (y * (1.0 + scale) + shift).astype(BF16)


def _mod_kernel(c_ref, w_ref, b_ref, o_ref):
    c = c_ref[...]
    o_ref[0] = jnp.dot(_silu(c), w_ref[0], preferred_element_type=F32,
                       precision=lax.Precision.HIGHEST) + b_ref[0]


def _adaln_mod(c, w_ada, b_ada):
    depth, d, d3 = w_ada.shape
    b = c.shape[0]
    n_col = d3 // d
    return pl.pallas_call(
        _mod_kernel,
        grid=(depth, n_col),
        in_specs=[pl.BlockSpec((b, d), lambda l, j: (0, 0)),
                  pl.BlockSpec((1, d, d), lambda l, j: (l, 0, j)),
                  pl.BlockSpec((1, 1, d), lambda l, j: (l, 0, j))],
        out_specs=pl.BlockSpec((1, b, d), lambda l, j: (l, 0, j)),
        out_shape=jax.ShapeDtypeStruct((depth, b, d3), F32),
        compiler_params=pltpu.CompilerParams(
            dimension_semantics=("parallel", "parallel"),
            vmem_limit_bytes=_vmem_limit(d * d * 4 + 2 * b * d * 4, 0, 4 * b * d * 4)),
        name="adaln_mod",
    )(c, w_ada, b_ada.reshape(depth, 1, d3))


_A_FOX_K = 0
_A_GLA_Q = _A_FOX_K + FOX_WIDTH
_A_GLA_K = _A_GLA_Q + GLA_K_WIDTH
_A_GLA_V = _A_GLA_K + GLA_K_WIDTH
_A_WIDTH = _A_GLA_V + GLA_V_WIDTH


def _in_proj_kernel(x_ref, mod_ref, g_ref, wa_ref, wt_ref, ws_ref,
                    fk_ref, gq_ref, gk_ref, gv_ref, fqt_ref, fvt_ref, small_ref):
    h = _modulated_norm(x_ref[0], g_ref[...], mod_ref[0])
    fk_ref[0] = _dot(h, wa_ref[:, _A_FOX_K:_A_GLA_Q]).astype(BF16)
    gq_ref[0] = _dot(h, wa_ref[:, _A_GLA_Q:_A_GLA_K]).astype(BF16)
    gk_ref[0] = _dot(h, wa_ref[:, _A_GLA_K:_A_GLA_V]).astype(BF16)
    gv_ref[0] = _dot(h, wa_ref[:, _A_GLA_V:_A_WIDTH]).astype(BF16)
    fqt_ref[0] = _dot_nt(wt_ref[0:FOX_WIDTH, :], h).astype(BF16)
    fvt_ref[0] = _dot_nt(wt_ref[FOX_WIDTH:2 * FOX_WIDTH, :], h).astype(BF16)
    small_ref[0] = _dot(h, ws_ref[...])


def _in_proj(x, mod, g_pre, w_a, w_t, w_small):
    b, s, d = x.shape
    t = min(IN_TOKENS, s)
    tok = lambda width: pl.BlockSpec((1, t, width), lambda bi, i: (bi, i, 0))
    tok_t = pl.BlockSpec((1, FOX_WIDTH, t), lambda bi, i: (bi, 0, i))
    out_shapes = (
        jax.ShapeDtypeStruct((b, s, FOX_WIDTH), BF16),
        jax.ShapeDtypeStruct((b, s, GLA_K_WIDTH), BF16),
        jax.ShapeDtypeStruct((b, s, GLA_K_WIDTH), BF16),
        jax.ShapeDtypeStruct((b, s, GLA_V_WIDTH), BF16),
        jax.ShapeDtypeStruct((b, FOX_WIDTH, s), BF16),
        jax.ShapeDtypeStruct((b, FOX_WIDTH, s), BF16),
        jax.ShapeDtypeStruct((b, s, SMALL_WIDTH), F32),
    )
    out_specs = (tok(FOX_WIDTH), tok(GLA_K_WIDTH), tok(GLA_K_WIDTH), tok(GLA_V_WIDTH), tok_t, tok_t,
                 tok(SMALL_WIDTH))
    n_out = _A_WIDTH + 2 * FOX_WIDTH
    block_bytes = t * d * 4 + t * n_out * 2 + t * SMALL_WIDTH * 4
    weight_bytes = d * (n_out + SMALL_WIDTH) * 2
    return pl.pallas_call(
        _in_proj_kernel,
        grid=(b, s // t),
        in_specs=[tok(d),
                  pl.BlockSpec((1, 1, 3 * d), lambda bi, i: (bi, 0, 0)),
                  _resident((1, d)), _resident(w_a.shape), _resident(w_t.shape), _resident(w_small.shape)],
        out_specs=out_specs,
        out_shape=out_shapes,
        compiler_params=pltpu.CompilerParams(
            dimension_semantics=("parallel", "parallel"),
            vmem_limit_bytes=_vmem_limit(block_bytes, weight_bytes, 6 * t * d * 4)),
        name="in_proj",
    )(x, mod, g_pre, w_a, w_t, w_small)


BIAS_SLOTS = 16


def _bias_placement():
    pk = np.zeros((3, SMALL_WIDTH, FOX_WIDTH), np.float32)
    k_const = np.zeros((1, FOX_WIDTH), np.float32)
    pq = np.zeros((3, FOX_HEADS * BIAS_SLOTS, SMALL_WIDTH), np.float32)
    q_const = np.zeros((FOX_HEADS * BIAS_SLOTS, 1), np.float32)
    for head in range(FOX_HEADS):
        pair, odd = divmod(head, 2)
        for part in range(3):
            pk[part, head, pair * LANES + 3 + 3 * odd + part] = -1.0
            pq[part, head * BIAS_SLOTS + part, head] = 1.0
            q_const[head * BIAS_SLOTS + 3 + 3 * odd + part, 0] = 1.0
            k_const[0, pair * LANES + part] = 1.0
    return pk, k_const, pq, q_const


def _forget_kernel(small_ref, b_ref, pk_ref, kc_ref, pq_ref, qc_ref, kx_ref, qx_ref, carry_ref):
    @pl.when(pl.program_id(1) == 0)
    def _():
        carry_ref[...] = jnp.zeros_like(carry_ref)

    t = small_ref.shape[1]
    log_f = _log_sigmoid(small_ref[0] + b_ref[...])
    row = lax.broadcasted_iota(jnp.int32, (t, t), 0)
    col = lax.broadcasted_iota(jnp.int32, (t, t), 1)
    tri = jnp.where(col <= row, 1.0, 0.0).astype(BF16)
    cs = _ones_dot_f32(tri, log_f) + carry_ref[...]
    carry_ref[...] = cs[t - 1:t, :]
    parts = _split3(cs)
    kx = kc_ref[...]
    qx = qc_ref[...]
    for part in range(3):
        kx = kx + _dot(parts[part], pk_ref[part])
        qx = qx + _dot_nt(pq_ref[part], parts[part])
    kx_ref[0] = kx.astype(BF16)
    qx_ref[0] = qx.astype(BF16)


def _forget_cumsum(small, b_forget_pad):
    b, s, _ = small.shape
    t = min(FORGET_TOKENS, s)
    pk, k_const, pq, q_const = _bias_placement()
    q_rows = FOX_HEADS * BIAS_SLOTS
    return pl.pallas_call(
        _forget_kernel,
        grid=(b, s // t),
        in_specs=[pl.BlockSpec((1, t, SMALL_WIDTH), lambda bi, i: (bi, i, 0)),
                  _resident((1, SMALL_WIDTH)), _resident(pk.shape), _resident(k_const.shape),
                  _resident(pq.shape), _resident(q_const.shape)],
        out_specs=(pl.BlockSpec((1, t, FOX_WIDTH), lambda bi, i: (bi, i, 0)),
                   pl.BlockSpec((1, q_rows, t), lambda bi, i: (bi, 0, i))),
        out_shape=(jax.ShapeDtypeStruct((b, s, FOX_WIDTH), BF16),
                   jax.ShapeDtypeStruct((b, q_rows, s), BF16)),
        scratch_shapes=[pltpu.VMEM((1, SMALL_WIDTH), F32)],
        compiler_params=pltpu.CompilerParams(
            dimension_semantics=("parallel", "arbitrary"),
            vmem_limit_bytes=_vmem_limit(t * (SMALL_WIDTH * 4 + FOX_WIDTH * 2 + q_rows * 2), pk.size * 2 + pq.size * 2,
                                         2 * t * t * 4 + 8 * t * FOX_WIDTH * 4)),
        name="forget_cumsum",
    )(small, b_forget_pad, jnp.asarray(pk, BF16), jnp.asarray(k_const), jnp.asarray(pq, BF16), jnp.asarray(q_const))


V_ROWS = FOX_HEAD_DIM + 16


def _fox_kernel(qt_ref, qx_ref, k_ref, kx_ref, vt_ref, o_ref, s_ref, p_ref):
    t = qt_ref.shape[2]
    i = pl.program_id(2)
    qt = qt_ref[0]
    row = lax.broadcasted_iota(jnp.int32, (LANES, t), 0)
    low = row < FOX_HEAD_DIM
    zero = jnp.zeros_like(qt)
    pad = jnp.zeros((LANES - BIAS_SLOTS, t), BF16)
    q_aug = tuple(
        jnp.concatenate([jnp.where(low, qt, zero) if h == 0 else jnp.where(low, zero, qt),
                         qx_ref[0, h * BIAS_SLOTS:(h + 1) * BIAS_SLOTS, :], pad], axis=0)
        for h in range(2))
    ones = jnp.ones((V_ROWS - FOX_HEAD_DIM, t), BF16)

    def logits(j, masked):
        k0 = pl.multiple_of(j * t, t)
        k_aug = jnp.concatenate([k_ref[0, pl.ds(k0, t), :], kx_ref[0, pl.ds(k0, t), :]], axis=1)
        out = []
        for h in range(2):
            s = _dot(k_aug, q_aug[h])
            if masked:
                r = lax.broadcasted_iota(jnp.int32, (t, t), 0)
                c = lax.broadcasted_iota(jnp.int32, (t, t), 1)
                s = jnp.where(r <= c, s, MASK_VALUE)
            out.append(s)
        return out

    def weighted_values(j, h, p):
        k0 = pl.multiple_of(j * t, t)
        v_aug = jnp.concatenate([vt_ref[0, h * FOX_HEAD_DIM:(h + 1) * FOX_HEAD_DIM, pl.ds(k0, t)], ones], axis=0)
        return _dot(v_aug, p)

    def stage_logits(j, masked):
        maxes = []
        for h, s in enumerate(logits(j, masked)):
            s_ref[h] = s
            maxes.append(jnp.max(s, axis=0, keepdims=True))
        return maxes

    def stage_softmax(h, s_max, m):
        m_new = jnp.maximum(m, s_max)
        p_ref[h] = jnp.exp(s_ref[h] - m_new).astype(BF16)
        return m_new, jnp.exp(m - m_new)

    m0 = jnp.full((1, t), MASK_VALUE, F32)
    s_max = stage_logits(i, True)
    stats = [stage_softmax(h, s_max[h], m0) for h in range(2)]
    s_max = stage_logits(0, False)
    heads = tuple(stats[h] + (s_max[h], jnp.zeros((V_ROWS, t), F32)) for h in range(2))

    def trip(k, carry):
        prev, heads = carry
        pv = [weighted_values(prev, h, p_ref[h]) for h in range(2)]
        stats = [stage_softmax(h, heads[h][2], heads[h][0]) for h in range(2)]
        s_max = stage_logits(jnp.minimum(k + 1, i - 1), False)
        return k, tuple(stats[h] + (s_max[h], heads[h][1] * heads[h][3] + pv[h]) for h in range(2))

    prev, heads = lax.fori_loop(0, i, trip, (i, heads))
    o_heads = []
    for h in range(2):
        _, alpha, _, acc = heads[h]
        acc = alpha * acc + weighted_values(prev, h, p_ref[h])
        o_heads.append(acc[0:FOX_HEAD_DIM] * (1.0 / acc[FOX_HEAD_DIM:FOX_HEAD_DIM + 1]))
    o_ref[0] = jnp.concatenate(o_heads, axis=0).T.astype(o_ref.dtype)


def _fox_attention(fqt, qx, fk, kx, fvt):
    b, s, _ = fk.shape
    tq = min(FOX_TILE, s)
    pairs = FOX_WIDTH // LANES
    block_bytes = (LANES + 2 * BIAS_SLOTS) * tq * 2 + tq * LANES * 2 + 3 * s * LANES * 2
    return pl.pallas_call(
        _fox_kernel,
        grid=(b, pairs, s // tq),
        in_specs=[pl.BlockSpec((1, LANES, tq), lambda bi, hp, i: (bi, hp, i)),
                  pl.BlockSpec((1, 2 * BIAS_SLOTS, tq), lambda bi, hp, i: (bi, hp, i)),
                  pl.BlockSpec((1, s, LANES), lambda bi, hp, i: (bi, 0, hp)),
                  pl.BlockSpec((1, s, LANES), lambda bi, hp, i: (bi, 0, hp)),
                  pl.BlockSpec((1, LANES, s), lambda bi, hp, i: (bi, hp, 0))],
        out_specs=pl.BlockSpec((1, tq, LANES), lambda bi, hp, i: (bi, i, hp)),
        out_shape=jax.ShapeDtypeStruct((b, s, FOX_WIDTH), BF16),
        scratch_shapes=[pltpu.VMEM((2, tq, tq), F32), pltpu.VMEM((2, tq, tq), BF16)],
        compiler_params=pltpu.CompilerParams(
            dimension_semantics=("parallel", "parallel", "parallel"),
            vmem_limit_bytes=_vmem_limit(block_bytes, 0, 16 * tq * tq * 4)),
        name="fox_attention",
    )(fqt, qx, fk, kx, fvt)


def _gla_kernel(q_ref, k_ref, v_ref, small_ref, wg_ref, bg_ref, gn_ref, o_ref, state_ref):
    @pl.when(pl.program_id(1) == 0)
    def _():
        state_ref[...] = jnp.zeros_like(state_ref)

    t = q_ref.shape[1]
    c = GLA_CHUNK
    half = c // 2
    gate_logits = _dot(small_ref[0].astype(BF16), wg_ref[...]) + bg_ref[...]
    log_a = _log_sigmoid(gate_logits) * (1.0 / GLA_TAU)

    row = lax.broadcasted_iota(jnp.int32, (t, t), 0)
    col = lax.broadcasted_iota(jnp.int32, (t, t), 1)
    same_chunk = _div_pow2(row, c) == _div_pow2(col, c)
    tri = jnp.where(same_chunk & (col <= row), 1.0, 0.0).astype(BF16)
    cum_all = _ones_dot_f32(tri, log_a)

    crow = lax.broadcasted_iota(jnp.int32, (c, GLA_K_WIDTH), 0)
    first = crow < half
    lane = lax.broadcasted_iota(jnp.int32, (c, GLA_K_WIDTH), 1)
    head_lanes = [_div_pow2(lane, GLA_KEY_DIM) == h for h in range(GLA_HEADS)]
    sr = lax.broadcasted_iota(jnp.int32, (GLA_HEADS * c, c), 0) & (c - 1)
    sc = lax.broadcasted_iota(jnp.int32, (GLA_HEADS * c, c), 1)
    near = (_div_pow2(sr, half) == _div_pow2(sc, half)) & (sc <= sr)
    srow = lax.broadcasted_iota(jnp.int32, state_ref.shape, 0)
    slane = lax.broadcasted_iota(jnp.int32, state_ref.shape, 1)
    own_head = _div_pow2(srow, GLA_VAL_DIM) == _div_pow2(slane, GLA_KEY_DIM)

    def stack_heads(x):
        return jnp.concatenate([jnp.where(m, x, 0.0) for m in head_lanes], axis=0).astype(BF16)

    for ci in range(t // c):
        r0 = ci * c
        cum = cum_all[r0:r0 + c]
        q = q_ref[0, r0:r0 + c, :].astype(F32)
        k = k_ref[0, r0:r0 + c, :].astype(F32)
        v = v_ref[0, r0:r0 + c, :]
        last = cum[c - 1:c]
        ref_far = cum[half - 1:half]
        ref_near = jnp.where(first, cum[half // 2:half // 2 + 1], cum[half + half // 2:half + half // 2 + 1])
        q_far = jnp.where(first, 0.0, q * jnp.exp(cum - ref_far))
        k_far = jnp.where(first, k * jnp.exp(ref_far - cum), 0.0)
        q_near = q * jnp.exp(cum - ref_near)
        k_near = k * jnp.exp(ref_near - cum)
        scores = (_dot_nt(stack_heads(q_far), k_far.astype(BF16))
                  + jnp.where(near, _dot_nt(stack_heads(q_near), k_near.astype(BF16)), 0.0))
        state = state_ref[...]
        o = _dot_nt((q * jnp.exp(cum)).astype(BF16), state.astype(BF16))
        p = scores.astype(BF16)
        for h in range(GLA_HEADS):
            lo, hi = h * GLA_VAL_DIM, (h + 1) * GLA_VAL_DIM
            o_h = o[:, lo:hi] + _dot(p[h * c:(h + 1) * c], v[:, lo:hi])
            y = o_h * lax.rsqrt(jnp.mean(o_h * o_h, axis=-1, keepdims=True) + RMS_EPS) * gn_ref[:, lo:hi]
            o_ref[0, r0:r0 + c, lo:hi] = y.astype(o_ref.dtype)
        k_tail = (k * jnp.exp(last - cum)).astype(BF16)
        update = _dot_tn(v, k_tail)
        state_ref[...] = state * jnp.exp(last) + jnp.where(own_head, update, 0.0)


def _gla(gq, gk, gv, small, w_gate_pad, b_gate, g_norm):
    b, s, _ = gq.shape
    t = min(GLA_TOKENS, s)
    tok = lambda width: pl.BlockSpec((1, t, width), lambda bi, i: (bi, i, 0))
    block_bytes = t * (2 * GLA_K_WIDTH * 2 + 2 * GLA_V_WIDTH * 2 + SMALL_WIDTH * 4)
    state_bytes = GLA_V_WIDTH * GLA_K_WIDTH * 4
    return pl.pallas_call(
        _gla_kernel,
        grid=(b, s // t),
        in_specs=[tok(GLA_K_WIDTH), tok(GLA_K_WIDTH), tok(GLA_V_WIDTH), tok(SMALL_WIDTH),
                  _resident(w_gate_pad.shape), _resident((1, GLA_K_WIDTH)), _resident((1, GLA_V_WIDTH))],
        out_specs=tok(GLA_V_WIDTH),
        out_shape=jax.ShapeDtypeStruct((b, s, GLA_V_WIDTH), BF16),
        scratch_shapes=[pltpu.VMEM((GLA_V_WIDTH, GLA_K_WIDTH), F32)],
        compiler_params=pltpu.CompilerParams(
            dimension_semantics=("parallel", "arbitrary"),
            vmem_limit_bytes=_vmem_limit(block_bytes, state_bytes, 8 * state_bytes + 16 * t * GLA_K_WIDTH * 4)),
        name="gla",
    )(gq, gk, gv, small, w_gate_pad, b_gate, g_norm)


_B_POOL = 0
_B_Z = _B_POOL + POOL_WIDTH
_B_MERGE = _B_Z + FOX_WIDTH + GLA_V_WIDTH + POOL_WIDTH
_B_WIDTH = _B_MERGE + N_BRANCHES * D_MODEL


def _merge_kernel(x_ref, mod_ref, gpre_ref, gpost_ref, ofox_ref, ogla_ref, wb_ref, wpool_ref, pscale_ref,
                  wbr_ref, wout_ref, o_ref, u_ref):
    t = x_ref.shape[1]
    i = pl.program_id(1)
    x = x_ref[0]
    mod = mod_ref[0]
    h = _modulated_norm(x, gpre_ref[...], mod)

    @pl.when(i == 0)
    def _():
        u_ref[0:POOL_HALO, :] = jnp.zeros((POOL_HALO, POOL_WIDTH), F32)

    @pl.when(i > 0)
    def _():
        u_ref[0:POOL_HALO, :] = u_ref[t:t + POOL_HALO, :]

    u = _dot(h, wb_ref[:, _B_POOL:_B_Z])
    u_ref[POOL_HALO:POOL_HALO + t, :] = u
    count = (i * t + 1 + lax.broadcasted_iota(jnp.int32, (t, 1), 0)).astype(F32)
    pooled = []
    for g, w in enumerate(POOL_WINDOWS):
        lo, hi = g * POOL_GROUP_DIM, (g + 1) * POOL_GROUP_DIM
        window = u[:, lo:hi]
        for back in range(1, w):
            window = window + u_ref[pl.ds(POOL_HALO - back, t), lo:hi]
        mean = window / jnp.minimum(count, float(w))
        diff = (mean - u[:, lo:hi]).astype(BF16)
        pooled.append(_dot(diff, wpool_ref[g]) * pscale_ref[:, lo:hi])
    o_pool = jnp.concatenate(pooled, axis=-1)

    branches = (ofox_ref[0].astype(F32), ogla_ref[0].astype(F32), o_pool)
    merged = jnp.zeros((t, D_MODEL), F32)
    for br, o_br in enumerate(branches):
        z = _dot(h, wb_ref[:, _B_Z + br * FOX_WIDTH:_B_Z + (br + 1) * FOX_WIDTH])
        y = _dot((o_br * _silu(z)).astype(BF16), wbr_ref[br])
        m = _dot(h, wb_ref[:, _B_MERGE + br * D_MODEL:_B_MERGE + (br + 1) * D_MODEL])
        merged = merged + _sigmoid(m) * y
    out = _dot(merged.astype(BF16), wout_ref[...])
    out = out * lax.rsqrt(jnp.mean(out * out, axis=-1, keepdims=True) + RMS_EPS) * gpost_ref[...]
    gate = mod[:, 2 * D_MODEL:3 * D_MODEL]
    o_ref[0] = x + gate * out


def _merge(x, mod, g_pre, g_post, o_fox, o_gla, w_b, w_pool, pool_scale, w_br, w_out):
    b, s, d = x.shape
    t = min(MERGE_TOKENS, s)
    tok = lambda width: pl.BlockSpec((1, t, width), lambda bi, i: (bi, i, 0))
    block_bytes = 2 * t * d * 4 + 2 * t * FOX_WIDTH * 2
    weight_bytes = (w_b.size + w_pool.size + w_br.size + w_out.size) * 2
    return pl.pallas_call(
        _merge_kernel,
        grid=(b, s // t),
        in_specs=[tok(d),
                  pl.BlockSpec((1, 1, 3 * d), lambda bi, i: (bi, 0, 0)),
                  _resident((1, d)), _resident((1, d)),
                  tok(FOX_WIDTH), tok(GLA_V_WIDTH),
                  _resident(w_b.shape), _resident(w_pool.shape), _resident((1, POOL_WIDTH)),
                  _resident(w_br.shape), _resident(w_out.shape)],
        out_specs=tok(d),
        out_shape=jax.ShapeDtypeStruct((b, s, d), F32),
        scratch_shapes=[pltpu.VMEM((POOL_HALO + t, POOL_WIDTH), F32)],
        compiler_params=pltpu.CompilerParams(
            dimension_semantics=("parallel", "arbitrary"),
            vmem_limit_bytes=_vmem_limit(block_bytes, weight_bytes, 12 * t * d * 4)),
        name="merge",
    )(x, mod, g_pre, g_post, o_fox, o_gla, w_b, w_pool, pool_scale, w_br, w_out)


def _layer(x, mod, g_pre, g_post, w_in, b_forget, w_gla_gate, b_gla_gate, g_gla_norm, w_pool, pool_scale,
           w_br_fox, w_br_gla, w_br_pool, w_out):
    b, s, d = x.shape
    seg = lambda n: w_in[:, _OFF[n]:_OFF[n + 1]]
    w_a = jnp.concatenate([seg(1), seg(4) * GLA_KEY_DIM ** -0.5, seg(5), seg(6)], axis=1).astype(BF16)
    w_t = jnp.concatenate([seg(0) * FOX_HEAD_DIM ** -0.5, seg(2)], axis=1).T.astype(BF16)
    pad = jnp.zeros((d, SMALL_WIDTH - FOX_HEADS - GLA_GATE_RANK), w_in.dtype)
    w_small = jnp.concatenate([seg(3), seg(7), pad], axis=1).astype(BF16)
    w_b = w_in[:, _OFF[8]:_OFF[13]].astype(BF16)
    b_forget_pad = jnp.zeros((1, SMALL_WIDTH), F32).at[0, :FOX_HEADS].set(b_forget)
    w_gate_pad = jnp.zeros((SMALL_WIDTH, GLA_K_WIDTH), F32).at[FOX_HEADS:FOX_HEADS + GLA_GATE_RANK].set(
        w_gla_gate).astype(BF16)
    w_br = jnp.stack([w_br_fox, w_br_gla, w_br_pool]).astype(BF16)
    mod3 = mod.reshape(b, 1, 3 * d)
    g_pre = g_pre.reshape(1, d)

    fk, gq, gk, gv, fqt, fvt, small = _in_proj(x, mod3, g_pre, w_a, w_t, w_small)
    kx, qx = _forget_cumsum(small, b_forget_pad)
    o_fox = _fox_attention(fqt, qx, fk, kx, fvt)
    o_gla = _gla(gq, gk, gv, small, w_gate_pad, b_gla_gate.reshape(1, GLA_K_WIDTH),
                 g_gla_norm.reshape(1, GLA_V_WIDTH))
    return _merge(x, mod3, g_pre, g_post.reshape(1, d), o_fox, o_gla, w_b, w_pool.astype(BF16),
                  pool_scale.reshape(1, POOL_WIDTH), w_br, w_out.astype(BF16))


def kernel(x, c, w_ada, b_ada, g_pre, g_post, w_in, b_forget, w_gla_gate, b_gla_gate, g_gla_norm, w_pool,
           pool_scale, w_br_fox, w_br_gla, w_br_pool, w_out):
    mods = _adaln_mod(c, w_ada, b_ada)
    h = x
    for i in range(w_in.shape[0]):
        h = _layer(h, mods[i], g_pre[i], g_post[i], w_in[i], b_forget[i], w_gla_gate[i], b_gla_gate[i],
                   g_gla_norm[i], w_pool[i], pool_scale[i], w_br_fox[i], w_br_gla[i], w_br_pool[i], w_out[i])
    return h
```

```python
import functools

import numpy as np
import jax
import jax.numpy as jnp
from jax import lax
from jax.experimental import pallas as pl
from jax.experimental.pallas import tpu as pltpu

F32 = jnp.float32
BF16 = jnp.bfloat16

D_MODEL = 1024
FOX_HEADS = 8
FOX_HEAD_DIM = 64
FOX_WIDTH = FOX_HEADS * FOX_HEAD_DIM
GLA_HEADS = 4
GLA_KEY_DIM = 64
GLA_VAL_DIM = 128
GLA_K_WIDTH = GLA_HEADS * GLA_KEY_DIM
GLA_V_WIDTH = GLA_HEADS * GLA_VAL_DIM
GLA_GATE_RANK = 16
GLA_TAU = 16.0
POOL_WINDOWS = (2, 4, 8, 16)
POOL_GROUP_DIM = 128
POOL_WIDTH = len(POOL_WINDOWS) * POOL_GROUP_DIM
N_BRANCHES = 3
RMS_EPS = 1e-6
IN_SPLITS = (FOX_WIDTH, FOX_WIDTH, FOX_WIDTH, FOX_HEADS,
             GLA_K_WIDTH, GLA_K_WIDTH, GLA_V_WIDTH, GLA_GATE_RANK,
             POOL_WIDTH, FOX_WIDTH, GLA_V_WIDTH, POOL_WIDTH, N_BRANCHES * D_MODEL)
_OFF = np.concatenate([[0], np.cumsum(IN_SPLITS)]).tolist()

LANES = 128
SMALL_WIDTH = LANES
V_ROWS = FOX_HEAD_DIM + 16
POOL_HALO = 16
MASK_VALUE = -1e30
LOG2_E = 1.4426950408889634
V7X_VMEM_BYTES = 64 * 1024 * 1024

IN_TOKENS = 512
FORGET_TOKENS = 512
FOX_TILE = 256
FOX_PAIRS = 4
GLA_TOKENS = 256
GLA_CHUNK = 64
GLA_BATCH = 2
MERGE_TOKENS = 512


def _vmem_limit(block_bytes, resident_bytes, temp_bytes):
    need = 2 * block_bytes + resident_bytes + temp_bytes
    return int(min(max(need, 16 * 1024 * 1024), V7X_VMEM_BYTES - 8 * 1024 * 1024))


def _resident(shape):
    zeros = (0,) * len(shape)
    return pl.BlockSpec(shape, lambda *_: zeros, pipeline_mode=pl.Buffered(1))


def _div_pow2(i, n):
    shift = n.bit_length() - 1
    assert 1 << shift == n
    return jnp.right_shift(i, shift)


def _log_sigmoid(x):
    return jnp.minimum(x, 0.0) - jnp.log1p(jnp.exp(-jnp.abs(x)))


def _sigmoid(x):
    return jax.nn.sigmoid(x)


def _silu(x):
    return x * _sigmoid(x)


def _dot(a, b):
    return jnp.dot(a, b, preferred_element_type=F32)


def _dot_nt(a, b):
    return lax.dot_general(a, b, (((1,), (1,)), ((), ())), preferred_element_type=F32)


def _dot_tn(a, b):
    return lax.dot_general(a, b, (((0,), (0,)), ((), ())), preferred_element_type=F32)


def _split3(x):
    hi = x.astype(BF16)
    r1 = x - hi.astype(F32)
    mid = r1.astype(BF16)
    lo = (r1 - mid.astype(F32)).astype(BF16)
    return hi, mid, lo


def _ones_dot_f32(ones_bf16, x):
    hi, mid, lo = _split3(x)
    return _dot(ones_bf16, hi) + _dot(ones_bf16, mid) + _dot(ones_bf16, lo)


def _modulated_norm(x, g, mod):
    y = x * lax.rsqrt(jnp.mean(x * x, axis=-1, keepdims=True) + RMS_EPS) * g
    shift = mod[:, 0:D_MODEL]
    scale = mod[:, D_MODEL:2 * D_MODEL]
    return (y * (1.0 + scale) + shift).astype(BF16)


def _mod_kernel(c_ref, w_ref, b_ref, o_ref):
    c = c_ref[...]
    o_ref[0] = jnp.dot(_silu(c), w_ref[0], preferred_element_type=F32,
                       precision=lax.Precision.HIGHEST) + b_ref[0]


def _adaln_mod(c, w_ada, b_ada):
    depth, d, d3 = w_ada.shape
    b = c.shape[0]
    n_col = d3 // d
    return pl.pallas_call(
        _mod_kernel,
        grid=(depth, n_col),
        in_specs=[pl.BlockSpec((b, d), lambda l, j: (0, 0)),
                  pl.BlockSpec((1, d, d), lambda l, j: (l, 0, j)),
                  pl.BlockSpec((1, 1, d), lambda l, j: (l, 0, j))],
        out_specs=pl.BlockSpec((1, b, d), lambda l, j: (l, 0, j)),
        out_shape=jax.ShapeDtypeStruct((depth, b, d3), F32),
        compiler_params=pltpu.CompilerParams(
            dimension_semantics=("parallel", "parallel"),
            vmem_limit_bytes=_vmem_limit(d * d * 4 + 2 * b * d * 4, 0, 4 * b * d * 4)),
        name="adaln_mod",
    )(c, w_ada, b_ada.reshape(depth, 1, d3))


_A_FOX_K = 0
_A_GLA_Q = _A_FOX_K + FOX_WIDTH
_A_GLA_K = _A_GLA_Q + GLA_K_WIDTH
_A_GLA_V = _A_GLA_K + GLA_K_WIDTH
_A_WIDTH = _A_GLA_V + GLA_V_WIDTH


def _in_proj_kernel(x_ref, mod_ref, g_ref, wa_ref, wqt_ref, wvt_ref, vones_ref, ws_ref,
                    fk_ref, gq_ref, gk_ref, gv_ref, fqt_ref, fvt_ref, small_ref):
    h = _modulated_norm(x_ref[0], g_ref[...], mod_ref[0])
    fk_ref[0] = _dot(h, wa_ref[:, _A_FOX_K:_A_GLA_Q]).astype(BF16)
    gq_ref[0] = _dot(h, wa_ref[:, _A_GLA_Q:_A_GLA_K]).astype(BF16)
    gk_ref[0] = _dot(h, wa_ref[:, _A_GLA_K:_A_GLA_V]).astype(BF16)
    gv_ref[0] = _dot(h, wa_ref[:, _A_GLA_V:_A_WIDTH]).astype(BF16)
    fqt_ref[0] = _dot_nt(wqt_ref[...], h).astype(BF16)
    fvt_ref[0] = (_dot_nt(wvt_ref[...], h) + vones_ref[...]).astype(BF16)
    small_ref[0] = _dot(h, ws_ref[...])


def _in_proj(x, mod, g_pre, w_a, w_qt, w_vt, v_ones, w_small):
    b, s, d = x.shape
    t = min(IN_TOKENS, s)
    tok = lambda width: pl.BlockSpec((1, t, width), lambda bi, i: (bi, i, 0))
    tok_t = lambda rows: pl.BlockSpec((1, rows, t), lambda bi, i: (bi, 0, i))
    v_rows = FOX_HEADS * V_ROWS
    out_shapes = (
        jax.ShapeDtypeStruct((b, s, FOX_WIDTH), BF16),
        jax.ShapeDtypeStruct((b, s, GLA_K_WIDTH), BF16),
        jax.ShapeDtypeStruct((b, s, GLA_K_WIDTH), BF16),
        jax.ShapeDtypeStruct((b, s, GLA_V_WIDTH), BF16),
        jax.ShapeDtypeStruct((b, FOX_WIDTH, s), BF16),
        jax.ShapeDtypeStruct((b, v_rows, s), BF16),
        jax.ShapeDtypeStruct((b, s, SMALL_WIDTH), F32),
    )
    out_specs = (tok(FOX_WIDTH), tok(GLA_K_WIDTH), tok(GLA_K_WIDTH), tok(GLA_V_WIDTH), tok_t(FOX_WIDTH), tok_t(v_rows),
                 tok(SMALL_WIDTH))
    n_out = _A_WIDTH + FOX_WIDTH + v_rows
    block_bytes = t * d * 4 + t * n_out * 2 + t * SMALL_WIDTH * 4
    weight_bytes = d * (n_out + SMALL_WIDTH) * 2
    return pl.pallas_call(
        _in_proj_kernel,
        grid=(b, s // t),
        in_specs=[tok(d),
                  pl.BlockSpec((1, 1, 3 * d), lambda bi, i: (bi, 0, 0)),
                  _resident((1, d)), _resident(w_a.shape), _resident(w_qt.shape), _resident(w_vt.shape),
                  _resident(v_ones.shape), _resident(w_small.shape)],
        out_specs=out_specs,
        out_shape=out_shapes,
        compiler_params=pltpu.CompilerParams(
            dimension_semantics=("parallel", "parallel"),
            vmem_limit_bytes=_vmem_limit(block_bytes, weight_bytes, 6 * t * d * 4)),
        name="in_proj",
    )(x, mod, g_pre, w_a, w_qt, w_vt, v_ones, w_small)


BIAS_SLOTS = 16


def _bias_placement():
    pk = np.zeros((3, SMALL_WIDTH, FOX_WIDTH), np.float32)
    k_const = np.zeros((1, FOX_WIDTH), np.float32)
    pq = np.zeros((3, FOX_HEADS * BIAS_SLOTS, SMALL_WIDTH), np.float32)
    q_const = np.zeros((FOX_HEADS * BIAS_SLOTS, 1), np.float32)
    for head in range(FOX_HEADS):
        pair, odd = divmod(head, 2)
        for part in range(3):
            pk[part, head, pair * LANES + 3 + 3 * odd + part] = -1.0
            pq[part, head * BIAS_SLOTS + part, head] = 1.0
            q_const[head * BIAS_SLOTS + 3 + 3 * odd + part, 0] = 1.0
            k_const[0, pair * LANES + part] = 1.0
    return pk, k_const, pq, q_const


def _forget_kernel(small_ref, b_ref, pk_ref, kc_ref, pq_ref, qc_ref, kx_ref, qx_ref, carry_ref):
    @pl.when(pl.program_id(1) == 0)
    def _():
        carry_ref[...] = jnp.zeros_like(carry_ref)

    t = small_ref.shape[1]
    log_f = _log_sigmoid(small_ref[0] + b_ref[...])
    row = lax.broadcasted_iota(jnp.int32, (t, t), 0)
    col = lax.broadcasted_iota(jnp.int32, (t, t), 1)
    tri = jnp.where(col <= row, 1.0, 0.0).astype(BF16)
    cs = _ones_dot_f32(tri, log_f) + carry_ref[...]
    carry_ref[...] = cs[t - 1:t, :]
    parts = _split3(cs * LOG2_E)
    kx = kc_ref[...]
    qx = qc_ref[...]
    for part in range(3):
        kx = kx + _dot(parts[part], pk_ref[part])
        qx = qx + _dot_nt(pq_ref[part], parts[part])
    kx_ref[0] = kx.astype(BF16)
    qx_ref[0] = qx.astype(BF16)


def _forget_cumsum(small, b_forget_pad):
    b, s, _ = small.shape
    t = min(FORGET_TOKENS, s)
    pk, k_const, pq, q_const = _bias_placement()
    q_rows = FOX_HEADS * BIAS_SLOTS
    return pl.pallas_call(
        _forget_kernel,
        grid=(b, s // t),
        in_specs=[pl.BlockSpec((1, t, SMALL_WIDTH), lambda bi, i: (bi, i, 0)),
                  _resident((1, SMALL_WIDTH)), _resident(pk.shape), _resident(k_const.shape),
                  _resident(pq.shape), _resident(q_const.shape)],
        out_specs=(pl.BlockSpec((1, t, FOX_WIDTH), lambda bi, i: (bi, i, 0)),
                   pl.BlockSpec((1, q_rows, t), lambda bi, i: (bi, 0, i))),
        out_shape=(jax.ShapeDtypeStruct((b, s, FOX_WIDTH), BF16),
                   jax.ShapeDtypeStruct((b, q_rows, s), BF16)),
        scratch_shapes=[pltpu.VMEM((1, SMALL_WIDTH), F32)],
        compiler_params=pltpu.CompilerParams(
            dimension_semantics=("parallel", "arbitrary"),
            vmem_limit_bytes=_vmem_limit(t * (SMALL_WIDTH * 4 + FOX_WIDTH * 2 + q_rows * 2), pk.size * 2 + pq.size * 2,
                                         2 * t * t * 4 + 8 * t * FOX_WIDTH * 4)),
        name="forget_cumsum",
    )(small, b_forget_pad, jnp.asarray(pk, BF16), jnp.asarray(k_const), jnp.asarray(pq, BF16), jnp.asarray(q_const))


def _fox_kernel(qt_ref, qx_ref, k_ref, kx_ref, vt_ref, o_ref, s_ref, p_ref, acc_ref):
    t = qt_ref.shape[2]
    i = pl.program_id(2)
    n_heads = 2 * FOX_PAIRS
    row = lax.broadcasted_iota(jnp.int32, (LANES, t), 0)
    low = row < FOX_HEAD_DIM
    pad = jnp.zeros((LANES - BIAS_SLOTS, t), BF16)
    q_aug = []
    for hd in range(n_heads):
        qt = qt_ref[0, (hd // 2) * LANES:(hd // 2 + 1) * LANES, :]
        own = jnp.where(low, qt, jnp.zeros_like(qt)) if hd % 2 == 0 else jnp.where(low, jnp.zeros_like(qt), qt)
        q_aug.append(jnp.concatenate([own, qx_ref[0, hd * BIAS_SLOTS:(hd + 1) * BIAS_SLOTS, :], pad], axis=0))

    def logits(j, masked):
        k0 = pl.multiple_of(j * t, t)
        out = []
        for pr in range(FOX_PAIRS):
            lanes = slice(pr * LANES, (pr + 1) * LANES)
            k_aug = jnp.concatenate([k_ref[0, pl.ds(k0, t), lanes], kx_ref[0, pl.ds(k0, t), lanes]], axis=1)
            for h in range(2):
                s = _dot(k_aug, q_aug[2 * pr + h])
                if masked:
                    r = lax.broadcasted_iota(jnp.int32, (t, t), 0)
                    c = lax.broadcasted_iota(jnp.int32, (t, t), 1)
                    s = jnp.where(r <= c, s, MASK_VALUE)
                out.append(s)
        return out

    def weighted_values(j, hd, p):
        k0 = pl.multiple_of(j * t, t)
        return _dot(vt_ref[0, hd * V_ROWS:(hd + 1) * V_ROWS, pl.ds(k0, t)], p)

    def stage_logits(j, masked, slot):
        maxes = []
        for hd, s in enumerate(logits(j, masked)):
            s_ref[slot, hd] = s
            maxes.append(jnp.max(s, axis=0, keepdims=True))
        return maxes

    def stage_softmax(hd, s_max, m, s_slot, p_slot):
        m_new = jnp.maximum(m, s_max)
        p_ref[p_slot, hd] = jnp.exp2(s_ref[s_slot, hd] - m_new).astype(BF16)
        return m_new, jnp.exp2(m - m_new)

    m0 = jnp.full((1, t), MASK_VALUE, F32)
    s_max = stage_logits(i, True, 1)
    stats = [stage_softmax(hd, s_max[hd], m0, 1, 0) for hd in range(n_heads)]
    s_max = stage_logits(0, False, 0)
    acc_ref[...] = jnp.zeros_like(acc_ref)

    def trip(par, k, carry):
        prev, heads = carry
        for hd in range(n_heads):
            pv = weighted_values(prev, hd, p_ref[par, hd])
            acc_ref[hd] = heads[hd][1] * acc_ref[hd] + pv
        stats = [stage_softmax(hd, heads[hd][2], heads[hd][0], par, 1 - par) for hd in range(n_heads)]
        s_max = stage_logits(jnp.minimum(k + 1, i - 1), False, 1 - par)
        return k, tuple(stats[hd] + (s_max[hd],) for hd in range(n_heads))

    def either_trip(k, carry):
        return lax.cond((k & 1) == 0, functools.partial(trip, 0, k), functools.partial(trip, 1, k), carry)

    prev, heads = lax.fori_loop(0, i, either_trip, (i, tuple(stats[hd] + (s_max[hd],) for hd in range(n_heads))))
    o_heads = []
    for hd in range(n_heads):
        acc = heads[hd][1] * acc_ref[hd] + weighted_values(prev, hd, p_ref[i & 1, hd])
        o_heads.append(acc[0:FOX_HEAD_DIM] * (1.0 / acc[FOX_HEAD_DIM:FOX_HEAD_DIM + 1]))
    o_ref[0] = jnp.concatenate(o_heads, axis=0).T.astype(o_ref.dtype)


def _fox_attention(fqt, qx, fk, kx, fvt):
    b, s, _ = fk.shape
    t = min(FOX_TILE, s)
    width = FOX_PAIRS * LANES
    n_heads = 2 * FOX_PAIRS
    groups = FOX_WIDTH // width
    whole = lambda shape, index_map: pl.BlockSpec(shape, index_map, pipeline_mode=pl.Buffered(1))
    block_bytes = (width + n_heads * BIAS_SLOTS) * t * 2 + t * width * 2
    resident_bytes = (2 * width + n_heads * V_ROWS) * s * 2 + n_heads * t * (2 * t * 6 + V_ROWS * 4)
    return pl.pallas_call(
        _fox_kernel,
        grid=(b, groups, s // t),
        in_specs=[pl.BlockSpec((1, width, t), lambda bi, g, i: (bi, g, i)),
                  pl.BlockSpec((1, n_heads * BIAS_SLOTS, t), lambda bi, g, i: (bi, g, i)),
                  whole((1, s, width), lambda bi, g, i: (bi, 0, g)),
                  whole((1, s, width), lambda bi, g, i: (bi, 0, g)),
                  whole((1, n_heads * V_ROWS, s), lambda bi, g, i: (bi, g, 0))],
        out_specs=pl.BlockSpec((1, t, width), lambda bi, g, i: (bi, i, g)),
        out_shape=jax.ShapeDtypeStruct((b, s, FOX_WIDTH), BF16),
        scratch_shapes=[pltpu.VMEM((2, n_heads, t, t), F32), pltpu.VMEM((2, n_heads, t, t), BF16),
                        pltpu.VMEM((n_heads, V_ROWS, t), F32)],
        compiler_params=pltpu.CompilerParams(
            dimension_semantics=("parallel", "parallel", "parallel"),
            vmem_limit_bytes=_vmem_limit(block_bytes, resident_bytes, 8 * n_heads * t * t)),
        name="fox_attention",
    )(fqt, qx, fk, kx, fvt)


def _gla_kernel(q_ref, k_ref, v_ref, small_ref, wg_ref, bg_ref, gn_ref, o_ref, state_ref):
    @pl.when(pl.program_id(1) == 0)
    def _():
        state_ref[...] = jnp.zeros_like(state_ref)

    nb, t, _ = q_ref.shape
    c = GLA_CHUNK
    half = c // 2
    row = lax.broadcasted_iota(jnp.int32, (t, t), 0)
    col = lax.broadcasted_iota(jnp.int32, (t, t), 1)
    same_chunk = _div_pow2(row, c) == _div_pow2(col, c)
    tri = jnp.where(same_chunk & (col <= row), 1.0, 0.0).astype(BF16)
    cum_all = []
    for bb in range(nb):
        gate_logits = _dot(small_ref[bb].astype(BF16), wg_ref[...]) + bg_ref[...]
        cum_all.append(_ones_dot_f32(tri, _log_sigmoid(gate_logits) * (1.0 / GLA_TAU)))

    crow = lax.broadcasted_iota(jnp.int32, (c, GLA_K_WIDTH), 0)
    first = crow < half
    lane = lax.broadcasted_iota(jnp.int32, (c, GLA_K_WIDTH), 1)
    head_lanes = [_div_pow2(lane, GLA_KEY_DIM) == h for h in range(GLA_HEADS)]
    sr = lax.broadcasted_iota(jnp.int32, (GLA_HEADS * c, c), 0) & (c - 1)
    sc = lax.broadcasted_iota(jnp.int32, (GLA_HEADS * c, c), 1)
    near = (_div_pow2(sr, half) == _div_pow2(sc, half)) & (sc <= sr)
    srow = lax.broadcasted_iota(jnp.int32, state_ref.shape[1:], 0)
    slane = lax.broadcasted_iota(jnp.int32, state_ref.shape[1:], 1)
    own_head = _div_pow2(srow, GLA_VAL_DIM) == _div_pow2(slane, GLA_KEY_DIM)

    def stack_heads(x):
        return jnp.concatenate([jnp.where(m, x, 0.0) for m in head_lanes], axis=0).astype(BF16)

    for ci, bb in [(ci, bb) for ci in range(t // c) for bb in range(nb)]:
        r0 = ci * c
        cum = cum_all[bb][r0:r0 + c]
        q = q_ref[bb, r0:r0 + c, :].astype(F32)
        k = k_ref[bb, r0:r0 + c, :].astype(F32)
        v = v_ref[bb, r0:r0 + c, :]
        last = cum[c - 1:c]
        ref_far = cum[half - 1:half]
        ref_near = jnp.where(first, cum[half // 2:half // 2 + 1], cum[half + half // 2:half + half // 2 + 1])
        q_far = jnp.where(first, 0.0, q * jnp.exp(cum - ref_far))
        k_far = jnp.where(first, k * jnp.exp(ref_far - cum), 0.0)
        q_near = q * jnp.exp(cum - ref_near)
        k_near = k * jnp.exp(ref_near - cum)
        scores = (_dot_nt(stack_heads(q_far), k_far.astype(BF16))
                  + jnp.where(near, _dot_nt(stack_heads(q_near), k_near.astype(BF16)), 0.0))
        state = state_ref[bb]
        o = _dot_nt((q * jnp.exp(cum)).astype(BF16), state.astype(BF16))
        p = scores.astype(BF16)
        for h in range(GLA_HEADS):
            lo, hi = h * GLA_VAL_DIM, (h + 1) * GLA_VAL_DIM
            o_h = o[:, lo:hi] + _dot(p[h * c:(h + 1) * c], v[:, lo:hi])
            y = o_h * lax.rsqrt(jnp.mean(o_h * o_h, axis=-1, keepdims=True) + RMS_EPS) * gn_ref[:, lo:hi]
            o_ref[bb, r0:r0 + c, lo:hi] = y.astype(o_ref.dtype)
        k_tail = (k * jnp.exp(last - cum)).astype(BF16)
        update = _dot_tn(v, k_tail)
        state_ref[bb] = state * jnp.exp(last) + jnp.where(own_head, update, 0.0)


def _gla(gq, gk, gv, small, w_gate_pad, b_gate, g_norm):
    b, s, _ = gq.shape
    t = min(GLA_TOKENS, s)
    nb = GLA_BATCH if b % GLA_BATCH == 0 else 1
    tok = lambda width: pl.BlockSpec((nb, t, width), lambda bi, i: (bi, i, 0))
    block_bytes = nb * t * (2 * GLA_K_WIDTH * 2 + 2 * GLA_V_WIDTH * 2 + SMALL_WIDTH * 4)
    state_bytes = nb * GLA_V_WIDTH * GLA_K_WIDTH * 4
    return pl.pallas_call(
        _gla_kernel,
        grid=(b // nb, s // t),
        in_specs=[tok(GLA_K_WIDTH), tok(GLA_K_WIDTH), tok(GLA_V_WIDTH), tok(SMALL_WIDTH),
                  _resident(w_gate_pad.shape), _resident((1, GLA_K_WIDTH)), _resident((1, GLA_V_WIDTH))],
        out_specs=tok(GLA_V_WIDTH),
        out_shape=jax.ShapeDtypeStruct((b, s, GLA_V_WIDTH), BF16),
        scratch_shapes=[pltpu.VMEM((nb, GLA_V_WIDTH, GLA_K_WIDTH), F32)],
        compiler_params=pltpu.CompilerParams(
            dimension_semantics=("parallel", "arbitrary"),
            vmem_limit_bytes=_vmem_limit(block_bytes, state_bytes, 8 * state_bytes + 16 * t * GLA_K_WIDTH * 4)),
        name="gla",
    )(gq, gk, gv, small, w_gate_pad, b_gate, g_norm)


_B_POOL = 0
_B_Z = _B_POOL + POOL_WIDTH
_B_MERGE = _B_Z + FOX_WIDTH + GLA_V_WIDTH + POOL_WIDTH
_B_WIDTH = _B_MERGE + N_BRANCHES * D_MODEL


def _merge_kernel(x_ref, mod_ref, gpre_ref, gpost_ref, ofox_ref, ogla_ref, wb_ref, wpool_ref, pscale_ref,
                  wbr_ref, wout_ref, o_ref, u_ref):
    t = x_ref.shape[1]
    i = pl.program_id(1)
    x = x_ref[0]
    mod = mod_ref[0]
    h = _modulated_norm(x, gpre_ref[...], mod)

    @pl.when(i == 0)
    def _():
        u_ref[0:POOL_HALO, :] = jnp.zeros((POOL_HALO, POOL_WIDTH), F32)

    @pl.when(i > 0)
    def _():
        u_ref[0:POOL_HALO, :] = u_ref[t:t + POOL_HALO, :]

    u = _dot(h, wb_ref[:, _B_POOL:_B_Z])
    u_ref[POOL_HALO:POOL_HALO + t, :] = u
    count = (i * t + 1 + lax.broadcasted_iota(jnp.int32, (t, 1), 0)).astype(F32)
    pooled = []
    for g, w in enumerate(POOL_WINDOWS):
        lo, hi = g * POOL_GROUP_DIM, (g + 1) * POOL_GROUP_DIM
        window = u[:, lo:hi]
        for back in range(1, w):
            window = window + u_ref[pl.ds(POOL_HALO - back, t), lo:hi]
        mean = window / jnp.minimum(count, float(w))
        diff = (mean - u[:, lo:hi]).astype(BF16)
        pooled.append(_dot(diff, wpool_ref[g]) * pscale_ref[:, lo:hi])
    o_pool = jnp.concatenate(pooled, axis=-1)

    branches = (ofox_ref[0].astype(F32), ogla_ref[0].astype(F32), o_pool)
    merged = jnp.zeros((t, D_MODEL), F32)
    for br, o_br in enumerate(branches):
        z = _dot(h, wb_ref[:, _B_Z + br * FOX_WIDTH:_B_Z + (br + 1) * FOX_WIDTH])
        y = _dot((o_br * _silu(z)).astype(BF16), wbr_ref[br])
        m = _dot(h, wb_ref[:, _B_MERGE + br * D_MODEL:_B_MERGE + (br + 1) * D_MODEL])
        merged = merged + _sigmoid(m) * y
    out = _dot(merged.astype(BF16), wout_ref[...])
    out = out * lax.rsqrt(jnp.mean(out * out, axis=-1, keepdims=True) + RMS_EPS) * gpost_ref[...]
    gate = mod[:, 2 * D_MODEL:3 * D_MODEL]
    o_ref[0] = x + gate * out


def _merge(x, mod, g_pre, g_post, o_fox, o_gla, w_b, w_pool, pool_scale, w_br, w_out):
    b, s, d = x.shape
    t = min(MERGE_TOKENS, s)
    tok = lambda width: pl.BlockSpec((1, t, width), lambda bi, i: (bi, i, 0))
    block_bytes = 2 * t * d * 4 + 2 * t * FOX_WIDTH * 2
    weight_bytes = (w_b.size + w_pool.size + w_br.size + w_out.size) * 2
    return pl.pallas_call(
        _merge_kernel,
        grid=(b, s // t),
        in_specs=[tok(d),
                  pl.BlockSpec((1, 1, 3 * d), lambda bi, i: (bi, 0, 0)),
                  _resident((1, d)), _resident((1, d)),
                  tok(FOX_WIDTH), tok(GLA_V_WIDTH),
                  _resident(w_b.shape), _resident(w_pool.shape), _resident((1, POOL_WIDTH)),
                  _resident(w_br.shape), _resident(w_out.shape)],
        out_specs=tok(d),
        out_shape=jax.ShapeDtypeStruct((b, s, d), F32),
        scratch_shapes=[pltpu.VMEM((POOL_HALO + t, POOL_WIDTH), F32)],
        compiler_params=pltpu.CompilerParams(
            dimension_semantics=("parallel", "arbitrary"),
            vmem_limit_bytes=_vmem_limit(block_bytes, weight_bytes, 12 * t * d * 4)),
        name="merge",
    )(x, mod, g_pre, g_post, o_fox, o_gla, w_b, w_pool, pool_scale, w_br, w_out)


def _layer(x, mod, g_pre, g_post, w_in, b_forget, w_gla_gate, b_gla_gate, g_gla_norm, w_pool, pool_scale,
           w_br_fox, w_br_gla, w_br_pool, w_out):
    b, s, d = x.shape
    seg = lambda n: w_in[:, _OFF[n]:_OFF[n + 1]]
    w_a = jnp.concatenate([seg(1), seg(4) * GLA_KEY_DIM ** -0.5, seg(5), seg(6)], axis=1).astype(BF16)
    w_qt = (seg(0) * (LOG2_E * FOX_HEAD_DIM ** -0.5)).T.astype(BF16)
    w_vt = jnp.pad(seg(2).T.reshape(FOX_HEADS, FOX_HEAD_DIM, d), ((0, 0), (0, V_ROWS - FOX_HEAD_DIM), (0, 0)))
    w_vt = w_vt.reshape(FOX_HEADS * V_ROWS, d).astype(BF16)
    v_ones = jnp.tile(jnp.arange(V_ROWS) >= FOX_HEAD_DIM, FOX_HEADS).astype(F32).reshape(FOX_HEADS * V_ROWS, 1)
    pad = jnp.zeros((d, SMALL_WIDTH - FOX_HEADS - GLA_GATE_RANK), w_in.dtype)
    w_small = jnp.concatenate([seg(3), seg(7), pad], axis=1).astype(BF16)
    w_b = w_in[:, _OFF[8]:_OFF[13]].astype(BF16)
    b_forget_pad = jnp.zeros((1, SMALL_WIDTH), F32).at[0, :FOX_HEADS].set(b_forget)
    w_gate_pad = jnp.zeros((SMALL_WIDTH, GLA_K_WIDTH), F32).at[FOX_HEADS:FOX_HEADS + GLA_GATE_RANK].set(
        w_gla_gate).astype(BF16)
    w_br = jnp.stack([w_br_fox, w_br_gla, w_br_pool]).astype(BF16)
    mod3 = mod.reshape(b, 1, 3 * d)
    g_pre = g_pre.reshape(1, d)

    fk, gq, gk, gv, fqt, fvt, small = _in_proj(x, mod3, g_pre, w_a, w_qt, w_vt, v_ones, w_small)
    kx, qx = _forget_cumsum(small, b_forget_pad)
    o_fox = _fox_attention(fqt, qx, fk, kx, fvt)
    o_gla = _gla(gq, gk, gv, small, w_gate_pad, b_gla_gate.reshape(1, GLA_K_WIDTH),
                 g_gla_norm.reshape(1, GLA_V_WIDTH))
    return _merge(x, mod3, g_pre, g_post.reshape(1, d), o_fox, o_gla, w_b, w_pool.astype(BF16),
                  pool_scale.reshape(1, POOL_WIDTH), w_br, w_out.astype(BF16))


def kernel(x, c, w_ada, b_ada, g_pre, g_post, w_in, b_forget, w_gla_gate, b_gla_gate, g_gla_norm, w_pool,
           pool_scale, w_br_fox, w_br_gla, w_br_pool, w_out):
    mods = _adaln_mod(c, w_ada, b_ada)
    h = x
    for i in range(w_in.shape[0]):
        h = _layer(h, mods[i], g_pre[i], g_post[i], w_in[i], b_forget[i], w_gla_gate[i], b_gla_gate[i],
                   g_gla_norm[i], w_pool[i], pool_scale[i], w_br_fox[i], w_br_gla[i], w_br_pool[i], w_out[i])
    return h
```

```python
import functools

import numpy as np
import jax
import jax.numpy as jnp
from jax import lax
from jax.experimental import pallas as pl
from jax.experimental.pallas import tpu as pltpu

F32 = jnp.float32
BF16 = jnp.bfloat16

D_MODEL = 1024
FOX_HEADS = 8
FOX_HEAD_DIM = 64
FOX_WIDTH = FOX_HEADS * FOX_HEAD_DIM
GLA_HEADS = 4
GLA_KEY_DIM = 64
GLA_VAL_DIM = 128
GLA_K_WIDTH = GLA_HEADS * GLA_KEY_DIM
GLA_V_WIDTH = GLA_HEADS * GLA_VAL_DIM
GLA_GATE_RANK = 16
GLA_TAU = 16.0
POOL_WINDOWS = (2, 4, 8, 16)
POOL_GROUP_DIM = 128
POOL_WIDTH = len(POOL_WINDOWS) * POOL_GROUP_DIM
N_BRANCHES = 3
RMS_EPS = 1e-6
IN_SPLITS = (FOX_WIDTH, FOX_WIDTH, FOX_WIDTH, FOX_HEADS,
             GLA_K_WIDTH, GLA_K_WIDTH, GLA_V_WIDTH, GLA_GATE_RANK,
             POOL_WIDTH, FOX_WIDTH, GLA_V_WIDTH, POOL_WIDTH, N_BRANCHES * D_MODEL)
_OFF = np.concatenate([[0], np.cumsum(IN_SPLITS)]).tolist()

LANES = 128
SMALL_WIDTH = LANES
V_ROWS = FOX_HEAD_DIM + 16
POOL_HALO = 16
MASK_VALUE = -1e30
LOG2_E = 1.4426950408889634
SKIP_LOG2 = 160.0
NORM_SLACK = 1.02
V7X_VMEM_BYTES = 64 * 1024 * 1024

IN_TOKENS = 512
FORGET_TOKENS = 512
FOX_TILE = 256
FOX_PAIRS = 4
GLA_TOKENS = 256
GLA_CHUNK = 64
GLA_BATCH = 2
MERGE_TOKENS = 512


def _vmem_limit(block_bytes, resident_bytes, temp_bytes):
    need = 2 * block_bytes + resident_bytes + temp_bytes
    return int(min(max(need, 16 * 1024 * 1024), V7X_VMEM_BYTES - 8 * 1024 * 1024))


def _resident(shape):
    zeros = (0,) * len(shape)
    return pl.BlockSpec(shape, lambda *_: zeros, pipeline_mode=pl.Buffered(1))


def _div_pow2(i, n):
    shift = n.bit_length() - 1
    assert 1 << shift == n
    return jnp.right_shift(i, shift)


def _log_sigmoid(x):
    return jnp.minimum(x, 0.0) - jnp.log1p(jnp.exp(-jnp.abs(x)))


def _sigmoid(x):
    return jax.nn.sigmoid(x)


def _silu(x):
    return x * _sigmoid(x)


def _dot(a, b):
    return jnp.dot(a, b, preferred_element_type=F32)


def _dot_nt(a, b):
    return lax.dot_general(a, b, (((1,), (1,)), ((), ())), preferred_element_type=F32)


def _dot_tn(a, b):
    return lax.dot_general(a, b, (((0,), (0,)), ((), ())), preferred_element_type=F32)


def _split3(x):
    hi = x.astype(BF16)
    r1 = x - hi.astype(F32)
    mid = r1.astype(BF16)
    lo = (r1 - mid.astype(F32)).astype(BF16)
    return hi, mid, lo


def _ones_dot_f32(ones_bf16, x):
    hi, mid, lo = _split3(x)
    return _dot(ones_bf16, hi) + _dot(ones_bf16, mid) + _dot(ones_bf16, lo)


def _modulated_norm(x, g, mod):
    y = x * lax.rsqrt(jnp.mean(x * x, axis=-1, keepdims=True) + RMS_EPS) * g
    shift = mod[:, 0:D_MODEL]
    scale = mod[:, D_MODEL:2 * D_MODEL]
    return (y * (1.0 + scale) + shift).astype(BF16)


def _mod_kernel(c_ref, w_ref, b_ref, o_ref):
    c = c_ref[...]
    o_ref[0] = jnp.dot(_silu(c), w_ref[0], preferred_element_type=F32,
                       precision=lax.Precision.HIGHEST) + b_ref[0]


def _adaln_mod(c, w_ada, b_ada):
    depth, d, d3 = w_ada.shape
    b = c.shape[0]
    n_col = d3 // d
    return pl.pallas_call(
        _mod_kernel,
        grid=(depth, n_col),
        in_specs=[pl.BlockSpec((b, d), lambda l, j: (0, 0)),
                  pl.BlockSpec((1, d, d), lambda l, j: (l, 0, j)),
                  pl.BlockSpec((1, 1, d), lambda l, j: (l, 0, j))],
        out_specs=pl.BlockSpec((1, b, d), lambda l, j: (l, 0, j)),
        out_shape=jax.ShapeDtypeStruct((depth, b, d3), F32),
        compiler_params=pltpu.CompilerParams(
            dimension_semantics=("parallel", "parallel"),
            vmem_limit_bytes=_vmem_limit(d * d * 4 + 2 * b * d * 4, 0, 4 * b * d * 4)),
        name="adaln_mod",
    )(c, w_ada, b_ada.reshape(depth, 1, d3))


_A_FOX_K = 0
_A_GLA_Q = _A_FOX_K + FOX_WIDTH
_A_GLA_K = _A_GLA_Q + GLA_K_WIDTH
_A_GLA_V = _A_GLA_K + GLA_K_WIDTH
_A_WIDTH = _A_GLA_V + GLA_V_WIDTH
KN_ROWS = 16


def _in_proj_kernel(x_ref, mod_ref, g_ref, wa_ref, wqt_ref, wvt_ref, vones_ref, ws_ref, hsel_ref,
                    fk_ref, gq_ref, gk_ref, gv_ref, fqt_ref, fvt_ref, small_ref, kn2_ref):
    h = _modulated_norm(x_ref[0], g_ref[...], mod_ref[0])
    fk = _dot(h, wa_ref[:, _A_FOX_K:_A_GLA_Q]).astype(BF16)
    fk_ref[0] = fk
    k2 = fk.astype(F32)
    norms = _dot_nt(hsel_ref[...], (k2 * k2).astype(BF16))
    tile_max = jnp.broadcast_to(jnp.max(norms, axis=1, keepdims=True), kn2_ref.shape[1:])

    @pl.when(pl.program_id(1) == 0)
    def _():
        kn2_ref[0] = tile_max

    @pl.when(pl.program_id(1) > 0)
    def _():
        kn2_ref[0] = jnp.maximum(kn2_ref[0], tile_max)

    gq_ref[0] = _dot(h, wa_ref[:, _A_GLA_Q:_A_GLA_K]).astype(BF16)
    gk_ref[0] = _dot(h, wa_ref[:, _A_GLA_K:_A_GLA_V]).astype(BF16)
    gv_ref[0] = _dot(h, wa_ref[:, _A_GLA_V:_A_WIDTH]).astype(BF16)
    fqt_ref[0] = _dot_nt(wqt_ref[...], h).astype(BF16)
    fvt_ref[0] = (_dot_nt(wvt_ref[...], h) + vones_ref[...]).astype(BF16)
    small_ref[0] = _dot(h, ws_ref[...])


def _in_proj(x, mod, g_pre, w_a, w_qt, w_vt, v_ones, w_small, head_sel):
    b, s, d = x.shape
    t = min(IN_TOKENS, s)
    tok = lambda width: pl.BlockSpec((1, t, width), lambda bi, i: (bi, i, 0))
    tok_t = lambda rows: pl.BlockSpec((1, rows, t), lambda bi, i: (bi, 0, i))
    v_rows = FOX_HEADS * V_ROWS
    out_shapes = (
        jax.ShapeDtypeStruct((b, s, FOX_WIDTH), BF16),
        jax.ShapeDtypeStruct((b, s, GLA_K_WIDTH), BF16),
        jax.ShapeDtypeStruct((b, s, GLA_K_WIDTH), BF16),
        jax.ShapeDtypeStruct((b, s, GLA_V_WIDTH), BF16),
        jax.ShapeDtypeStruct((b, FOX_WIDTH, s), BF16),
        jax.ShapeDtypeStruct((b, v_rows, s), BF16),
        jax.ShapeDtypeStruct((b, s, SMALL_WIDTH), F32),
        jax.ShapeDtypeStruct((b, KN_ROWS, LANES), F32),
    )
    out_specs = (tok(FOX_WIDTH), tok(GLA_K_WIDTH), tok(GLA_K_WIDTH), tok(GLA_V_WIDTH), tok_t(FOX_WIDTH), tok_t(v_rows),
                 tok(SMALL_WIDTH), pl.BlockSpec((1, KN_ROWS, LANES), lambda bi, i: (bi, 0, 0)))
    n_out = _A_WIDTH + FOX_WIDTH + v_rows
    block_bytes = t * d * 4 + t * n_out * 2 + t * SMALL_WIDTH * 4
    weight_bytes = d * (n_out + SMALL_WIDTH) * 2
    return pl.pallas_call(
        _in_proj_kernel,
        grid=(b, s // t),
        in_specs=[tok(d),
                  pl.BlockSpec((1, 1, 3 * d), lambda bi, i: (bi, 0, 0)),
                  _resident((1, d)), _resident(w_a.shape), _resident(w_qt.shape), _resident(w_vt.shape),
                  _resident(v_ones.shape), _resident(w_small.shape), _resident(head_sel.shape)],
        out_specs=out_specs,
        out_shape=out_shapes,
        compiler_params=pltpu.CompilerParams(
            dimension_semantics=("parallel", "arbitrary"),
            vmem_limit_bytes=_vmem_limit(block_bytes, weight_bytes, 6 * t * d * 4)),
        name="in_proj",
    )(x, mod, g_pre, w_a, w_qt, w_vt, v_ones, w_small, head_sel)


BIAS_SLOTS = 16


def _bias_placement():
    pk = np.zeros((3, SMALL_WIDTH, FOX_WIDTH), np.float32)
    k_const = np.zeros((1, FOX_WIDTH), np.float32)
    pq = np.zeros((3, FOX_HEADS * BIAS_SLOTS, SMALL_WIDTH), np.float32)
    q_const = np.zeros((FOX_HEADS * BIAS_SLOTS, 1), np.float32)
    for head in range(FOX_HEADS):
        pair, odd = divmod(head, 2)
        for part in range(3):
            pk[part, head, pair * LANES + 3 + 3 * odd + part] = -1.0
            pq[part, head * BIAS_SLOTS + part, head] = 1.0
            q_const[head * BIAS_SLOTS + 3 + 3 * odd + part, 0] = 1.0
            k_const[0, pair * LANES + part] = 1.0
    return pk, k_const, pq, q_const


def _forget_kernel(small_ref, b_ref, pk_ref, kc_ref, pq_ref, qc_ref, kx_ref, qx_ref, carry_ref):
    @pl.when(pl.program_id(1) == 0)
    def _():
        carry_ref[...] = jnp.zeros_like(carry_ref)

    t = small_ref.shape[1]
    log_f = _log_sigmoid(small_ref[0] + b_ref[...])
    row = lax.broadcasted_iota(jnp.int32, (t, t), 0)
    col = lax.broadcasted_iota(jnp.int32, (t, t), 1)
    tri = jnp.where(col <= row, 1.0, 0.0).astype(BF16)
    cs = _ones_dot_f32(tri, log_f) + carry_ref[...]
    carry_ref[...] = cs[t - 1:t, :]
    parts = _split3(cs * LOG2_E)
    kx = kc_ref[...]
    qx = qc_ref[...]
    for part in range(3):
        kx = kx + _dot(parts[part], pk_ref[part])
        qx = qx + _dot_nt(pq_ref[part], parts[part])
    kx_ref[0] = kx.astype(BF16)
    qx_ref[0] = qx.astype(BF16)


def _forget_cumsum(small, b_forget_pad):
    b, s, _ = small.shape
    t = min(FORGET_TOKENS, s)
    pk, k_const, pq, q_const = _bias_placement()
    q_rows = FOX_HEADS * BIAS_SLOTS
    return pl.pallas_call(
        _forget_kernel,
        grid=(b, s // t),
        in_specs=[pl.BlockSpec((1, t, SMALL_WIDTH), lambda bi, i: (bi, i, 0)),
                  _resident((1, SMALL_WIDTH)), _resident(pk.shape), _resident(k_const.shape),
                  _resident(pq.shape), _resident(q_const.shape)],
        out_specs=(pl.BlockSpec((1, t, FOX_WIDTH), lambda bi, i: (bi, i, 0)),
                   pl.BlockSpec((1, q_rows, t), lambda bi, i: (bi, 0, i))),
        out_shape=(jax.ShapeDtypeStruct((b, s, FOX_WIDTH), BF16),
                   jax.ShapeDtypeStruct((b, q_rows, s), BF16)),
        scratch_shapes=[pltpu.VMEM((1, SMALL_WIDTH), F32)],
        compiler_params=pltpu.CompilerParams(
            dimension_semantics=("parallel", "arbitrary"),
            vmem_limit_bytes=_vmem_limit(t * (SMALL_WIDTH * 4 + FOX_WIDTH * 2 + q_rows * 2), pk.size * 2 + pq.size * 2,
                                         2 * t * t * 4 + 8 * t * FOX_WIDTH * 4)),
        name="forget_cumsum",
    )(small, b_forget_pad, jnp.asarray(pk, BF16), jnp.asarray(k_const), jnp.asarray(pq, BF16), jnp.asarray(q_const))


def _fox_kernel(qt_ref, qx_ref, k_ref, kx_ref, vt_ref, kn2_ref, flast_ref, o_ref, s_ref, p_ref, acc_ref):
    t = qt_ref.shape[2]
    i = pl.program_id(2)
    n_heads = 2 * FOX_PAIRS
    row = lax.broadcasted_iota(jnp.int32, (LANES, t), 0)
    low = row < FOX_HEAD_DIM
    pad = jnp.zeros((LANES - BIAS_SLOTS, t), BF16)
    q_aug = []
    for hd in range(n_heads):
        qt = qt_ref[0, (hd // 2) * LANES:(hd // 2 + 1) * LANES, :]
        own = jnp.where(low, qt, jnp.zeros_like(qt)) if hd % 2 == 0 else jnp.where(low, jnp.zeros_like(qt), qt)
        q_aug.append(jnp.concatenate([own, qx_ref[0, hd * BIAS_SLOTS:(hd + 1) * BIAS_SLOTS, :], pad], axis=0))

    def logits(j, masked):
        k0 = pl.multiple_of(j * t, t)
        out = []
        for pr in range(FOX_PAIRS):
            lanes = slice(pr * LANES, (pr + 1) * LANES)
            k_aug = jnp.concatenate([k_ref[0, pl.ds(k0, t), lanes], kx_ref[0, pl.ds(k0, t), lanes]], axis=1)
            for h in range(2):
                s = _dot(k_aug, q_aug[2 * pr + h])
                if masked:
                    r = lax.broadcasted_iota(jnp.int32, (t, t), 0)
                    c = lax.broadcasted_iota(jnp.int32, (t, t), 1)
                    s = jnp.where(r <= c, s, MASK_VALUE)
                out.append(s)
        return out

    def weighted_values(j, hd, p):
        k0 = pl.multiple_of(j * t, t)
        return _dot(vt_ref[0, hd * V_ROWS:(hd + 1) * V_ROWS, pl.ds(k0, t)], p)

    def stage_logits(j, masked, slot):
        maxes = []
        for hd, s in enumerate(logits(j, masked)):
            s_ref[slot, hd] = s
            maxes.append(jnp.max(s, axis=0, keepdims=True))
        return maxes

    def stage_softmax(hd, s_max, m, s_slot, p_slot):
        m_new = jnp.maximum(m, s_max)
        p_ref[p_slot, hd] = jnp.exp2(s_ref[s_slot, hd] - m_new).astype(BF16)
        return m_new, jnp.exp2(m - m_new)

    m0 = jnp.full((1, t), MASK_VALUE, F32)
    s_max = stage_logits(i, True, 1)
    stats = [stage_softmax(hd, s_max[hd], m0, 1, 0) for hd in range(n_heads)]
    s_max = stage_logits(jnp.maximum(i - 1, 0), False, 0)
    acc_ref[...] = jnp.zeros_like(acc_ref)

    tile_id = lax.broadcasted_iota(jnp.int32, (1, LANES), 1).astype(F32)
    first_head = pl.program_id(1) * n_heads
    first_needed = []
    for hd in range(n_heads):
        base = hd * BIAS_SLOTS
        f_k = sum(flast_ref[0, base + part:base + part + 1, :].astype(F32) for part in range(3))
        f_q = sum(qx_ref[0, base + part:base + part + 1, 0:1].astype(F32) for part in range(3))
        q = qt_ref[0, hd * FOX_HEAD_DIM:(hd + 1) * FOX_HEAD_DIM, :].astype(F32)
        qn2 = jnp.max(jnp.sum(q * q, axis=0, keepdims=True), axis=1, keepdims=True)
        qk = jnp.sqrt(qn2 * kn2_ref[0, pl.ds(first_head + hd, 1), 0:1]) * NORM_SLACK
        m_min = jnp.min(stats[hd][0], axis=1, keepdims=True)
        needed = (qk + f_q - f_k >= m_min - SKIP_LOG2) & (tile_id < i.astype(F32))
        first_needed.append(jnp.min(jnp.where(needed, tile_id, i.astype(F32)), axis=1, keepdims=True))
    first_tile = jnp.min(jnp.concatenate(first_needed, axis=1), axis=1, keepdims=True)[0, 0].astype(jnp.int32)
    n_trips = i - first_tile

    def trip(par, k, carry):
        prev, heads = carry
        for hd in range(n_heads):
            pv = weighted_values(prev, hd, p_ref[par, hd])
            acc_ref[hd] = heads[hd][1] * acc_ref[hd] + pv
        stats = [stage_softmax(hd, heads[hd][2], heads[hd][0], par, 1 - par) for hd in range(n_heads)]
        s_max = stage_logits(jnp.maximum(i - 2 - k, 0), False, 1 - par)
        return i - 1 - k, tuple(stats[hd] + (s_max[hd],) for hd in range(n_heads))

    def either_trip(k, carry):
        return lax.cond((k & 1) == 0, functools.partial(trip, 0, k), functools.partial(trip, 1, k), carry)

    prev, heads = lax.fori_loop(0, n_trips, either_trip, (i, tuple(stats[hd] + (s_max[hd],) for hd in range(n_heads))))
    o_heads = []
    for hd in range(n_heads):
        acc = heads[hd][1] * acc_ref[hd] + weighted_values(prev, hd, p_ref[n_trips & 1, hd])
        o_heads.append(acc[0:FOX_HEAD_DIM] * (1.0 / acc[FOX_HEAD_DIM:FOX_HEAD_DIM + 1]))
    o_ref[0] = jnp.concatenate(o_heads, axis=0).T.astype(o_ref.dtype)


def _fox_attention(fqt, qx, fk, kx, fvt, kn2, flast):
    b, s, _ = fk.shape
    t = min(FOX_TILE, s)
    width = FOX_PAIRS * LANES
    n_heads = 2 * FOX_PAIRS
    groups = FOX_WIDTH // width
    whole = lambda shape, index_map: pl.BlockSpec(shape, index_map, pipeline_mode=pl.Buffered(1))
    block_bytes = (width + n_heads * BIAS_SLOTS) * t * 2 + t * width * 2
    resident_bytes = (2 * width + n_heads * V_ROWS) * s * 2 + n_heads * t * (2 * t * 6 + V_ROWS * 4)
    return pl.pallas_call(
        _fox_kernel,
        grid=(b, groups, s // t),
        in_specs=[pl.BlockSpec((1, width, t), lambda bi, g, i: (bi, g, i)),
                  pl.BlockSpec((1, n_heads * BIAS_SLOTS, t), lambda bi, g, i: (bi, g, i)),
                  whole((1, s, width), lambda bi, g, i: (bi, 0, g)),
                  whole((1, s, width), lambda bi, g, i: (bi, 0, g)),
                  whole((1, n_heads * V_ROWS, s), lambda bi, g, i: (bi, g, 0)),
                  whole((1, KN_ROWS, LANES), lambda bi, g, i: (bi, 0, 0)),
                  whole((1, n_heads * BIAS_SLOTS, flast.shape[2]), lambda bi, g, i: (bi, g, 0))],
        out_specs=pl.BlockSpec((1, t, width), lambda bi, g, i: (bi, i, g)),
        out_shape=jax.ShapeDtypeStruct((b, s, FOX_WIDTH), BF16),
        scratch_shapes=[pltpu.VMEM((2, n_heads, t, t), F32), pltpu.VMEM((2, n_heads, t, t), BF16),
                        pltpu.VMEM((n_heads, V_ROWS, t), F32)],
        compiler_params=pltpu.CompilerParams(
            dimension_semantics=("parallel", "parallel", "parallel"),
            vmem_limit_bytes=_vmem_limit(block_bytes, resident_bytes, 8 * n_heads * t * t)),
        name="fox_attention",
    )(fqt, qx, fk, kx, fvt, kn2, flast)


def _gla_kernel(q_ref, k_ref, v_ref, small_ref, wg_ref, bg_ref, gn_ref, o_ref, state_ref):
    @pl.when(pl.program_id(1) == 0)
    def _():
        state_ref[...] = jnp.zeros_like(state_ref)

    nb, t, _ = q_ref.shape
    c = GLA_CHUNK
    half = c // 2
    row = lax.broadcasted_iota(jnp.int32, (t, t), 0)
    col = lax.broadcasted_iota(jnp.int32, (t, t), 1)
    same_chunk = _div_pow2(row, c) == _div_pow2(col, c)
    tri = jnp.where(same_chunk & (col <= row), 1.0, 0.0).astype(BF16)
    cum_all = []
    for bb in range(nb):
        gate_logits = _dot(small_ref[bb].astype(BF16), wg_ref[...]) + bg_ref[...]
        cum_all.append(_ones_dot_f32(tri, _log_sigmoid(gate_logits) * (1.0 / GLA_TAU)))

    crow = lax.broadcasted_iota(jnp.int32, (c, GLA_K_WIDTH), 0)
    first = crow < half
    lane = lax.broadcasted_iota(jnp.int32, (c, GLA_K_WIDTH), 1)
    head_lanes = [_div_pow2(lane, GLA_KEY_DIM) == h for h in range(GLA_HEADS)]
    sr = lax.broadcasted_iota(jnp.int32, (GLA_HEADS * c, c), 0) & (c - 1)
    sc = lax.broadcasted_iota(jnp.int32, (GLA_HEADS * c, c), 1)
    near = (_div_pow2(sr, half) == _div_pow2(sc, half)) & (sc <= sr)
    srow = lax.broadcasted_iota(jnp.int32, state_ref.shape[1:], 0)
    slane = lax.broadcasted_iota(jnp.int32, state_ref.shape[1:], 1)
    own_head = _div_pow2(srow, GLA_VAL_DIM) == _div_pow2(slane, GLA_KEY_DIM)

    def stack_heads(x):
        return jnp.concatenate([jnp.where(m, x, 0.0) for m in head_lanes], axis=0).astype(BF16)

    for ci, bb in [(ci, bb) for ci in range(t // c) for bb in range(nb)]:
        r0 = ci * c
        cum = cum_all[bb][r0:r0 + c]
        q = q_ref[bb, r0:r0 + c, :].astype(F32)
        k = k_ref[bb, r0:r0 + c, :].astype(F32)
        v = v_ref[bb, r0:r0 + c, :]
        last = cum[c - 1:c]
        ref_far = cum[half - 1:half]
        ref_near = jnp.where(first, cum[half // 2:half // 2 + 1], cum[half + half // 2:half + half // 2 + 1])
        q_far = jnp.where(first, 0.0, q * jnp.exp(cum - ref_far))
        k_far = jnp.where(first, k * jnp.exp(ref_far - cum), 0.0)
        q_near = q * jnp.exp(cum - ref_near)
        k_near = k * jnp.exp(ref_near - cum)
        scores = (_dot_nt(stack_heads(q_far), k_far.astype(BF16))
                  + jnp.where(near, _dot_nt(stack_heads(q_near), k_near.astype(BF16)), 0.0))
        state = state_ref[bb]
        o = _dot_nt((q * jnp.exp(cum)).astype(BF16), state.astype(BF16))
        p = scores.astype(BF16)
        for h in range(GLA_HEADS):
            lo, hi = h * GLA_VAL_DIM, (h + 1) * GLA_VAL_DIM
            o_h = o[:, lo:hi] + _dot(p[h * c:(h + 1) * c], v[:, lo:hi])
            y = o_h * lax.rsqrt(jnp.mean(o_h * o_h, axis=-1, keepdims=True) + RMS_EPS) * gn_ref[:, lo:hi]
            o_ref[bb, r0:r0 + c, lo:hi] = y.astype(o_ref.dtype)
        k_tail = (k * jnp.exp(last - cum)).astype(BF16)
        update = _dot_tn(v, k_tail)
        state_ref[bb] = state * jnp.exp(last) + jnp.where(own_head, update, 0.0)


def _gla(gq, gk, gv, small, w_gate_pad, b_gate, g_norm):
    b, s, _ = gq.shape
    t = min(GLA_TOKENS, s)
    nb = GLA_BATCH if b % GLA_BATCH == 0 else 1
    tok = lambda width: pl.BlockSpec((nb, t, width), lambda bi, i: (bi, i, 0))
    block_bytes = nb * t * (2 * GLA_K_WIDTH * 2 + 2 * GLA_V_WIDTH * 2 + SMALL_WIDTH * 4)
    state_bytes = nb * GLA_V_WIDTH * GLA_K_WIDTH * 4
    return pl.pallas_call(
        _gla_kernel,
        grid=(b // nb, s // t),
        in_specs=[tok(GLA_K_WIDTH), tok(GLA_K_WIDTH), tok(GLA_V_WIDTH), tok(SMALL_WIDTH),
                  _resident(w_gate_pad.shape), _resident((1, GLA_K_WIDTH)), _resident((1, GLA_V_WIDTH))],
        out_specs=tok(GLA_V_WIDTH),
        out_shape=jax.ShapeDtypeStruct((b, s, GLA_V_WIDTH), BF16),
        scratch_shapes=[pltpu.VMEM((nb, GLA_V_WIDTH, GLA_K_WIDTH), F32)],
        compiler_params=pltpu.CompilerParams(
            dimension_semantics=("parallel", "arbitrary"),
            vmem_limit_bytes=_vmem_limit(block_bytes, state_bytes, 8 * state_bytes + 16 * t * GLA_K_WIDTH * 4)),
        name="gla",
    )(gq, gk, gv, small, w_gate_pad, b_gate, g_norm)


_B_POOL = 0
_B_Z = _B_POOL + POOL_WIDTH
_B_MERGE = _B_Z + FOX_WIDTH + GLA_V_WIDTH + POOL_WIDTH
_B_WIDTH = _B_MERGE + N_BRANCHES * D_MODEL


def _merge_kernel(x_ref, mod_ref, gpre_ref, gpost_ref, ofox_ref, ogla_ref, wb_ref, wpool_ref, pscale_ref,
                  wbr_ref, wout_ref, o_ref, u_ref):
    t = x_ref.shape[1]
    i = pl.program_id(1)
    x = x_ref[0]
    mod = mod_ref[0]
    h = _modulated_norm(x, gpre_ref[...], mod)

    @pl.when(i == 0)
    def _():
        u_ref[0:POOL_HALO, :] = jnp.zeros((POOL_HALO, POOL_WIDTH), F32)

    @pl.when(i > 0)
    def _():
        u_ref[0:POOL_HALO, :] = u_ref[t:t + POOL_HALO, :]

    u = _dot(h, wb_ref[:, _B_POOL:_B_Z])
    u_ref[POOL_HALO:POOL_HALO + t, :] = u
    count = (i * t + 1 + lax.broadcasted_iota(jnp.int32, (t, 1), 0)).astype(F32)
    pooled = []
    for g, w in enumerate(POOL_WINDOWS):
        lo, hi = g * POOL_GROUP_DIM, (g + 1) * POOL_GROUP_DIM
        window = u[:, lo:hi]
        for back in range(1, w):
            window = window + u_ref[pl.ds(POOL_HALO - back, t), lo:hi]
        mean = window / jnp.minimum(count, float(w))
        diff = (mean - u[:, lo:hi]).astype(BF16)
        pooled.append(_dot(diff, wpool_ref[g]) * pscale_ref[:, lo:hi])
    o_pool = jnp.concatenate(pooled, axis=-1)

    branches = (ofox_ref[0].astype(F32), ogla_ref[0].astype(F32), o_pool)
    merged = jnp.zeros((t, D_MODEL), F32)
    for br, o_br in enumerate(branches):
        z = _dot(h, wb_ref[:, _B_Z + br * FOX_WIDTH:_B_Z + (br + 1) * FOX_WIDTH])
        y = _dot((o_br * _silu(z)).astype(BF16), wbr_ref[br])
        m = _dot(h, wb_ref[:, _B_MERGE + br * D_MODEL:_B_MERGE + (br + 1) * D_MODEL])
        merged = merged + _sigmoid(m) * y
    out = _dot(merged.astype(BF16), wout_ref[...])
    out = out * lax.rsqrt(jnp.mean(out * out, axis=-1, keepdims=True) + RMS_EPS) * gpost_ref[...]
    gate = mod[:, 2 * D_MODEL:3 * D_MODEL]
    o_ref[0] = x + gate * out


def _merge(x, mod, g_pre, g_post, o_fox, o_gla, w_b, w_pool, pool_scale, w_br, w_out):
    b, s, d = x.shape
    t = min(MERGE_TOKENS, s)
    tok = lambda width: pl.BlockSpec((1, t, width), lambda bi, i: (bi, i, 0))
    block_bytes = 2 * t * d * 4 + 2 * t * FOX_WIDTH * 2
    weight_bytes = (w_b.size + w_pool.size + w_br.size + w_out.size) * 2
    return pl.pallas_call(
        _merge_kernel,
        grid=(b, s // t),
        in_specs=[tok(d),
                  pl.BlockSpec((1, 1, 3 * d), lambda bi, i: (bi, 0, 0)),
                  _resident((1, d)), _resident((1, d)),
                  tok(FOX_WIDTH), tok(GLA_V_WIDTH),
                  _resident(w_b.shape), _resident(w_pool.shape), _resident((1, POOL_WIDTH)),
                  _resident(w_br.shape), _resident(w_out.shape)],
        out_specs=tok(d),
        out_shape=jax.ShapeDtypeStruct((b, s, d), F32),
        scratch_shapes=[pltpu.VMEM((POOL_HALO + t, POOL_WIDTH), F32)],
        compiler_params=pltpu.CompilerParams(
            dimension_semantics=("parallel", "arbitrary"),
            vmem_limit_bytes=_vmem_limit(block_bytes, weight_bytes, 12 * t * d * 4)),
        name="merge",
    )(x, mod, g_pre, g_post, o_fox, o_gla, w_b, w_pool, pool_scale, w_br, w_out)


def _layer(x, mod, g_pre, g_post, w_in, b_forget, w_gla_gate, b_gla_gate, g_gla_norm, w_pool, pool_scale,
           w_br_fox, w_br_gla, w_br_pool, w_out):
    b, s, d = x.shape
    seg = lambda n: w_in[:, _OFF[n]:_OFF[n + 1]]
    w_a = jnp.concatenate([seg(1), seg(4) * GLA_KEY_DIM ** -0.5, seg(5), seg(6)], axis=1).astype(BF16)
    w_qt = (seg(0) * (LOG2_E * FOX_HEAD_DIM ** -0.5)).T.astype(BF16)
    w_vt = jnp.pad(seg(2).T.reshape(FOX_HEADS, FOX_HEAD_DIM, d), ((0, 0), (0, V_ROWS - FOX_HEAD_DIM), (0, 0)))
    w_vt = w_vt.reshape(FOX_HEADS * V_ROWS, d).astype(BF16)
    v_ones = jnp.tile(jnp.arange(V_ROWS) >= FOX_HEAD_DIM, FOX_HEADS).astype(F32).reshape(FOX_HEADS * V_ROWS, 1)
    pad = jnp.zeros((d, SMALL_WIDTH - FOX_HEADS - GLA_GATE_RANK), w_in.dtype)
    w_small = jnp.concatenate([seg(3), seg(7), pad], axis=1).astype(BF16)
    w_b = w_in[:, _OFF[8]:_OFF[13]].astype(BF16)
    b_forget_pad = jnp.zeros((1, SMALL_WIDTH), F32).at[0, :FOX_HEADS].set(b_forget)
    w_gate_pad = jnp.zeros((SMALL_WIDTH, GLA_K_WIDTH), F32).at[FOX_HEADS:FOX_HEADS + GLA_GATE_RANK].set(
        w_gla_gate).astype(BF16)
    w_br = jnp.stack([w_br_fox, w_br_gla, w_br_pool]).astype(BF16)
    mod3 = mod.reshape(b, 1, 3 * d)
    g_pre = g_pre.reshape(1, d)

    head_sel = (jnp.arange(KN_ROWS)[:, None] == jnp.arange(FOX_WIDTH)[None, :] // FOX_HEAD_DIM).astype(BF16)
    fk, gq, gk, gv, fqt, fvt, small, kn2 = _in_proj(x, mod3, g_pre, w_a, w_qt, w_vt, v_ones, w_small, head_sel)
    kx, qx = _forget_cumsum(small, b_forget_pad)
    t_fox = min(FOX_TILE, s)
    assert s // t_fox <= LANES, "the tile-skip bound keeps one key tile per lane"
    flast = qx[:, :, t_fox - 1::t_fox]
    flast = jnp.pad(flast, ((0, 0), (0, 0), (0, -flast.shape[2] % LANES)))
    o_fox = _fox_attention(fqt, qx, fk, kx, fvt, kn2, flast)
    o_gla = _gla(gq, gk, gv, small, w_gate_pad, b_gla_gate.reshape(1, GLA_K_WIDTH),
                 g_gla_norm.reshape(1, GLA_V_WIDTH))
    return _merge(x, mod3, g_pre, g_post.reshape(1, d), o_fox, o_gla, w_b, w_pool.astype(BF16),
                  pool_scale.reshape(1, POOL_WIDTH), w_br, w_out.astype(BF16))


def kernel(x, c, w_ada, b_ada, g_pre, g_post, w_in, b_forget, w_gla_gate, b_gla_gate, g_gla_norm, w_pool,
           pool_scale, w_br_fox, w_br_gla, w_br_pool, w_out):
    mods = _adaln_mod(c, w_ada, b_ada)
    h = x
    for i in range(w_in.shape[0]):
        h = _layer(h, mods[i], g_pre[i], g_post[i], w_in[i], b_forget[i], w_gla_gate[i], b_gla_gate[i],
                   g_gla_norm[i], w_pool[i], pool_scale[i], w_br_fox[i], w_br_gla[i], w_br_pool[i], w_out[i])
    return h
```

```python
import functools

import numpy as np
import jax
import jax.numpy as jnp
from jax import lax
from jax.experimental import pallas as pl
from jax.experimental.pallas import tpu as pltpu

F32 = jnp.float32
BF16 = jnp.bfloat16

D_MODEL = 1024
FOX_HEADS = 8
FOX_HEAD_DIM = 64
FOX_WIDTH = FOX_HEADS * FOX_HEAD_DIM
GLA_HEADS = 4
GLA_KEY_DIM = 64
GLA_VAL_DIM = 128
GLA_K_WIDTH = GLA_HEADS * GLA_KEY_DIM
GLA_V_WIDTH = GLA_HEADS * GLA_VAL_DIM
GLA_GATE_RANK = 16
GLA_TAU = 16.0
POOL_WINDOWS = (2, 4, 8, 16)
POOL_GROUP_DIM = 128
POOL_WIDTH = len(POOL_WINDOWS) * POOL_GROUP_DIM
N_BRANCHES = 3
RMS_EPS = 1e-6
IN_SPLITS = (FOX_WIDTH, FOX_WIDTH, FOX_WIDTH, FOX_HEADS,
             GLA_K_WIDTH, GLA_K_WIDTH, GLA_V_WIDTH, GLA_GATE_RANK,
             POOL_WIDTH, FOX_WIDTH, GLA_V_WIDTH, POOL_WIDTH, N_BRANCHES * D_MODEL)
_OFF = np.concatenate([[0], np.cumsum(IN_SPLITS)]).tolist()

LANES = 128
MXU_WIDTH = 256
SMALL_WIDTH = LANES
V_ROWS = FOX_HEAD_DIM + 16
POOL_HALO = 8 * len(POOL_WINDOWS)
MASK_VALUE = -1e30
LOG2_E = 1.4426950408889634
SKIP_LOG2 = 160.0
NORM_SLACK = 1.02
V7X_VMEM_BYTES = 64 * 1024 * 1024

IN_TOKENS = 512
FORGET_TOKENS = 2048
FOX_TILE = 256
FOX_PAIRS = 4
GLA_TOKENS = 256
GLA_CHUNK = 64
GLA_BATCH = 2
MERGE_TOKENS = 512


def _vmem_limit(block_bytes, resident_bytes, temp_bytes):
    need = 2 * block_bytes + resident_bytes + temp_bytes
    return int(min(max(need, 16 * 1024 * 1024), V7X_VMEM_BYTES - 8 * 1024 * 1024))


def _resident(shape):
    zeros = (0,) * len(shape)
    return pl.BlockSpec(shape, lambda *_: zeros, pipeline_mode=pl.Buffered(1))


def _div_pow2(i, n):
    shift = n.bit_length() - 1
    assert 1 << shift == n
    return jnp.right_shift(i, shift)


def _log_sigmoid(x):
    return jnp.minimum(x, 0.0) - jnp.log(1.0 + jnp.exp(-jnp.abs(x)))


def _sigmoid(x):
    return jax.nn.sigmoid(x)


def _silu(x):
    return x * _sigmoid(x)


def _dot(a, b):
    return jnp.dot(a, b, preferred_element_type=F32)


def _dot_nt(a, b):
    return lax.dot_general(a, b, (((1,), (1,)), ((), ())), preferred_element_type=F32)


def _dot_tn(a, b):
    return lax.dot_general(a, b, (((0,), (0,)), ((), ())), preferred_element_type=F32)


def _split3(x):
    hi = x.astype(BF16)
    r1 = x - hi.astype(F32)
    mid = r1.astype(BF16)
    lo = (r1 - mid.astype(F32)).astype(BF16)
    return hi, mid, lo


def _ones_dot_f32(ones_bf16, x):
    hi, mid, lo = _split3(x)
    return _dot(ones_bf16, hi) + _dot(ones_bf16, mid) + _dot(ones_bf16, lo)


def _modulated_norm(x, g, mod):
    y = x * lax.rsqrt(jnp.mean(x * x, axis=-1, keepdims=True) + RMS_EPS) * g
    shift = mod[:, 0:D_MODEL]
    scale = mod[:, D_MODEL:2 * D_MODEL]
    return (y * (1.0 + scale) + shift).astype(BF16)


def _mod_kernel(c_ref, w_ref, b_ref, o_ref):
    c = c_ref[...]
    o_ref[0] = jnp.dot(_silu(c), w_ref[0], preferred_element_type=F32,
                       precision=lax.Precision.HIGHEST) + b_ref[0]


def _adaln_mod(c, w_ada, b_ada):
    depth, d, d3 = w_ada.shape
    b = c.shape[0]
    n_col = d3 // d
    return pl.pallas_call(
        _mod_kernel,
        grid=(depth, n_col),
        in_specs=[pl.BlockSpec((b, d), lambda l, j: (0, 0)),
                  pl.BlockSpec((1, d, d), lambda l, j: (l, 0, j)),
                  pl.BlockSpec((1, 1, d), lambda l, j: (l, 0, j))],
        out_specs=pl.BlockSpec((1, b, d), lambda l, j: (l, 0, j)),
        out_shape=jax.ShapeDtypeStruct((depth, b, d3), F32),
        compiler_params=pltpu.CompilerParams(
            dimension_semantics=("parallel", "parallel"),
            vmem_limit_bytes=_vmem_limit(d * d * 4 + 2 * b * d * 4, 0, 4 * b * d * 4)),
        name="adaln_mod",
    )(c, w_ada, b_ada.reshape(depth, 1, d3))


_A_FOX_K = 0
_A_GLA_Q = _A_FOX_K + FOX_WIDTH
_A_GLA_K = _A_GLA_Q + GLA_K_WIDTH
_A_GLA_V = _A_GLA_K + GLA_K_WIDTH
_A_WIDTH = _A_GLA_V + GLA_V_WIDTH
KN_ROWS = 16


def _in_proj_kernel(x_ref, mod_ref, g_ref, wa_ref, wqt_ref, wvt_ref, vones_ref, ws_ref, hsel_ref,
                    fk_ref, gq_ref, gk_ref, gv_ref, fqt_ref, fvt_ref, small_ref, kn2_ref, fft_ref):
    h = _modulated_norm(x_ref[0], g_ref[...], mod_ref[0])
    fk = _dot(h, wa_ref[:, _A_FOX_K:_A_GLA_Q]).astype(BF16)
    fk_ref[0] = fk
    k2 = fk.astype(F32)
    norms = _dot_nt(hsel_ref[...], (k2 * k2).astype(BF16))
    tile_max = jnp.broadcast_to(jnp.max(norms, axis=1, keepdims=True), kn2_ref.shape[1:])

    @pl.when(pl.program_id(1) == 0)
    def _():
        kn2_ref[0] = tile_max

    @pl.when(pl.program_id(1) > 0)
    def _():
        kn2_ref[0] = jnp.maximum(kn2_ref[0], tile_max)

    gq_ref[0] = _dot(h, wa_ref[:, _A_GLA_Q:_A_GLA_K]).astype(BF16)
    gk_ref[0] = _dot(h, wa_ref[:, _A_GLA_K:_A_GLA_V]).astype(BF16)
    gv_ref[0] = _dot(h, wa_ref[:, _A_GLA_V:_A_WIDTH]).astype(BF16)
    qt = _dot_nt(wqt_ref[...], h)
    fqt_ref[0] = qt[0:FOX_WIDTH].astype(BF16)
    fft_ref[0] = qt[FOX_WIDTH:FOX_WIDTH + KN_ROWS]
    fvt_ref[0] = (_dot_nt(wvt_ref[...], h) + vones_ref[...]).astype(BF16)
    small_ref[0] = _dot(h, ws_ref[...])


def _in_proj(x, mod, g_pre, w_a, w_qt, w_vt, v_ones, w_small, head_sel):
    b, s, d = x.shape
    t = min(IN_TOKENS, s)
    tok = lambda width: pl.BlockSpec((1, t, width), lambda bi, i: (bi, i, 0))
    tok_t = lambda rows: pl.BlockSpec((1, rows, t), lambda bi, i: (bi, 0, i))
    v_rows = FOX_HEADS * V_ROWS
    out_shapes = (
        jax.ShapeDtypeStruct((b, s, FOX_WIDTH), BF16),
        jax.ShapeDtypeStruct((b, s, GLA_K_WIDTH), BF16),
        jax.ShapeDtypeStruct((b, s, GLA_K_WIDTH), BF16),
        jax.ShapeDtypeStruct((b, s, GLA_V_WIDTH), BF16),
        jax.ShapeDtypeStruct((b, FOX_WIDTH, s), BF16),
        jax.ShapeDtypeStruct((b, v_rows, s), BF16),
        jax.ShapeDtypeStruct((b, s, SMALL_WIDTH), F32),
        jax.ShapeDtypeStruct((b, KN_ROWS, LANES), F32),
        jax.ShapeDtypeStruct((b, KN_ROWS, s), F32),
    )
    out_specs = (tok(FOX_WIDTH), tok(GLA_K_WIDTH), tok(GLA_K_WIDTH), tok(GLA_V_WIDTH), tok_t(FOX_WIDTH), tok_t(v_rows),
                 tok(SMALL_WIDTH), pl.BlockSpec((1, KN_ROWS, LANES), lambda bi, i: (bi, 0, 0)), tok_t(KN_ROWS))
    n_out = _A_WIDTH + FOX_WIDTH + v_rows
    block_bytes = t * d * 4 + t * n_out * 2 + t * SMALL_WIDTH * 4
    weight_bytes = d * (n_out + SMALL_WIDTH) * 2
    return pl.pallas_call(
        _in_proj_kernel,
        grid=(b, s // t),
        in_specs=[tok(d),
                  pl.BlockSpec((1, 1, 3 * d), lambda bi, i: (bi, 0, 0)),
                  _resident((1, d)), _resident(w_a.shape), _resident(w_qt.shape), _resident(w_vt.shape),
                  _resident(v_ones.shape), _resident(w_small.shape), _resident(head_sel.shape)],
        out_specs=out_specs,
        out_shape=out_shapes,
        compiler_params=pltpu.CompilerParams(
            dimension_semantics=("parallel", "arbitrary"),
            vmem_limit_bytes=_vmem_limit(block_bytes, weight_bytes, 6 * t * d * 4)),
        name="in_proj",
    )(x, mod, g_pre, w_a, w_qt, w_vt, v_ones, w_small, head_sel)


BIAS_SLOTS = 16


def _bias_placement():
    pk = np.zeros((3 * KN_ROWS, FOX_WIDTH), np.float32)
    k_const = np.zeros((1, FOX_WIDTH), np.float32)
    pq = np.zeros((FOX_HEADS * BIAS_SLOTS, 3 * KN_ROWS), np.float32)
    q_const = np.zeros((FOX_HEADS * BIAS_SLOTS, 1), np.float32)
    for head in range(FOX_HEADS):
        pair, odd = divmod(head, 2)
        for part in range(3):
            pk[part * KN_ROWS + head, pair * LANES + 3 + 3 * odd + part] = -1.0
            pq[head * BIAS_SLOTS + part, part * KN_ROWS + head] = 1.0
            q_const[head * BIAS_SLOTS + 3 + 3 * odd + part, 0] = 1.0
            k_const[0, pair * LANES + part] = 1.0
    return pk, k_const, pq, q_const


def _forget_kernel(fft_ref, b_ref, pk_ref, kc_ref, pq_ref, qc_ref, kx_ref, qx_ref, carry_ref):
    @pl.when(pl.program_id(1) == 0)
    def _():
        carry_ref[...] = jnp.zeros_like(carry_ref)

    t = fft_ref.shape[2]
    w = min(MXU_WIDTH, t)
    log_f = _log_sigmoid(fft_ref[0] + b_ref[...])
    row = lax.broadcasted_iota(jnp.int32, (w, w), 0)
    col = lax.broadcasted_iota(jnp.int32, (w, w), 1)
    upper = jnp.where(row <= col, 1.0, 0.0).astype(BF16)
    split = jnp.concatenate(_split3(log_f), axis=0)
    carry = carry_ref[:, 0:1]
    blocks = []
    for j in range(t // w):
        sums = _dot(split[:, j * w:(j + 1) * w], upper)
        blocks.append(sums[0:KN_ROWS] + sums[KN_ROWS:2 * KN_ROWS] + sums[2 * KN_ROWS:3 * KN_ROWS] + carry)
        carry = blocks[-1][:, w - 1:w]
    carry_ref[...] = jnp.broadcast_to(carry, carry_ref.shape)
    cs = jnp.concatenate(blocks, axis=1)
    parts = jnp.concatenate(_split3(cs * LOG2_E), axis=0)
    qx_ref[0] = (_dot(pq_ref[...], parts) + qc_ref[...]).astype(BF16)
    kx_ref[0] = (_dot_tn(parts, pk_ref[...]) + kc_ref[...]).astype(BF16)


def _forget_cumsum(fft, b_forget_col):
    b, _, s = fft.shape
    t = min(FORGET_TOKENS, s)
    pk, k_const, pq, q_const = _bias_placement()
    q_rows = FOX_HEADS * BIAS_SLOTS
    return pl.pallas_call(
        _forget_kernel,
        grid=(b, s // t),
        in_specs=[pl.BlockSpec((1, KN_ROWS, t), lambda bi, i: (bi, 0, i)),
                  _resident((KN_ROWS, 1)), _resident(pk.shape), _resident(k_const.shape),
                  _resident(pq.shape), _resident(q_const.shape)],
        out_specs=(pl.BlockSpec((1, t, FOX_WIDTH), lambda bi, i: (bi, i, 0)),
                   pl.BlockSpec((1, q_rows, t), lambda bi, i: (bi, 0, i))),
        out_shape=(jax.ShapeDtypeStruct((b, s, FOX_WIDTH), BF16),
                   jax.ShapeDtypeStruct((b, q_rows, s), BF16)),
        scratch_shapes=[pltpu.VMEM((KN_ROWS, LANES), F32)],
        compiler_params=pltpu.CompilerParams(
            dimension_semantics=("parallel", "arbitrary"),
            vmem_limit_bytes=_vmem_limit(t * (KN_ROWS * 4 + FOX_WIDTH * 2 + q_rows * 2), pk.size * 2 + pq.size * 2,
                                         8 * t * FOX_WIDTH * 4)),
        name="forget_cumsum",
    )(fft, b_forget_col, jnp.asarray(pk, BF16), jnp.asarray(k_const), jnp.asarray(pq, BF16), jnp.asarray(q_const))


def _fox_kernel(qt_ref, qx_ref, k_ref, kx_ref, vt_ref, kn2_ref, flast_ref, o_ref, s_ref, p_ref, acc_ref):
    t = qt_ref.shape[2]
    i = pl.program_id(2)
    n_heads = 2 * FOX_PAIRS
    row = lax.broadcasted_iota(jnp.int32, (LANES, t), 0)
    low = row < FOX_HEAD_DIM
    pad = jnp.zeros((LANES - BIAS_SLOTS, t), BF16)
    q_aug = []
    for hd in range(n_heads):
        qt = qt_ref[0, (hd // 2) * LANES:(hd // 2 + 1) * LANES, :]
        own = jnp.where(low, qt, jnp.zeros_like(qt)) if hd % 2 == 0 else jnp.where(low, jnp.zeros_like(qt), qt)
        q_aug.append(jnp.concatenate([own, qx_ref[0, hd * BIAS_SLOTS:(hd + 1) * BIAS_SLOTS, :], pad], axis=0))

    def logits(j, masked):
        k0 = pl.multiple_of(j * t, t)
        out = []
        for pr in range(FOX_PAIRS):
            lanes = slice(pr * LANES, (pr + 1) * LANES)
            k_aug = jnp.concatenate([k_ref[0, pl.ds(k0, t), lanes], kx_ref[0, pl.ds(k0, t), lanes]], axis=1)
            for h in range(2):
                s = _dot(k_aug, q_aug[2 * pr + h])
                if masked:
                    r = lax.broadcasted_iota(jnp.int32, (t, t), 0)
                    c = lax.broadcasted_iota(jnp.int32, (t, t), 1)
                    s = jnp.where(r <= c, s, MASK_VALUE)
                out.append(s)
        return out

    def weighted_values(j, hd, p):
        k0 = pl.multiple_of(j * t, t)
        return _dot(vt_ref[0, hd * V_ROWS:(hd + 1) * V_ROWS, pl.ds(k0, t)], p)

    def stage_logits(j, masked, slot):
        maxes = []
        for hd, s in enumerate(logits(j, masked)):
            s_ref[slot, hd] = s
            maxes.append(jnp.max(s, axis=0, keepdims=True))
        return maxes

    def stage_softmax(hd, s_max, m, s_slot, p_slot):
        m_new = jnp.maximum(m, s_max)
        p_ref[p_slot, hd] = jnp.exp2(s_ref[s_slot, hd] - m_new).astype(BF16)
        return m_new, jnp.exp2(m - m_new)

    m0 = jnp.full((1, t), MASK_VALUE, F32)
    s_max = stage_logits(i, True, 1)
    stats = [stage_softmax(hd, s_max[hd], m0, 1, 0) for hd in range(n_heads)]
    s_max = stage_logits(jnp.maximum(i - 1, 0), False, 0)
    acc_ref[...] = jnp.zeros_like(acc_ref)

    tile_id = lax.broadcasted_iota(jnp.int32, (1, LANES), 1).astype(F32)
    first_head = pl.program_id(1) * n_heads
    first_needed = []
    for hd in range(n_heads):
        base = hd * BIAS_SLOTS
        f_k = sum(flast_ref[0, base + part:base + part + 1, :].astype(F32) for part in range(3))
        f_q = sum(qx_ref[0, base + part:base + part + 1, 0:1].astype(F32) for part in range(3))
        q = qt_ref[0, hd * FOX_HEAD_DIM:(hd + 1) * FOX_HEAD_DIM, :].astype(F32)
        qn2 = jnp.max(jnp.sum(q * q, axis=0, keepdims=True), axis=1, keepdims=True)
        qk = jnp.sqrt(qn2 * kn2_ref[0, pl.ds(first_head + hd, 1), 0:1]) * NORM_SLACK
        m_min = jnp.min(stats[hd][0], axis=1, keepdims=True)
        needed = (qk + f_q - f_k >= m_min - SKIP_LOG2) & (tile_id < i.astype(F32))
        first_needed.append(jnp.min(jnp.where(needed, tile_id, i.astype(F32)), axis=1, keepdims=True))
    first_tile = jnp.min(jnp.concatenate(first_needed, axis=1), axis=1, keepdims=True)[0, 0].astype(jnp.int32)
    n_trips = i - first_tile

    def trip(par, fetch, k, carry):
        prev, heads = carry
        for hd in range(n_heads):
            pv = weighted_values(prev, hd, p_ref[par, hd])
            acc_ref[hd] = heads[hd][1] * acc_ref[hd] + pv
        stats = [stage_softmax(hd, heads[hd][2], heads[hd][0], par, 1 - par) for hd in range(n_heads)]
        s_max = stage_logits(i - 2 - k, False, 1 - par) if fetch else [h[2] for h in heads]
        return i - 1 - k, tuple(stats[hd] + (s_max[hd],) for hd in range(n_heads))

    def either_trip(fetch, k, carry):
        return lax.cond((k & 1) == 0, functools.partial(trip, 0, fetch, k), functools.partial(trip, 1, fetch, k), carry)

    carry = (i, tuple(stats[hd] + (s_max[hd],) for hd in range(n_heads)))
    carry = lax.fori_loop(0, n_trips - 1, functools.partial(either_trip, True), carry)
    prev, heads = lax.cond(n_trips > 0, functools.partial(either_trip, False, n_trips - 1), lambda c: c, carry)
    o_heads = []
    for hd in range(n_heads):
        acc = heads[hd][1] * acc_ref[hd] + weighted_values(prev, hd, p_ref[n_trips & 1, hd])
        o_heads.append(acc[0:FOX_HEAD_DIM] * (1.0 / acc[FOX_HEAD_DIM:FOX_HEAD_DIM + 1]))
    o_ref[0] = jnp.concatenate(o_heads, axis=0).T.astype(o_ref.dtype)


def _fox_attention(fqt, qx, fk, kx, fvt, kn2, flast):
    b, s, _ = fk.shape
    t = min(FOX_TILE, s)
    width = FOX_PAIRS * LANES
    n_heads = 2 * FOX_PAIRS
    groups = FOX_WIDTH // width
    whole = lambda shape, index_map: pl.BlockSpec(shape, index_map, pipeline_mode=pl.Buffered(1))
    block_bytes = (width + n_heads * BIAS_SLOTS) * t * 2 + t * width * 2
    resident_bytes = (2 * width + n_heads * V_ROWS) * s * 2 + n_heads * t * (2 * t * 6 + V_ROWS * 4)
    return pl.pallas_call(
        _fox_kernel,
        grid=(b, groups, s // t),
        in_specs=[pl.BlockSpec((1, width, t), lambda bi, g, i: (bi, g, i)),
                  pl.BlockSpec((1, n_heads * BIAS_SLOTS, t), lambda bi, g, i: (bi, g, i)),
                  whole((1, s, width), lambda bi, g, i: (bi, 0, g)),
                  whole((1, s, width), lambda bi, g, i: (bi, 0, g)),
                  whole((1, n_heads * V_ROWS, s), lambda bi, g, i: (bi, g, 0)),
                  whole((1, KN_ROWS, LANES), lambda bi, g, i: (bi, 0, 0)),
                  whole((1, n_heads * BIAS_SLOTS, flast.shape[2]), lambda bi, g, i: (bi, g, 0))],
        out_specs=pl.BlockSpec((1, t, width), lambda bi, g, i: (bi, i, g)),
        out_shape=jax.ShapeDtypeStruct((b, s, FOX_WIDTH), BF16),
        scratch_shapes=[pltpu.VMEM((2, n_heads, t, t), F32), pltpu.VMEM((2, n_heads, t, t), BF16),
                        pltpu.VMEM((n_heads, V_ROWS, t), F32)],
        compiler_params=pltpu.CompilerParams(
            dimension_semantics=("parallel", "parallel", "parallel"),
            vmem_limit_bytes=_vmem_limit(block_bytes, resident_bytes, 8 * n_heads * t * t)),
        name="fox_attention",
    )(fqt, qx, fk, kx, fvt, kn2, flast)


def _gla_kernel(q_ref, k_ref, v_ref, small_ref, wg_ref, bg_ref, gn_ref, o_ref, state_ref):
    @pl.when(pl.program_id(1) == 0)
    def _():
        state_ref[...] = jnp.zeros_like(state_ref)

    nb, t, _ = q_ref.shape
    c = GLA_CHUNK
    half = c // 2
    row = lax.broadcasted_iota(jnp.int32, (t, t), 0)
    col = lax.broadcasted_iota(jnp.int32, (t, t), 1)
    same_chunk = _div_pow2(row, c) == _div_pow2(col, c)
    tri = jnp.where(same_chunk & (col <= row), 1.0, 0.0).astype(BF16)
    cum_all = []
    for bb in range(nb):
        gate_logits = _dot(small_ref[bb].astype(BF16), wg_ref[...]) + bg_ref[...]
        cum_all.append(_ones_dot_f32(tri, _log_sigmoid(gate_logits) * (1.0 / GLA_TAU)))

    crow = lax.broadcasted_iota(jnp.int32, (c, GLA_K_WIDTH), 0)
    first = crow < half
    lane = lax.broadcasted_iota(jnp.int32, (c, GLA_K_WIDTH), 1)
    head_lanes = [_div_pow2(lane, GLA_KEY_DIM) == h for h in range(GLA_HEADS)]
    sr = lax.broadcasted_iota(jnp.int32, (GLA_HEADS * c, c), 0) & (c - 1)
    sc = lax.broadcasted_iota(jnp.int32, (GLA_HEADS * c, c), 1)
    near = (_div_pow2(sr, half) == _div_pow2(sc, half)) & (sc <= sr)
    srow = lax.broadcasted_iota(jnp.int32, state_ref.shape[1:], 0)
    slane = lax.broadcasted_iota(jnp.int32, state_ref.shape[1:], 1)
    own_head = _div_pow2(srow, GLA_VAL_DIM) == _div_pow2(slane, GLA_KEY_DIM)

    def stack_heads(x):
        return jnp.concatenate([jnp.where(m, x, 0.0) for m in head_lanes], axis=0).astype(BF16)

    for ci, bb in [(ci, bb) for ci in range(t // c) for bb in range(nb)]:
        r0 = ci * c
        cum = cum_all[bb][r0:r0 + c]
        q = q_ref[bb, r0:r0 + c, :].astype(F32)
        k = k_ref[bb, r0:r0 + c, :].astype(F32)
        v = v_ref[bb, r0:r0 + c, :]
        last = cum[c - 1:c]
        ref_far = cum[half - 1:half]
        ref_near = jnp.where(first, cum[half // 2:half // 2 + 1], cum[half + half // 2:half + half // 2 + 1])
        q_far = jnp.where(first, 0.0, q * jnp.exp(cum - ref_far))
        k_far = jnp.where(first, k * jnp.exp(ref_far - cum), 0.0)
        q_near = q * jnp.exp(cum - ref_near)
        k_near = k * jnp.exp(ref_near - cum)
        scores = (_dot_nt(stack_heads(q_far), k_far.astype(BF16))
                  + jnp.where(near, _dot_nt(stack_heads(q_near), k_near.astype(BF16)), 0.0))
        state = state_ref[bb]
        o = _dot_nt((q * jnp.exp(cum)).astype(BF16), state.astype(BF16))
        p = scores.astype(BF16)
        for h in range(GLA_HEADS):
            lo, hi = h * GLA_VAL_DIM, (h + 1) * GLA_VAL_DIM
            o_h = o[:, lo:hi] + _dot(p[h * c:(h + 1) * c], v[:, lo:hi])
            y = o_h * lax.rsqrt(jnp.mean(o_h * o_h, axis=-1, keepdims=True) + RMS_EPS) * gn_ref[:, lo:hi]
            o_ref[bb, r0:r0 + c, lo:hi] = y.astype(o_ref.dtype)
        k_tail = (k * jnp.exp(last - cum)).astype(BF16)
        update = _dot_tn(v, k_tail)
        state_ref[bb] = state * jnp.exp(last) + jnp.where(own_head, update, 0.0)


def _gla(gq, gk, gv, small, w_gate_pad, b_gate, g_norm):
    b, s, _ = gq.shape
    t = min(GLA_TOKENS, s)
    nb = GLA_BATCH if b % GLA_BATCH == 0 else 1
    tok = lambda width: pl.BlockSpec((nb, t, width), lambda bi, i: (bi, i, 0))
    block_bytes = nb * t * (2 * GLA_K_WIDTH * 2 + 2 * GLA_V_WIDTH * 2 + SMALL_WIDTH * 4)
    state_bytes = nb * GLA_V_WIDTH * GLA_K_WIDTH * 4
    return pl.pallas_call(
        _gla_kernel,
        grid=(b // nb, s // t),
        in_specs=[tok(GLA_K_WIDTH), tok(GLA_K_WIDTH), tok(GLA_V_WIDTH), tok(SMALL_WIDTH),
                  _resident(w_gate_pad.shape), _resident((1, GLA_K_WIDTH)), _resident((1, GLA_V_WIDTH))],
        out_specs=tok(GLA_V_WIDTH),
        out_shape=jax.ShapeDtypeStruct((b, s, GLA_V_WIDTH), BF16),
        scratch_shapes=[pltpu.VMEM((nb, GLA_V_WIDTH, GLA_K_WIDTH), F32)],
        compiler_params=pltpu.CompilerParams(
            dimension_semantics=("parallel", "arbitrary"),
            vmem_limit_bytes=_vmem_limit(block_bytes, state_bytes, 8 * state_bytes + 16 * t * GLA_K_WIDTH * 4)),
        name="gla",
    )(gq, gk, gv, small, w_gate_pad, b_gate, g_norm)


_B_POOL = 0
_B_Z = _B_POOL + POOL_WIDTH
_B_MERGE = _B_Z + FOX_WIDTH + GLA_V_WIDTH + POOL_WIDTH
_B_WIDTH = _B_MERGE + N_BRANCHES * D_MODEL


def _merge_kernel(x_ref, mod_ref, gpre_ref, gpost_ref, ofox_ref, ogla_ref, wb_ref, wpool_ref, pscale_ref,
                  wbr_ref, wout_ref, o_ref, u_ref, lvl_a_ref, lvl_b_ref):
    t = x_ref.shape[1]
    i = pl.program_id(1)
    x = x_ref[0]
    mod = mod_ref[0]
    h = _modulated_norm(x, gpre_ref[...], mod)

    @pl.when(i == 0)
    def _():
        u_ref[0:POOL_HALO, :] = jnp.zeros((POOL_HALO, POOL_WIDTH), F32)

    @pl.when(i > 0)
    def _():
        u_ref[0:POOL_HALO, :] = u_ref[t:t + POOL_HALO, :]

    u = _dot(h, wb_ref[:, _B_POOL:_B_Z])
    u_ref[POOL_HALO:POOL_HALO + t, :] = u
    count = (i * t + 1 + lax.broadcasted_iota(jnp.int32, (t, 1), 0)).astype(F32)
    levels = (u_ref, lvl_a_ref, lvl_b_ref, lvl_a_ref, lvl_b_ref)
    for n in range(1, len(POOL_WINDOWS) + 1):
        src, dst, back = levels[n - 1], levels[n], 2 ** (n - 1)
        first, rows = 8 * n, POOL_HALO + t - 8 * n
        lanes = slice((n - 1) * POOL_GROUP_DIM, POOL_WIDTH)
        dst[pl.ds(first, rows), lanes] = src[pl.ds(first, rows), lanes] + src[pl.ds(first - back, rows), lanes]
    pooled = []
    for g, w in enumerate(POOL_WINDOWS):
        lo, hi = g * POOL_GROUP_DIM, (g + 1) * POOL_GROUP_DIM
        window = levels[g + 1][pl.ds(POOL_HALO, t), lo:hi]
        mean = window / jnp.minimum(count, float(w))
        diff = (mean - u[:, lo:hi]).astype(BF16)
        pooled.append(_dot(diff, wpool_ref[g]) * pscale_ref[:, lo:hi])
    o_pool = jnp.concatenate(pooled, axis=-1)

    branches = (ofox_ref[0].astype(F32), ogla_ref[0].astype(F32), o_pool)
    merged = jnp.zeros((t, D_MODEL), F32)
    for br, o_br in enumerate(branches):
        z = _dot(h, wb_ref[:, _B_Z + br * FOX_WIDTH:_B_Z + (br + 1) * FOX_WIDTH])
        y = _dot((o_br * _silu(z)).astype(BF16), wbr_ref[br])
        m = _dot(h, wb_ref[:, _B_MERGE + br * D_MODEL:_B_MERGE + (br + 1) * D_MODEL])
        merged = merged + _sigmoid(m) * y
    out = _dot(merged.astype(BF16), wout_ref[...])
    out = out * lax.rsqrt(jnp.mean(out * out, axis=-1, keepdims=True) + RMS_EPS) * gpost_ref[...]
    gate = mod[:, 2 * D_MODEL:3 * D_MODEL]
    o_ref[0] = x + gate * out


def _merge(x, mod, g_pre, g_post, o_fox, o_gla, w_b, w_pool, pool_scale, w_br, w_out):
    b, s, d = x.shape
    t = min(MERGE_TOKENS, s)
    tok = lambda width: pl.BlockSpec((1, t, width), lambda bi, i: (bi, i, 0))
    block_bytes = 2 * t * d * 4 + 2 * t * FOX_WIDTH * 2
    weight_bytes = (w_b.size + w_pool.size + w_br.size + w_out.size) * 2
    return pl.pallas_call(
        _merge_kernel,
        grid=(b, s // t),
        in_specs=[tok(d),
                  pl.BlockSpec((1, 1, 3 * d), lambda bi, i: (bi, 0, 0)),
                  _resident((1, d)), _resident((1, d)),
                  tok(FOX_WIDTH), tok(GLA_V_WIDTH),
                  _resident(w_b.shape), _resident(w_pool.shape), _resident((1, POOL_WIDTH)),
                  _resident(w_br.shape), _resident(w_out.shape)],
        out_specs=tok(d),
        out_shape=jax.ShapeDtypeStruct((b, s, d), F32),
        scratch_shapes=[pltpu.VMEM((POOL_HALO + t, POOL_WIDTH), F32)] * 3,
        compiler_params=pltpu.CompilerParams(
            dimension_semantics=("parallel", "arbitrary"),
            vmem_limit_bytes=_vmem_limit(block_bytes, weight_bytes, 12 * t * d * 4)),
        name="merge",
    )(x, mod, g_pre, g_post, o_fox, o_gla, w_b, w_pool, pool_scale, w_br, w_out)


def _layer(x, mod, g_pre, g_post, w_in, b_forget, w_gla_gate, b_gla_gate, g_gla_norm, w_pool, pool_scale,
           w_br_fox, w_br_gla, w_br_pool, w_out):
    b, s, d = x.shape
    seg = lambda n: w_in[:, _OFF[n]:_OFF[n + 1]]
    w_a = jnp.concatenate([seg(1), seg(4) * GLA_KEY_DIM ** -0.5, seg(5), seg(6)], axis=1).astype(BF16)
    w_qt = jnp.concatenate([seg(0) * (LOG2_E * FOX_HEAD_DIM ** -0.5), seg(3),
                            jnp.zeros((d, KN_ROWS - FOX_HEADS), w_in.dtype)], axis=1).T.astype(BF16)
    w_vt = jnp.pad(seg(2).T.reshape(FOX_HEADS, FOX_HEAD_DIM, d), ((0, 0), (0, V_ROWS - FOX_HEAD_DIM), (0, 0)))
    w_vt = w_vt.reshape(FOX_HEADS * V_ROWS, d).astype(BF16)
    v_ones = jnp.tile(jnp.arange(V_ROWS) >= FOX_HEAD_DIM, FOX_HEADS).astype(F32).reshape(FOX_HEADS * V_ROWS, 1)
    pad = jnp.zeros((d, SMALL_WIDTH - FOX_HEADS - GLA_GATE_RANK), w_in.dtype)
    w_small = jnp.concatenate([seg(3), seg(7), pad], axis=1).astype(BF16)
    w_b = w_in[:, _OFF[8]:_OFF[13]].astype(BF16)
    b_forget_col = jnp.zeros((KN_ROWS, 1), F32).at[:FOX_HEADS, 0].set(b_forget)
    w_gate_pad = jnp.zeros((SMALL_WIDTH, GLA_K_WIDTH), F32).at[FOX_HEADS:FOX_HEADS + GLA_GATE_RANK].set(
        w_gla_gate).astype(BF16)
    w_br = jnp.stack([w_br_fox, w_br_gla, w_br_pool]).astype(BF16)
    mod3 = mod.reshape(b, 1, 3 * d)
    g_pre = g_pre.reshape(1, d)

    head_sel = (jnp.arange(KN_ROWS)[:, None] == jnp.arange(FOX_WIDTH)[None, :] // FOX_HEAD_DIM).astype(BF16)
    fk, gq, gk, gv, fqt, fvt, small, kn2, fft = _in_proj(x, mod3, g_pre, w_a, w_qt, w_vt, v_ones, w_small, head_sel)
    kx, qx = _forget_cumsum(fft, b_forget_col)
    t_fox = min(FOX_TILE, s)
    assert s // t_fox <= LANES, "the tile-skip bound keeps one key tile per lane"
    flast = qx[:, :, t_fox - 1::t_fox]
    flast = jnp.pad(flast, ((0, 0), (0, 0), (0, -flast.shape[2] % LANES)))
    o_fox = _fox_attention(fqt, qx, fk, kx, fvt, kn2, flast)
    o_gla = _gla(gq, gk, gv, small, w_gate_pad, b_gla_gate.reshape(1, GLA_K_WIDTH),
                 g_gla_norm.reshape(1, GLA_V_WIDTH))
    return _merge(x, mod3, g_pre, g_post.reshape(1, d), o_fox, o_gla, w_b, w_pool.astype(BF16),
                  pool_scale.reshape(1, POOL_WIDTH), w_br, w_out.astype(BF16))


def kernel(x, c, w_ada, b_ada, g_pre, g_post, w_in, b_forget, w_gla_gate, b_gla_gate, g_gla_norm, w_pool,
           pool_scale, w_br_fox, w_br_gla, w_br_pool, w_out):
    mods = _adaln_mod(c, w_ada, b_ada)
    h = x
    for i in range(w_in.shape[0]):
        h = _layer(h, mods[i], g_pre[i], g_post[i], w_in[i], b_forget[i], w_gla_gate[i], b_gla_gate[i],
                   g_gla_norm[i], w_pool[i], pool_scale[i], w_br_fox[i], w_br_gla[i], w_br_pool[i], w_out[i])
    return h
```

```python
import functools

import numpy as np
import jax
import jax.numpy as jnp
from jax import lax
from jax.experimental import pallas as pl
from jax.experimental.pallas import tpu as pltpu

F32 = jnp.float32
BF16 = jnp.bfloat16

D_MODEL = 1024
FOX_HEADS = 8
FOX_HEAD_DIM = 64
FOX_WIDTH = FOX_HEADS * FOX_HEAD_DIM
GLA_HEADS = 4
GLA_KEY_DIM = 64
GLA_VAL_DIM = 128
GLA_K_WIDTH = GLA_HEADS * GLA_KEY_DIM
GLA_V_WIDTH = GLA_HEADS * GLA_VAL_DIM
GLA_GATE_RANK = 16
GLA_TAU = 16.0
POOL_WINDOWS = (2, 4, 8, 16)
POOL_GROUP_DIM = 128
POOL_WIDTH = len(POOL_WINDOWS) * POOL_GROUP_DIM
N_BRANCHES = 3
RMS_EPS = 1e-6
IN_SPLITS = (FOX_WIDTH, FOX_WIDTH, FOX_WIDTH, FOX_HEADS,
             GLA_K_WIDTH, GLA_K_WIDTH, GLA_V_WIDTH, GLA_GATE_RANK,
             POOL_WIDTH, FOX_WIDTH, GLA_V_WIDTH, POOL_WIDTH, N_BRANCHES * D_MODEL)
_OFF = np.concatenate([[0], np.cumsum(IN_SPLITS)]).tolist()

LANES = 128
MXU_WIDTH = 256
SMALL_WIDTH = LANES
V_ROWS = FOX_HEAD_DIM + 16
POOL_HALO = 8 * len(POOL_WINDOWS)
MASK_VALUE = -1e30
LOG2_E = 1.4426950408889634
SKIP_LOG2 = 140.0
NORM_SLACK = 1.02
V7X_VMEM_BYTES = 64 * 1024 * 1024

IN_TOKENS = 512
FORGET_TOKENS = 2048
FOX_TILE = 256
FOX_PAIRS = 4
GLA_TOKENS = 256
GLA_CHUNK = 64
GLA_BATCH = 8
MERGE_TOKENS = 512


def _vmem_limit(block_bytes, resident_bytes, temp_bytes):
    need = 2 * block_bytes + resident_bytes + temp_bytes
    return int(min(max(need, 16 * 1024 * 1024), V7X_VMEM_BYTES - 8 * 1024 * 1024))


def _resident(shape):
    zeros = (0,) * len(shape)
    return pl.BlockSpec(shape, lambda *_: zeros, pipeline_mode=pl.Buffered(1))


def _div_pow2(i, n):
    shift = n.bit_length() - 1
    assert 1 << shift == n
    return jnp.right_shift(i, shift)


def _log_sigmoid(x):
    return jnp.minimum(x, 0.0) - jnp.log(1.0 + jnp.exp(-jnp.abs(x)))


def _sigmoid(x):
    return jax.nn.sigmoid(x)


def _silu(x):
    return x * _sigmoid(x)


def _dot(a, b):
    return jnp.dot(a, b, preferred_element_type=F32)


def _dot_nt(a, b):
    return lax.dot_general(a, b, (((1,), (1,)), ((), ())), preferred_element_type=F32)


def _dot_tn(a, b):
    return lax.dot_general(a, b, (((0,), (0,)), ((), ())), preferred_element_type=F32)


def _split3(x):
    hi = x.astype(BF16)
    r1 = x - hi.astype(F32)
    mid = r1.astype(BF16)
    lo = (r1 - mid.astype(F32)).astype(BF16)
    return hi, mid, lo


def _ones_dot_f32(ones_bf16, x):
    hi, mid, lo = _split3(x)
    return _dot(ones_bf16, hi) + _dot(ones_bf16, mid) + _dot(ones_bf16, lo)


def _modulated_norm(x, g, mod):
    y = x * lax.rsqrt(jnp.mean(x * x, axis=-1, keepdims=True) + RMS_EPS) * g
    shift = mod[:, 0:D_MODEL]
    scale = mod[:, D_MODEL:2 * D_MODEL]
    return (y * (1.0 + scale) + shift).astype(BF16)


def _mod_kernel(c_ref, w_ref, b_ref, o_ref):
    c = c_ref[...]
    o_ref[0] = jnp.dot(_silu(c), w_ref[0], preferred_element_type=F32,
                       precision=lax.Precision.HIGHEST) + b_ref[0]


def _adaln_mod(c, w_ada, b_ada):
    depth, d, d3 = w_ada.shape
    b = c.shape[0]
    n_col = d3 // d
    return pl.pallas_call(
        _mod_kernel,
        grid=(depth, n_col),
        in_specs=[pl.BlockSpec((b, d), lambda l, j: (0, 0)),
                  pl.BlockSpec((1, d, d), lambda l, j: (l, 0, j)),
                  pl.BlockSpec((1, 1, d), lambda l, j: (l, 0, j))],
        out_specs=pl.BlockSpec((1, b, d), lambda l, j: (l, 0, j)),
        out_shape=jax.ShapeDtypeStruct((depth, b, d3), F32),
        compiler_params=pltpu.CompilerParams(
            dimension_semantics=("parallel", "parallel"),
            vmem_limit_bytes=_vmem_limit(d * d * 4 + 2 * b * d * 4, 0, 4 * b * d * 4)),
        name="adaln_mod",
    )(c, w_ada, b_ada.reshape(depth, 1, d3))


_A_FOX_K = 0
_A_GLA_Q = _A_FOX_K + FOX_WIDTH
_A_GLA_K = _A_GLA_Q + GLA_K_WIDTH
_A_GLA_V = _A_GLA_K + GLA_K_WIDTH
_A_WIDTH = _A_GLA_V + GLA_V_WIDTH
KN_ROWS = 16


def _in_proj_kernel(x_ref, mod_ref, g_ref, wa_ref, wqt_ref, wvt_ref, vones_ref, ws_ref, hsel_ref,
                    fk_ref, gq_ref, gk_ref, gv_ref, fqt_ref, fvt_ref, small_ref, kn2_ref, fft_ref):
    h = _modulated_norm(x_ref[0], g_ref[...], mod_ref[0])
    fk = _dot(h, wa_ref[:, _A_FOX_K:_A_GLA_Q]).astype(BF16)
    fk_ref[0] = fk
    k2 = fk.astype(F32)
    norms = _dot_nt(hsel_ref[...], (k2 * k2).astype(BF16))
    tile_max = jnp.broadcast_to(jnp.max(norms, axis=1, keepdims=True), kn2_ref.shape[1:])

    @pl.when(pl.program_id(1) == 0)
    def _():
        kn2_ref[0] = tile_max

    @pl.when(pl.program_id(1) > 0)
    def _():
        kn2_ref[0] = jnp.maximum(kn2_ref[0], tile_max)

    gq_ref[0] = _dot(h, wa_ref[:, _A_GLA_Q:_A_GLA_K]).astype(BF16)
    gk_ref[0] = _dot(h, wa_ref[:, _A_GLA_K:_A_GLA_V]).astype(BF16)
    gv_ref[0] = _dot(h, wa_ref[:, _A_GLA_V:_A_WIDTH]).astype(BF16)
    qt = _dot_nt(wqt_ref[...], h)
    fqt_ref[0] = qt[0:FOX_WIDTH].astype(BF16)
    fft_ref[0] = qt[FOX_WIDTH:FOX_WIDTH + KN_ROWS]
    fvt_ref[0] = (_dot_nt(wvt_ref[...], h) + vones_ref[...]).astype(BF16)
    small_ref[0] = _dot(h, ws_ref[...])


def _in_proj(x, mod, g_pre, w_a, w_qt, w_vt, v_ones, w_small, head_sel):
    b, s, d = x.shape
    t = min(IN_TOKENS, s)
    tok = lambda width: pl.BlockSpec((1, t, width), lambda bi, i: (bi, i, 0))
    tok_t = lambda rows: pl.BlockSpec((1, rows, t), lambda bi, i: (bi, 0, i))
    v_rows = FOX_HEADS * V_ROWS
    out_shapes = (
        jax.ShapeDtypeStruct((b, s, FOX_WIDTH), BF16),
        jax.ShapeDtypeStruct((b, s, GLA_K_WIDTH), BF16),
        jax.ShapeDtypeStruct((b, s, GLA_K_WIDTH), BF16),
        jax.ShapeDtypeStruct((b, s, GLA_V_WIDTH), BF16),
        jax.ShapeDtypeStruct((b, FOX_WIDTH, s), BF16),
        jax.ShapeDtypeStruct((b, v_rows, s), BF16),
        jax.ShapeDtypeStruct((b, s, SMALL_WIDTH), F32),
        jax.ShapeDtypeStruct((b, KN_ROWS, LANES), F32),
        jax.ShapeDtypeStruct((b, KN_ROWS, s), F32),
    )
    out_specs = (tok(FOX_WIDTH), tok(GLA_K_WIDTH), tok(GLA_K_WIDTH), tok(GLA_V_WIDTH), tok_t(FOX_WIDTH), tok_t(v_rows),
                 tok(SMALL_WIDTH), pl.BlockSpec((1, KN_ROWS, LANES), lambda bi, i: (bi, 0, 0)), tok_t(KN_ROWS))
    n_out = _A_WIDTH + FOX_WIDTH + v_rows
    block_bytes = t * d * 4 + t * n_out * 2 + t * SMALL_WIDTH * 4
    weight_bytes = d * (n_out + SMALL_WIDTH) * 2
    return pl.pallas_call(
        _in_proj_kernel,
        grid=(b, s // t),
        in_specs=[tok(d),
                  pl.BlockSpec((1, 1, 3 * d), lambda bi, i: (bi, 0, 0)),
                  _resident((1, d)), _resident(w_a.shape), _resident(w_qt.shape), _resident(w_vt.shape),
                  _resident(v_ones.shape), _resident(w_small.shape), _resident(head_sel.shape)],
        out_specs=out_specs,
        out_shape=out_shapes,
        compiler_params=pltpu.CompilerParams(
            dimension_semantics=("parallel", "arbitrary"),
            vmem_limit_bytes=_vmem_limit(block_bytes, weight_bytes, 6 * t * d * 4)),
        name="in_proj",
    )(x, mod, g_pre, w_a, w_qt, w_vt, v_ones, w_small, head_sel)


BIAS_SLOTS = 16


def _bias_placement():
    pk = np.zeros((3 * KN_ROWS, FOX_WIDTH), np.float32)
    k_const = np.zeros((1, FOX_WIDTH), np.float32)
    pq = np.zeros((FOX_HEADS * BIAS_SLOTS, 3 * KN_ROWS), np.float32)
    q_const = np.zeros((FOX_HEADS * BIAS_SLOTS, 1), np.float32)
    for head in range(FOX_HEADS):
        pair, odd = divmod(head, 2)
        for part in range(3):
            pk[part * KN_ROWS + head, pair * LANES + 3 + 3 * odd + part] = -1.0
            pq[head * BIAS_SLOTS + part, part * KN_ROWS + head] = 1.0
            q_const[head * BIAS_SLOTS + 3 + 3 * odd + part, 0] = 1.0
            k_const[0, pair * LANES + part] = 1.0
    return pk, k_const, pq, q_const


def _forget_kernel(fft_ref, b_ref, pk_ref, kc_ref, pq_ref, qc_ref, kx_ref, qx_ref, carry_ref):
    @pl.when(pl.program_id(1) == 0)
    def _():
        carry_ref[...] = jnp.zeros_like(carry_ref)

    t = fft_ref.shape[2]
    w = min(MXU_WIDTH, t)
    log_f = _log_sigmoid(fft_ref[0] + b_ref[...])
    row = lax.broadcasted_iota(jnp.int32, (w, w), 0)
    col = lax.broadcasted_iota(jnp.int32, (w, w), 1)
    upper = jnp.where(row <= col, 1.0, 0.0).astype(BF16)
    split = jnp.concatenate(_split3(log_f), axis=0)
    carry = carry_ref[:, 0:1]
    blocks = []
    for j in range(t // w):
        sums = _dot(split[:, j * w:(j + 1) * w], upper)
        blocks.append(sums[0:KN_ROWS] + sums[KN_ROWS:2 * KN_ROWS] + sums[2 * KN_ROWS:3 * KN_ROWS] + carry)
        carry = blocks[-1][:, w - 1:w]
    carry_ref[...] = jnp.broadcast_to(carry, carry_ref.shape)
    cs = jnp.concatenate(blocks, axis=1)
    parts = jnp.concatenate(_split3(cs * LOG2_E), axis=0)
    qx_ref[0] = (_dot(pq_ref[...], parts) + qc_ref[...]).astype(BF16)
    kx_ref[0] = (_dot_tn(parts, pk_ref[...]) + kc_ref[...]).astype(BF16)


def _forget_cumsum(fft, b_forget_col):
    b, _, s = fft.shape
    t = min(FORGET_TOKENS, s)
    pk, k_const, pq, q_const = _bias_placement()
    q_rows = FOX_HEADS * BIAS_SLOTS
    return pl.pallas_call(
        _forget_kernel,
        grid=(b, s // t),
        in_specs=[pl.BlockSpec((1, KN_ROWS, t), lambda bi, i: (bi, 0, i)),
                  _resident((KN_ROWS, 1)), _resident(pk.shape), _resident(k_const.shape),
                  _resident(pq.shape), _resident(q_const.shape)],
        out_specs=(pl.BlockSpec((1, t, FOX_WIDTH), lambda bi, i: (bi, i, 0)),
                   pl.BlockSpec((1, q_rows, t), lambda bi, i: (bi, 0, i))),
        out_shape=(jax.ShapeDtypeStruct((b, s, FOX_WIDTH), BF16),
                   jax.ShapeDtypeStruct((b, q_rows, s), BF16)),
        scratch_shapes=[pltpu.VMEM((KN_ROWS, LANES), F32)],
        compiler_params=pltpu.CompilerParams(
            dimension_semantics=("parallel", "arbitrary"),
            vmem_limit_bytes=_vmem_limit(t * (KN_ROWS * 4 + FOX_WIDTH * 2 + q_rows * 2), pk.size * 2 + pq.size * 2,
                                         8 * t * FOX_WIDTH * 4)),
        name="forget_cumsum",
    )(fft, b_forget_col, jnp.asarray(pk, BF16), jnp.asarray(k_const), jnp.asarray(pq, BF16), jnp.asarray(q_const))


def _fox_kernel(qt_ref, qx_ref, k_ref, kx_ref, vt_ref, kn2_ref, flast_ref, o_ref, s_ref, p_ref, acc_ref):
    t = qt_ref.shape[2]
    i = pl.program_id(2)
    n_heads = 2 * FOX_PAIRS
    row = lax.broadcasted_iota(jnp.int32, (LANES, t), 0)
    low = row < FOX_HEAD_DIM
    pad = jnp.zeros((LANES - BIAS_SLOTS, t), BF16)
    q_aug = []
    for hd in range(n_heads):
        qt = qt_ref[0, (hd // 2) * LANES:(hd // 2 + 1) * LANES, :]
        own = jnp.where(low, qt, jnp.zeros_like(qt)) if hd % 2 == 0 else jnp.where(low, jnp.zeros_like(qt), qt)
        q_aug.append(jnp.concatenate([own, qx_ref[0, hd * BIAS_SLOTS:(hd + 1) * BIAS_SLOTS, :], pad], axis=0))

    def logits(j, masked):
        k0 = pl.multiple_of(j * t, t)
        out = []
        for pr in range(FOX_PAIRS):
            lanes = slice(pr * LANES, (pr + 1) * LANES)
            k_aug = jnp.concatenate([k_ref[0, pl.ds(k0, t), lanes], kx_ref[0, pl.ds(k0, t), lanes]], axis=1)
            for h in range(2):
                s = _dot(k_aug, q_aug[2 * pr + h])
                if masked:
                    r = lax.broadcasted_iota(jnp.int32, (t, t), 0)
                    c = lax.broadcasted_iota(jnp.int32, (t, t), 1)
                    s = jnp.where(r <= c, s, MASK_VALUE)
                out.append(s)
        return out

    def weighted_values(j, hd, p):
        k0 = pl.multiple_of(j * t, t)
        return _dot(vt_ref[0, hd * V_ROWS:(hd + 1) * V_ROWS, pl.ds(k0, t)], p)

    def stage_logits(j, masked, slot):
        maxes = []
        for hd, s in enumerate(logits(j, masked)):
            s_ref[slot, hd] = s
            maxes.append(jnp.max(s, axis=0, keepdims=True))
        return maxes

    def stage_softmax(hd, s_max, m, s_slot, p_slot):
        m_new = jnp.maximum(m, s_max)
        p_ref[p_slot, hd] = jnp.exp2(s_ref[s_slot, hd] - m_new).astype(BF16)
        return m_new, jnp.exp2(m - m_new)

    m0 = jnp.full((1, t), MASK_VALUE, F32)
    s_max = stage_logits(i, True, 1)
    stats = [stage_softmax(hd, s_max[hd], m0, 1, 0) for hd in range(n_heads)]
    s_max = stage_logits(jnp.maximum(i - 1, 0), False, 0)
    acc_ref[...] = jnp.zeros_like(acc_ref)

    tile_id = lax.broadcasted_iota(jnp.int32, (1, LANES), 1).astype(F32)
    first_head = pl.program_id(1) * n_heads
    first_needed = []
    for hd in range(n_heads):
        base = hd * BIAS_SLOTS
        f_k = sum(flast_ref[0, base + part:base + part + 1, :].astype(F32) for part in range(3))
        f_q = sum(qx_ref[0, base + part:base + part + 1, 0:1].astype(F32) for part in range(3))
        q = qt_ref[0, hd * FOX_HEAD_DIM:(hd + 1) * FOX_HEAD_DIM, :].astype(F32)
        qn2 = jnp.max(jnp.sum(q * q, axis=0, keepdims=True), axis=1, keepdims=True)
        qk = jnp.sqrt(qn2 * kn2_ref[0, pl.ds(first_head + hd, 1), 0:1]) * NORM_SLACK
        m_min = jnp.min(stats[hd][0], axis=1, keepdims=True)
        needed = (qk + f_q - f_k >= m_min - SKIP_LOG2) & (tile_id < i.astype(F32))
        first_needed.append(jnp.min(jnp.where(needed, tile_id, i.astype(F32)), axis=1, keepdims=True))
    first_tile = jnp.min(jnp.concatenate(first_needed, axis=1), axis=1, keepdims=True)[0, 0].astype(jnp.int32)
    n_trips = i - first_tile

    def trip(par, fetch, k, carry):
        prev, heads = carry
        for hd in range(n_heads):
            pv = weighted_values(prev, hd, p_ref[par, hd])
            acc_ref[hd] = heads[hd][1] * acc_ref[hd] + pv
        stats = [stage_softmax(hd, heads[hd][2], heads[hd][0], par, 1 - par) for hd in range(n_heads)]
        s_max = stage_logits(i - 2 - k, False, 1 - par) if fetch else [h[2] for h in heads]
        return i - 1 - k, tuple(stats[hd] + (s_max[hd],) for hd in range(n_heads))

    def either_trip(fetch, k, carry):
        return lax.cond((k & 1) == 0, functools.partial(trip, 0, fetch, k), functools.partial(trip, 1, fetch, k), carry)

    carry = (i, tuple(stats[hd] + (s_max[hd],) for hd in range(n_heads)))
    carry = lax.fori_loop(0, n_trips - 1, functools.partial(either_trip, True), carry)
    prev, heads = lax.cond(n_trips > 0, functools.partial(either_trip, False, n_trips - 1), lambda c: c, carry)
    o_heads = []
    for hd in range(n_heads):
        acc = heads[hd][1] * acc_ref[hd] + weighted_values(prev, hd, p_ref[n_trips & 1, hd])
        o_heads.append(acc[0:FOX_HEAD_DIM] * (1.0 / acc[FOX_HEAD_DIM:FOX_HEAD_DIM + 1]))
    o_ref[0] = jnp.concatenate(o_heads, axis=0).T.astype(o_ref.dtype)


def _fox_attention(fqt, qx, fk, kx, fvt, kn2, flast):
    b, s, _ = fk.shape
    t = min(FOX_TILE, s)
    width = FOX_PAIRS * LANES
    n_heads = 2 * FOX_PAIRS
    groups = FOX_WIDTH // width
    whole = lambda shape, index_map: pl.BlockSpec(shape, index_map, pipeline_mode=pl.Buffered(1))
    block_bytes = (width + n_heads * BIAS_SLOTS) * t * 2 + t * width * 2
    resident_bytes = (2 * width + n_heads * V_ROWS) * s * 2 + n_heads * t * (2 * t * 6 + V_ROWS * 4)
    return pl.pallas_call(
        _fox_kernel,
        grid=(b, groups, s // t),
        in_specs=[pl.BlockSpec((1, width, t), lambda bi, g, i: (bi, g, i)),
                  pl.BlockSpec((1, n_heads * BIAS_SLOTS, t), lambda bi, g, i: (bi, g, i)),
                  whole((1, s, width), lambda bi, g, i: (bi, 0, g)),
                  whole((1, s, width), lambda bi, g, i: (bi, 0, g)),
                  whole((1, n_heads * V_ROWS, s), lambda bi, g, i: (bi, g, 0)),
                  whole((1, KN_ROWS, LANES), lambda bi, g, i: (bi, 0, 0)),
                  whole((1, n_heads * BIAS_SLOTS, flast.shape[2]), lambda bi, g, i: (bi, g, 0))],
        out_specs=pl.BlockSpec((1, t, width), lambda bi, g, i: (bi, i, g)),
        out_shape=jax.ShapeDtypeStruct((b, s, FOX_WIDTH), BF16),
        scratch_shapes=[pltpu.VMEM((2, n_heads, t, t), F32), pltpu.VMEM((2, n_heads, t, t), BF16),
                        pltpu.VMEM((n_heads, V_ROWS, t), F32)],
        compiler_params=pltpu.CompilerParams(
            dimension_semantics=("parallel", "parallel", "parallel"),
            vmem_limit_bytes=_vmem_limit(block_bytes, resident_bytes, 8 * n_heads * t * t)),
        name="fox_attention",
    )(fqt, qx, fk, kx, fvt, kn2, flast)


def _gla_kernel(q_ref, k_ref, v_ref, small_ref, wg_ref, bg_ref, gn_ref, o_ref, state_ref):
    @pl.when(pl.program_id(1) == 0)
    def _():
        state_ref[...] = jnp.zeros_like(state_ref)

    nb, t, _ = q_ref.shape
    c = GLA_CHUNK
    half = c // 2
    row = lax.broadcasted_iota(jnp.int32, (t, t), 0)
    col = lax.broadcasted_iota(jnp.int32, (t, t), 1)
    same_chunk = _div_pow2(row, c) == _div_pow2(col, c)
    tri = jnp.where(same_chunk & (col <= row), 1.0, 0.0).astype(BF16)
    cum_all = []
    for bb in range(nb):
        gate_logits = _dot(small_ref[bb].astype(BF16), wg_ref[...]) + bg_ref[...]
        cum_all.append(_ones_dot_f32(tri, _log_sigmoid(gate_logits) * (1.0 / GLA_TAU)))

    crow = lax.broadcasted_iota(jnp.int32, (c, GLA_K_WIDTH), 0)
    first = crow < half
    lane = lax.broadcasted_iota(jnp.int32, (c, GLA_K_WIDTH), 1)
    head_lanes = [_div_pow2(lane, GLA_KEY_DIM) == h for h in range(GLA_HEADS)]
    sr = lax.broadcasted_iota(jnp.int32, (GLA_HEADS * c, c), 0) & (c - 1)
    sc = lax.broadcasted_iota(jnp.int32, (GLA_HEADS * c, c), 1)
    near = (_div_pow2(sr, half) == _div_pow2(sc, half)) & (sc <= sr)
    srow = lax.broadcasted_iota(jnp.int32, state_ref.shape[1:], 0)
    slane = lax.broadcasted_iota(jnp.int32, state_ref.shape[1:], 1)
    own_head = _div_pow2(srow, GLA_VAL_DIM) == _div_pow2(slane, GLA_KEY_DIM)

    def stack_heads(x):
        return jnp.concatenate([jnp.where(m, x, 0.0) for m in head_lanes], axis=0).astype(BF16)

    for ci, bb in [(ci, bb) for ci in range(t // c) for bb in range(nb)]:
        r0 = ci * c
        cum = cum_all[bb][r0:r0 + c]
        q = q_ref[bb, r0:r0 + c, :].astype(F32)
        k = k_ref[bb, r0:r0 + c, :].astype(F32)
        v = v_ref[bb, r0:r0 + c, :]
        last = cum[c - 1:c]
        ref_far = cum[half - 1:half]
        ref_near = jnp.where(first, cum[half // 2:half // 2 + 1], cum[half + half // 2:half + half // 2 + 1])
        q_far = jnp.where(first, 0.0, q * jnp.exp(cum - ref_far))
        k_far = jnp.where(first, k * jnp.exp(ref_far - cum), 0.0)
        q_near = q * jnp.exp(cum - ref_near)
        k_near = k * jnp.exp(ref_near - cum)
        scores = (_dot_nt(stack_heads(q_far), k_far.astype(BF16))
                  + jnp.where(near, _dot_nt(stack_heads(q_near), k_near.astype(BF16)), 0.0))
        state = state_ref[bb]
        o = _dot_nt((q * jnp.exp(cum)).astype(BF16), state.astype(BF16))
        p = scores.astype(BF16)
        for h in range(GLA_HEADS):
            lo, hi = h * GLA_VAL_DIM, (h + 1) * GLA_VAL_DIM
            o_h = o[:, lo:hi] + _dot(p[h * c:(h + 1) * c], v[:, lo:hi])
            y = o_h * lax.rsqrt(jnp.mean(o_h * o_h, axis=-1, keepdims=True) + RMS_EPS) * gn_ref[:, lo:hi]
            o_ref[bb, r0:r0 + c, lo:hi] = y.astype(o_ref.dtype)
        k_tail = (k * jnp.exp(last - cum)).astype(BF16)
        update = _dot_tn(v, k_tail)
        state_ref[bb] = state * jnp.exp(last) + jnp.where(own_head, update, 0.0)


def _gla(gq, gk, gv, small, w_gate_pad, b_gate, g_norm):
    b, s, _ = gq.shape
    t = min(GLA_TOKENS, s)
    nb = GLA_BATCH if b % GLA_BATCH == 0 else 1
    tok = lambda width: pl.BlockSpec((nb, t, width), lambda bi, i: (bi, i, 0))
    block_bytes = nb * t * (2 * GLA_K_WIDTH * 2 + 2 * GLA_V_WIDTH * 2 + SMALL_WIDTH * 4)
    state_bytes = nb * GLA_V_WIDTH * GLA_K_WIDTH * 4
    return pl.pallas_call(
        _gla_kernel,
        grid=(b // nb, s // t),
        in_specs=[tok(GLA_K_WIDTH), tok(GLA_K_WIDTH), tok(GLA_V_WIDTH), tok(SMALL_WIDTH),
                  _resident(w_gate_pad.shape), _resident((1, GLA_K_WIDTH)), _resident((1, GLA_V_WIDTH))],
        out_specs=tok(GLA_V_WIDTH),
        out_shape=jax.ShapeDtypeStruct((b, s, GLA_V_WIDTH), BF16),
        scratch_shapes=[pltpu.VMEM((nb, GLA_V_WIDTH, GLA_K_WIDTH), F32)],
        compiler_params=pltpu.CompilerParams(
            dimension_semantics=("parallel", "arbitrary"),
            vmem_limit_bytes=_vmem_limit(block_bytes, state_bytes, 8 * state_bytes + 16 * t * GLA_K_WIDTH * 4)),
        name="gla",
    )(gq, gk, gv, small, w_gate_pad, b_gate, g_norm)


_B_POOL = 0
_B_Z = _B_POOL + POOL_WIDTH
_B_MERGE = _B_Z + FOX_WIDTH + GLA_V_WIDTH + POOL_WIDTH
_B_WIDTH = _B_MERGE + N_BRANCHES * D_MODEL


def _merge_kernel(x_ref, mod_ref, gpre_ref, gpost_ref, ofox_ref, ogla_ref, wb_ref, wpool_ref, pscale_ref,
                  wbr_ref, wout_ref, o_ref, u_ref, lvl_a_ref, lvl_b_ref):
    t = x_ref.shape[1]
    i = pl.program_id(1)
    x = x_ref[0]
    mod = mod_ref[0]
    h = _modulated_norm(x, gpre_ref[...], mod)

    @pl.when(i == 0)
    def _():
        u_ref[0:POOL_HALO, :] = jnp.zeros((POOL_HALO, POOL_WIDTH), F32)

    @pl.when(i > 0)
    def _():
        u_ref[0:POOL_HALO, :] = u_ref[t:t + POOL_HALO, :]

    u = _dot(h, wb_ref[:, _B_POOL:_B_Z])
    u_ref[POOL_HALO:POOL_HALO + t, :] = u
    count = (i * t + 1 + lax.broadcasted_iota(jnp.int32, (t, 1), 0)).astype(F32)
    levels = (u_ref, lvl_a_ref, lvl_b_ref, lvl_a_ref, lvl_b_ref)
    for n in range(1, len(POOL_WINDOWS) + 1):
        src, dst, back = levels[n - 1], levels[n], 2 ** (n - 1)
        first, rows = 8 * n, POOL_HALO + t - 8 * n
        lanes = slice((n - 1) * POOL_GROUP_DIM, POOL_WIDTH)
        dst[pl.ds(first, rows), lanes] = src[pl.ds(first, rows), lanes] + src[pl.ds(first - back, rows), lanes]
    pooled = []
    for g, w in enumerate(POOL_WINDOWS):
        lo, hi = g * POOL_GROUP_DIM, (g + 1) * POOL_GROUP_DIM
        window = levels[g + 1][pl.ds(POOL_HALO, t), lo:hi]
        mean = window / jnp.minimum(count, float(w))
        diff = (mean - u[:, lo:hi]).astype(BF16)
        pooled.append(_dot(diff, wpool_ref[g]) * pscale_ref[:, lo:hi])
    o_pool = jnp.concatenate(pooled, axis=-1)

    branches = (ofox_ref[0].astype(F32), ogla_ref[0].astype(F32), o_pool)
    merged = jnp.zeros((t, D_MODEL), F32)
    for br, o_br in enumerate(branches):
        z = _dot(h, wb_ref[:, _B_Z + br * FOX_WIDTH:_B_Z + (br + 1) * FOX_WIDTH])
        y = _dot((o_br * _silu(z)).astype(BF16), wbr_ref[br])
        m = _dot(h, wb_ref[:, _B_MERGE + br * D_MODEL:_B_MERGE + (br + 1) * D_MODEL])
        merged = merged + _sigmoid(m) * y
    out = _dot(merged.astype(BF16), wout_ref[...])
    out = out * lax.rsqrt(jnp.mean(out * out, axis=-1, keepdims=True) + RMS_EPS) * gpost_ref[...]
    gate = mod[:, 2 * D_MODEL:3 * D_MODEL]
    o_ref[0] = x + gate * out


def _merge(x, mod, g_pre, g_post, o_fox, o_gla, w_b, w_pool, pool_scale, w_br, w_out):
    b, s, d = x.shape
    t = min(MERGE_TOKENS, s)
    tok = lambda width: pl.BlockSpec((1, t, width), lambda bi, i: (bi, i, 0))
    block_bytes = 2 * t * d * 4 + 2 * t * FOX_WIDTH * 2
    weight_bytes = (w_b.size + w_pool.size + w_br.size + w_out.size) * 2
    return pl.pallas_call(
        _merge_kernel,
        grid=(b, s // t),
        in_specs=[tok(d),
                  pl.BlockSpec((1, 1, 3 * d), lambda bi, i: (bi, 0, 0)),
                  _resident((1, d)), _resident((1, d)),
                  tok(FOX_WIDTH), tok(GLA_V_WIDTH),
                  _resident(w_b.shape), _resident(w_pool.shape), _resident((1, POOL_WIDTH)),
                  _resident(w_br.shape), _resident(w_out.shape)],
        out_specs=tok(d),
        out_shape=jax.ShapeDtypeStruct((b, s, d), F32),
        scratch_shapes=[pltpu.VMEM((POOL_HALO + t, POOL_WIDTH), F32)] * 3,
        compiler_params=pltpu.CompilerParams(
            dimension_semantics=("parallel", "arbitrary"),
            vmem_limit_bytes=_vmem_limit(block_bytes, weight_bytes, 12 * t * d * 4)),
        name="merge",
    )(x, mod, g_pre, g_post, o_fox, o_gla, w_b, w_pool, pool_scale, w_br, w_out)


def _layer(x, mod, g_pre, g_post, w_in, b_forget, w_gla_gate, b_gla_gate, g_gla_norm, w_pool, pool_scale,
           w_br_fox, w_br_gla, w_br_pool, w_out):
    b, s, d = x.shape
    seg = lambda n: w_in[:, _OFF[n]:_OFF[n + 1]]
    w_a = jnp.concatenate([seg(1), seg(4) * GLA_KEY_DIM ** -0.5, seg(5), seg(6)], axis=1).astype(BF16)
    w_qt = jnp.concatenate([seg(0) * (LOG2_E * FOX_HEAD_DIM ** -0.5), seg(3),
                            jnp.zeros((d, KN_ROWS - FOX_HEADS), w_in.dtype)], axis=1).T.astype(BF16)
    w_vt = jnp.pad(seg(2).T.reshape(FOX_HEADS, FOX_HEAD_DIM, d), ((0, 0), (0, V_ROWS - FOX_HEAD_DIM), (0, 0)))
    w_vt = w_vt.reshape(FOX_HEADS * V_ROWS, d).astype(BF16)
    v_ones = jnp.tile(jnp.arange(V_ROWS) >= FOX_HEAD_DIM, FOX_HEADS).astype(F32).reshape(FOX_HEADS * V_ROWS, 1)
    pad = jnp.zeros((d, SMALL_WIDTH - FOX_HEADS - GLA_GATE_RANK), w_in.dtype)
    w_small = jnp.concatenate([seg(3), seg(7), pad], axis=1).astype(BF16)
    w_b = w_in[:, _OFF[8]:_OFF[13]].astype(BF16)
    b_forget_col = jnp.zeros((KN_ROWS, 1), F32).at[:FOX_HEADS, 0].set(b_forget)
    w_gate_pad = jnp.zeros((SMALL_WIDTH, GLA_K_WIDTH), F32).at[FOX_HEADS:FOX_HEADS + GLA_GATE_RANK].set(
        w_gla_gate).astype(BF16)
    w_br = jnp.stack([w_br_fox, w_br_gla, w_br_pool]).astype(BF16)
    mod3 = mod.reshape(b, 1, 3 * d)
    g_pre = g_pre.reshape(1, d)

    head_sel = (jnp.arange(KN_ROWS)[:, None] == jnp.arange(FOX_WIDTH)[None, :] // FOX_HEAD_DIM).astype(BF16)
    fk, gq, gk, gv, fqt, fvt, small, kn2, fft = _in_proj(x, mod3, g_pre, w_a, w_qt, w_vt, v_ones, w_small, head_sel)
    kx, qx = _forget_cumsum(fft, b_forget_col)
    t_fox = min(FOX_TILE, s)
    assert s // t_fox <= LANES, "the tile-skip bound keeps one key tile per lane"
    flast = qx[:, :, t_fox - 1::t_fox]
    flast = jnp.pad(flast, ((0, 0), (0, 0), (0, -flast.shape[2] % LANES)))
    o_fox = _fox_attention(fqt, qx, fk, kx, fvt, kn2, flast)
    o_gla = _gla(gq, gk, gv, small, w_gate_pad, b_gla_gate.reshape(1, GLA_K_WIDTH),
                 g_gla_norm.reshape(1, GLA_V_WIDTH))
    return _merge(x, mod3, g_pre, g_post.reshape(1, d), o_fox, o_gla, w_b, w_pool.astype(BF16),
                  pool_scale.reshape(1, POOL_WIDTH), w_br, w_out.astype(BF16))


def kernel(x, c, w_ada, b_ada, g_pre, g_post, w_in, b_forget, w_gla_gate, b_gla_gate, g_gla_norm, w_pool,
           pool_scale, w_br_fox, w_br_gla, w_br_pool, w_out):
    mods = _adaln_mod(c, w_ada, b_ada)
    h = x
    for i in range(w_in.shape[0]):
        h = _layer(h, mods[i], g_pre[i], g_post[i], w_in[i], b_forget[i], w_gla_gate[i], b_gla_gate[i],
                   g_gla_norm[i], w_pool[i], pool_scale[i], w_br_fox[i], w_br_gla[i], w_br_pool[i], w_out[i])
    return h
```

```python
import functools

import numpy as np
import jax
import jax.numpy as jnp
from jax import lax
from jax.experimental import pallas as pl
from jax.experimental.pallas import tpu as pltpu

F32 = jnp.float32
BF16 = jnp.bfloat16

D_MODEL = 1024
FOX_HEADS = 8
FOX_HEAD_DIM = 64
FOX_WIDTH = FOX_HEADS * FOX_HEAD_DIM
GLA_HEADS = 4
GLA_KEY_DIM = 64
GLA_VAL_DIM = 128
GLA_K_WIDTH = GLA_HEADS * GLA_KEY_DIM
GLA_V_WIDTH = GLA_HEADS * GLA_VAL_DIM
GLA_GATE_RANK = 16
GLA_TAU = 16.0
POOL_WINDOWS = (2, 4, 8, 16)
POOL_GROUP_DIM = 128
POOL_WIDTH = len(POOL_WINDOWS) * POOL_GROUP_DIM
N_BRANCHES = 3
RMS_EPS = 1e-6
IN_SPLITS = (FOX_WIDTH, FOX_WIDTH, FOX_WIDTH, FOX_HEADS,
             GLA_K_WIDTH, GLA_K_WIDTH, GLA_V_WIDTH, GLA_GATE_RANK,
             POOL_WIDTH, FOX_WIDTH, GLA_V_WIDTH, POOL_WIDTH, N_BRANCHES * D_MODEL)
_OFF = np.concatenate([[0], np.cumsum(IN_SPLITS)]).tolist()

LANES = 128
MXU_WIDTH = 256
V_ROWS = FOX_HEAD_DIM + 16
POOL_HALO = 8 * len(POOL_WINDOWS)
MASK_VALUE = -1e30
LOG2_E = 1.4426950408889634
SKIP_LOG2 = 140.0
NORM_SLACK = 1.02
V7X_VMEM_BYTES = 64 * 1024 * 1024

IN_TOKENS = 512
FORGET_TOKENS = 2048
FOX_TILE = 256
FOX_PAIRS = 4
GLA_TOKENS = 256
GLA_CHUNK = 64
GLA_BATCH = 8
MERGE_TOKENS = 512


def _vmem_limit(block_bytes, resident_bytes, temp_bytes):
    need = 2 * block_bytes + resident_bytes + temp_bytes
    return int(min(max(need, 16 * 1024 * 1024), V7X_VMEM_BYTES - 8 * 1024 * 1024))


def _resident(shape):
    zeros = (0,) * len(shape)
    return pl.BlockSpec(shape, lambda *_: zeros, pipeline_mode=pl.Buffered(1))


def _div_pow2(i, n):
    shift = n.bit_length() - 1
    assert 1 << shift == n
    return jnp.right_shift(i, shift)


def _log_sigmoid(x):
    return jnp.minimum(x, 0.0) - jnp.log(1.0 + jnp.exp(-jnp.abs(x)))


def _sigmoid(x):
    return jax.nn.sigmoid(x)


def _silu(x):
    return x * _sigmoid(x)


def _dot(a, b):
    return jnp.dot(a, b, preferred_element_type=F32)


def _dot_nt(a, b):
    return lax.dot_general(a, b, (((1,), (1,)), ((), ())), preferred_element_type=F32)


def _dot_tn(a, b):
    return lax.dot_general(a, b, (((0,), (0,)), ((), ())), preferred_element_type=F32)


def _split3(x):
    hi = x.astype(BF16)
    r1 = x - hi.astype(F32)
    mid = r1.astype(BF16)
    lo = (r1 - mid.astype(F32)).astype(BF16)
    return hi, mid, lo


def _ones_dot_f32(ones_bf16, x):
    hi, mid, lo = _split3(x)
    return _dot(ones_bf16, hi) + _dot(ones_bf16, mid) + _dot(ones_bf16, lo)


def _modulated_norm(x, g, mod):
    y = x * lax.rsqrt(jnp.mean(x * x, axis=-1, keepdims=True) + RMS_EPS) * g
    shift = mod[:, 0:D_MODEL]
    scale = mod[:, D_MODEL:2 * D_MODEL]
    return (y * (1.0 + scale) + shift).astype(BF16)


def _mod_kernel(c_ref, w_ref, b_ref, o_ref):
    c = c_ref[...]
    o_ref[0] = jnp.dot(_silu(c), w_ref[0], preferred_element_type=F32,
                       precision=lax.Precision.HIGHEST) + b_ref[0]


def _adaln_mod(c, w_ada, b_ada):
    depth, d, d3 = w_ada.shape
    b = c.shape[0]
    n_col = d3 // d
    return pl.pallas_call(
        _mod_kernel,
        grid=(depth, n_col),
        in_specs=[pl.BlockSpec((b, d), lambda l, j: (0, 0)),
                  pl.BlockSpec((1, d, d), lambda l, j: (l, 0, j)),
                  pl.BlockSpec((1, 1, d), lambda l, j: (l, 0, j))],
        out_specs=pl.BlockSpec((1, b, d), lambda l, j: (l, 0, j)),
        out_shape=jax.ShapeDtypeStruct((depth, b, d3), F32),
        compiler_params=pltpu.CompilerParams(
            dimension_semantics=("parallel", "parallel"),
            vmem_limit_bytes=_vmem_limit(d * d * 4 + 2 * b * d * 4, 0, 4 * b * d * 4)),
        name="adaln_mod",
    )(c, w_ada, b_ada.reshape(depth, 1, d3))


_A_FOX_K = 0
_A_GLA_Q = _A_FOX_K + FOX_WIDTH
_A_GLA_K = _A_GLA_Q + GLA_K_WIDTH
_A_GLA_V = _A_GLA_K + GLA_K_WIDTH
_A_WIDTH = _A_GLA_V + GLA_V_WIDTH
KN_ROWS = 16
GATE_ROWS = KN_ROWS + GLA_GATE_RANK
_T_FOX_Q = 0
_T_GATES = _T_FOX_Q + FOX_WIDTH
_T_FOX_V = _T_GATES + GATE_ROWS
_T_ROWS = _T_FOX_V + FOX_WIDTH


def _in_proj_kernel(x_ref, mod_ref, g_ref, wa_ref, wt_ref, hsel_ref,
                    fk_ref, gq_ref, gk_ref, gv_ref, fqt_ref, fvt_ref, gates_ref, kn2_ref):
    h = _modulated_norm(x_ref[0], g_ref[...], mod_ref[0])
    fk = _dot(h, wa_ref[:, _A_FOX_K:_A_GLA_Q]).astype(BF16)
    fk_ref[0] = fk
    norms = _dot_nt(hsel_ref[...], fk * fk)
    tile_max = jnp.broadcast_to(jnp.max(norms, axis=1, keepdims=True), kn2_ref.shape[1:])

    @pl.when(pl.program_id(1) == 0)
    def _():
        kn2_ref[0] = tile_max

    @pl.when(pl.program_id(1) > 0)
    def _():
        kn2_ref[0] = jnp.maximum(kn2_ref[0], tile_max)

    gq_ref[0] = _dot(h, wa_ref[:, _A_GLA_Q:_A_GLA_K]).astype(BF16)
    gk_ref[0] = _dot(h, wa_ref[:, _A_GLA_K:_A_GLA_V]).astype(BF16)
    gv_ref[0] = _dot(h, wa_ref[:, _A_GLA_V:_A_WIDTH]).astype(BF16)
    tr = _dot_nt(wt_ref[...], h)
    fqt_ref[0] = tr[_T_FOX_Q:_T_GATES].astype(BF16)
    gates_ref[0] = tr[_T_GATES:_T_FOX_V]
    ones = jnp.ones((V_ROWS - FOX_HEAD_DIM, tr.shape[1]), BF16)
    for hd in range(FOX_HEADS):
        v_rows = tr[_T_FOX_V + hd * FOX_HEAD_DIM:_T_FOX_V + (hd + 1) * FOX_HEAD_DIM]
        fvt_ref[0, hd * V_ROWS:hd * V_ROWS + FOX_HEAD_DIM, :] = v_rows.astype(BF16)
        fvt_ref[0, hd * V_ROWS + FOX_HEAD_DIM:(hd + 1) * V_ROWS, :] = ones


def _in_proj(x, mod, g_pre, w_a, w_t, head_sel):
    b, s, d = x.shape
    t = min(IN_TOKENS, s)
    tok = lambda width: pl.BlockSpec((1, t, width), lambda bi, i: (bi, i, 0))
    tok_t = lambda rows: pl.BlockSpec((1, rows, t), lambda bi, i: (bi, 0, i))
    v_rows = FOX_HEADS * V_ROWS
    out_shapes = (
        jax.ShapeDtypeStruct((b, s, FOX_WIDTH), BF16),
        jax.ShapeDtypeStruct((b, s, GLA_K_WIDTH), BF16),
        jax.ShapeDtypeStruct((b, s, GLA_K_WIDTH), BF16),
        jax.ShapeDtypeStruct((b, s, GLA_V_WIDTH), BF16),
        jax.ShapeDtypeStruct((b, FOX_WIDTH, s), BF16),
        jax.ShapeDtypeStruct((b, v_rows, s), BF16),
        jax.ShapeDtypeStruct((b, GATE_ROWS, s), F32),
        jax.ShapeDtypeStruct((b, KN_ROWS, LANES), F32),
    )
    out_specs = (tok(FOX_WIDTH), tok(GLA_K_WIDTH), tok(GLA_K_WIDTH), tok(GLA_V_WIDTH), tok_t(FOX_WIDTH), tok_t(v_rows),
                 tok_t(GATE_ROWS), pl.BlockSpec((1, KN_ROWS, LANES), lambda bi, i: (bi, 0, 0)))
    n_out = _A_WIDTH + FOX_WIDTH + v_rows
    block_bytes = t * d * 4 + t * n_out * 2 + t * GATE_ROWS * 4
    weight_bytes = d * (_A_WIDTH + _T_ROWS) * 2
    return pl.pallas_call(
        _in_proj_kernel,
        grid=(b, s // t),
        in_specs=[tok(d),
                  pl.BlockSpec((1, 1, 3 * d), lambda bi, i: (bi, 0, 0)),
                  _resident((1, d)), _resident(w_a.shape), _resident(w_t.shape), _resident(head_sel.shape)],
        out_specs=out_specs,
        out_shape=out_shapes,
        compiler_params=pltpu.CompilerParams(
            dimension_semantics=("parallel", "arbitrary"),
            vmem_limit_bytes=_vmem_limit(block_bytes, weight_bytes, 6 * t * d * 4)),
        name="in_proj",
    )(x, mod, g_pre, w_a, w_t, head_sel)


BIAS_SLOTS = 16


def _bias_placement():
    pk = np.zeros((3 * KN_ROWS, FOX_WIDTH), np.float32)
    k_const = np.zeros((1, FOX_WIDTH), np.float32)
    pq = np.zeros((FOX_HEADS * BIAS_SLOTS, 3 * KN_ROWS), np.float32)
    q_const = np.zeros((FOX_HEADS * BIAS_SLOTS, 1), np.float32)
    for head in range(FOX_HEADS):
        pair, odd = divmod(head, 2)
        for part in range(3):
            pk[part * KN_ROWS + head, pair * LANES + 3 + 3 * odd + part] = -1.0
            pq[head * BIAS_SLOTS + part, part * KN_ROWS + head] = 1.0
            q_const[head * BIAS_SLOTS + 3 + 3 * odd + part, 0] = 1.0
            k_const[0, pair * LANES + part] = 1.0
    return pk, k_const, pq, q_const


def _forget_kernel(fft_ref, b_ref, pk_ref, kc_ref, pq_ref, qc_ref, kx_ref, qx_ref, carry_ref):
    @pl.when(pl.program_id(1) == 0)
    def _():
        carry_ref[...] = jnp.zeros_like(carry_ref)

    t = fft_ref.shape[2]
    w = min(MXU_WIDTH, t)
    log_f = _log_sigmoid(fft_ref[0] + b_ref[...])
    row = lax.broadcasted_iota(jnp.int32, (w, w), 0)
    col = lax.broadcasted_iota(jnp.int32, (w, w), 1)
    upper = jnp.where(row <= col, 1.0, 0.0).astype(BF16)
    split = jnp.concatenate(_split3(log_f), axis=0)
    carry = carry_ref[:, 0:1]
    blocks = []
    for j in range(t // w):
        sums = _dot(split[:, j * w:(j + 1) * w], upper)
        blocks.append(sums[0:KN_ROWS] + sums[KN_ROWS:2 * KN_ROWS] + sums[2 * KN_ROWS:3 * KN_ROWS] + carry)
        carry = blocks[-1][:, w - 1:w]
    carry_ref[...] = jnp.broadcast_to(carry, carry_ref.shape)
    cs = jnp.concatenate(blocks, axis=1)
    parts = jnp.concatenate(_split3(cs * LOG2_E), axis=0)
    qx_ref[0] = (_dot(pq_ref[...], parts) + qc_ref[...]).astype(BF16)
    kx_ref[0] = (_dot_tn(parts, pk_ref[...]) + kc_ref[...]).astype(BF16)


def _forget_cumsum(gates, b_forget_col):
    b, _, s = gates.shape
    t = min(FORGET_TOKENS, s)
    pk, k_const, pq, q_const = _bias_placement()
    q_rows = FOX_HEADS * BIAS_SLOTS
    return pl.pallas_call(
        _forget_kernel,
        grid=(b, s // t),
        in_specs=[pl.BlockSpec((1, KN_ROWS, t), lambda bi, i: (bi, 0, i)),
                  _resident((KN_ROWS, 1)), _resident(pk.shape), _resident(k_const.shape),
                  _resident(pq.shape), _resident(q_const.shape)],
        out_specs=(pl.BlockSpec((1, t, FOX_WIDTH), lambda bi, i: (bi, i, 0)),
                   pl.BlockSpec((1, q_rows, t), lambda bi, i: (bi, 0, i))),
        out_shape=(jax.ShapeDtypeStruct((b, s, FOX_WIDTH), BF16),
                   jax.ShapeDtypeStruct((b, q_rows, s), BF16)),
        scratch_shapes=[pltpu.VMEM((KN_ROWS, LANES), F32)],
        compiler_params=pltpu.CompilerParams(
            dimension_semantics=("parallel", "arbitrary"),
            vmem_limit_bytes=_vmem_limit(t * (KN_ROWS * 4 + FOX_WIDTH * 2 + q_rows * 2), pk.size * 2 + pq.size * 2,
                                         8 * t * FOX_WIDTH * 4)),
        name="forget_cumsum",
    )(gates, b_forget_col, jnp.asarray(pk, BF16), jnp.asarray(k_const), jnp.asarray(pq, BF16), jnp.asarray(q_const))


def _fox_kernel(qt_ref, qx_ref, k_ref, kx_ref, vt_ref, kn2_ref, flast_ref, o_ref, s_ref, p_ref, acc_ref):
    t = qt_ref.shape[2]
    i = pl.program_id(2)
    n_heads = 2 * FOX_PAIRS
    row = lax.broadcasted_iota(jnp.int32, (LANES, t), 0)
    low = row < FOX_HEAD_DIM
    pad = jnp.zeros((LANES - BIAS_SLOTS, t), BF16)
    q_aug = []
    for hd in range(n_heads):
        qt = qt_ref[0, (hd // 2) * LANES:(hd // 2 + 1) * LANES, :]
        own = jnp.where(low, qt, jnp.zeros_like(qt)) if hd % 2 == 0 else jnp.where(low, jnp.zeros_like(qt), qt)
        q_aug.append(jnp.concatenate([own, qx_ref[0, hd * BIAS_SLOTS:(hd + 1) * BIAS_SLOTS, :], pad], axis=0))

    def logits(j, masked):
        k0 = pl.multiple_of(j * t, t)
        out = []
        for pr in range(FOX_PAIRS):
            lanes = slice(pr * LANES, (pr + 1) * LANES)
            k_aug = jnp.concatenate([k_ref[0, pl.ds(k0, t), lanes], kx_ref[0, pl.ds(k0, t), lanes]], axis=1)
            for h in range(2):
                s = _dot(k_aug, q_aug[2 * pr + h])
                if masked:
                    r = lax.broadcasted_iota(jnp.int32, (t, t), 0)
                    c = lax.broadcasted_iota(jnp.int32, (t, t), 1)
                    s = jnp.where(r <= c, s, MASK_VALUE)
                out.append(s)
        return out

    def weighted_values(j, hd, p):
        k0 = pl.multiple_of(j * t, t)
        return _dot(vt_ref[0, hd * V_ROWS:(hd + 1) * V_ROWS, pl.ds(k0, t)], p)

    def stage_logits(j, masked, slot):
        maxes = []
        for hd, s in enumerate(logits(j, masked)):
            s_ref[slot, hd] = s
            maxes.append(jnp.max(s, axis=0, keepdims=True))
        return maxes

    def stage_softmax(hd, s_max, m, s_slot, p_slot):
        m_new = jnp.maximum(m, s_max)
        p_ref[p_slot, hd] = jnp.exp2(s_ref[s_slot, hd] - m_new).astype(BF16)
        return m_new, jnp.exp2(m - m_new)

    m0 = jnp.full((1, t), MASK_VALUE, F32)
    s_max = stage_logits(i, True, 1)
    stats = [stage_softmax(hd, s_max[hd], m0, 1, 0) for hd in range(n_heads)]
    s_max = stage_logits(jnp.maximum(i - 1, 0), False, 0)
    acc_ref[...] = jnp.zeros_like(acc_ref)

    tile_id = lax.broadcasted_iota(jnp.int32, (1, LANES), 1).astype(F32)
    first_head = pl.program_id(1) * n_heads
    first_needed = []
    for hd in range(n_heads):
        base = hd * BIAS_SLOTS
        f_k = sum(flast_ref[0, base + part:base + part + 1, :].astype(F32) for part in range(3))
        f_q = sum(qx_ref[0, base + part:base + part + 1, 0:1].astype(F32) for part in range(3))
        q = qt_ref[0, hd * FOX_HEAD_DIM:(hd + 1) * FOX_HEAD_DIM, :].astype(F32)
        qn2 = jnp.max(jnp.sum(q * q, axis=0, keepdims=True), axis=1, keepdims=True)
        qk = jnp.sqrt(qn2 * kn2_ref[0, pl.ds(first_head + hd, 1), 0:1]) * NORM_SLACK
        m_min = jnp.min(stats[hd][0], axis=1, keepdims=True)
        needed = (qk + f_q - f_k >= m_min - SKIP_LOG2) & (tile_id < i.astype(F32))
        first_needed.append(jnp.min(jnp.where(needed, tile_id, i.astype(F32)), axis=1, keepdims=True))
    first_tile = jnp.min(jnp.concatenate(first_needed, axis=1), axis=1, keepdims=True)[0, 0].astype(jnp.int32)
    n_trips = i - first_tile

    def trip(par, fetch, k, carry):
        prev, heads = carry
        for hd in range(n_heads):
            pv = weighted_values(prev, hd, p_ref[par, hd])
            acc_ref[hd] = heads[hd][1] * acc_ref[hd] + pv
        stats = [stage_softmax(hd, heads[hd][2], heads[hd][0], par, 1 - par) for hd in range(n_heads)]
        s_max = stage_logits(i - 2 - k, False, 1 - par) if fetch else [h[2] for h in heads]
        return i - 1 - k, tuple(stats[hd] + (s_max[hd],) for hd in range(n_heads))

    def either_trip(fetch, k, carry):
        return lax.cond((k & 1) == 0, functools.partial(trip, 0, fetch, k), functools.partial(trip, 1, fetch, k), carry)

    carry = (i, tuple(stats[hd] + (s_max[hd],) for hd in range(n_heads)))
    carry = lax.fori_loop(0, n_trips - 1, functools.partial(either_trip, True), carry)
    prev, heads = lax.cond(n_trips > 0, functools.partial(either_trip, False, n_trips - 1), lambda c: c, carry)
    o_heads = []
    for hd in range(n_heads):
        acc = heads[hd][1] * acc_ref[hd] + weighted_values(prev, hd, p_ref[n_trips & 1, hd])
        o_heads.append(acc[0:FOX_HEAD_DIM] * (1.0 / acc[FOX_HEAD_DIM:FOX_HEAD_DIM + 1]))
    o_ref[0] = jnp.concatenate(o_heads, axis=0).T.astype(o_ref.dtype)


def _fox_attention(fqt, qx, fk, kx, fvt, kn2, flast):
    b, s, _ = fk.shape
    t = min(FOX_TILE, s)
    width = FOX_PAIRS * LANES
    n_heads = 2 * FOX_PAIRS
    groups = FOX_WIDTH // width
    whole = lambda shape, index_map: pl.BlockSpec(shape, index_map, pipeline_mode=pl.Buffered(1))
    block_bytes = (width + n_heads * BIAS_SLOTS) * t * 2 + t * width * 2
    resident_bytes = (2 * width + n_heads * V_ROWS) * s * 2 + n_heads * t * (2 * t * 6 + V_ROWS * 4)
    return pl.pallas_call(
        _fox_kernel,
        grid=(b, groups, s // t),
        in_specs=[pl.BlockSpec((1, width, t), lambda bi, g, i: (bi, g, i)),
                  pl.BlockSpec((1, n_heads * BIAS_SLOTS, t), lambda bi, g, i: (bi, g, i)),
                  whole((1, s, width), lambda bi, g, i: (bi, 0, g)),
                  whole((1, s, width), lambda bi, g, i: (bi, 0, g)),
                  whole((1, n_heads * V_ROWS, s), lambda bi, g, i: (bi, g, 0)),
                  whole((1, KN_ROWS, LANES), lambda bi, g, i: (bi, 0, 0)),
                  whole((1, n_heads * BIAS_SLOTS, flast.shape[2]), lambda bi, g, i: (bi, g, 0))],
        out_specs=pl.BlockSpec((1, t, width), lambda bi, g, i: (bi, i, g)),
        out_shape=jax.ShapeDtypeStruct((b, s, FOX_WIDTH), BF16),
        scratch_shapes=[pltpu.VMEM((2, n_heads, t, t), F32), pltpu.VMEM((2, n_heads, t, t), BF16),
                        pltpu.VMEM((n_heads, V_ROWS, t), F32)],
        compiler_params=pltpu.CompilerParams(
            dimension_semantics=("parallel", "parallel", "parallel"),
            vmem_limit_bytes=_vmem_limit(block_bytes, resident_bytes, 8 * n_heads * t * t)),
        name="fox_attention",
    )(fqt, qx, fk, kx, fvt, kn2, flast)


def _gla_kernel(q_ref, k_ref, v_ref, gates_ref, wg_ref, bg_ref, gn_ref, o_ref, state_ref):
    @pl.when(pl.program_id(1) == 0)
    def _():
        state_ref[...] = jnp.zeros_like(state_ref)

    nb, t, _ = q_ref.shape
    c = GLA_CHUNK
    half = c // 2
    row = lax.broadcasted_iota(jnp.int32, (t, t), 0)
    col = lax.broadcasted_iota(jnp.int32, (t, t), 1)
    same_chunk = _div_pow2(row, c) == _div_pow2(col, c)
    tri = jnp.where(same_chunk & (col <= row), 1.0, 0.0).astype(BF16)
    cum_all = []
    for bb in range(nb):
        gate_logits = _dot_tn(gates_ref[bb, KN_ROWS:GATE_ROWS, :].astype(BF16), wg_ref[...]) + bg_ref[...]
        cum_all.append(_ones_dot_f32(tri, _log_sigmoid(gate_logits) * (1.0 / GLA_TAU)))

    crow = lax.broadcasted_iota(jnp.int32, (c, GLA_K_WIDTH), 0)
    first = crow < half
    lane = lax.broadcasted_iota(jnp.int32, (c, GLA_K_WIDTH), 1)
    head_lanes = [_div_pow2(lane, GLA_KEY_DIM) == h for h in range(GLA_HEADS)]
    sr = lax.broadcasted_iota(jnp.int32, (GLA_HEADS * c, c), 0) & (c - 1)
    sc = lax.broadcasted_iota(jnp.int32, (GLA_HEADS * c, c), 1)
    near = (_div_pow2(sr, half) == _div_pow2(sc, half)) & (sc <= sr)
    srow = lax.broadcasted_iota(jnp.int32, state_ref.shape[1:], 0)
    slane = lax.broadcasted_iota(jnp.int32, state_ref.shape[1:], 1)
    own_head = _div_pow2(srow, GLA_VAL_DIM) == _div_pow2(slane, GLA_KEY_DIM)

    def stack_heads(x):
        return jnp.concatenate([jnp.where(m, x, 0.0) for m in head_lanes], axis=0).astype(BF16)

    for ci, bb in [(ci, bb) for ci in range(t // c) for bb in range(nb)]:
        r0 = ci * c
        cum = cum_all[bb][r0:r0 + c]
        q = q_ref[bb, r0:r0 + c, :].astype(F32)
        k = k_ref[bb, r0:r0 + c, :].astype(F32)
        v = v_ref[bb, r0:r0 + c, :]
        last = cum[c - 1:c]
        ref_far = cum[half - 1:half]
        ref_near = jnp.where(first, cum[half // 2:half // 2 + 1], cum[half + half // 2:half + half // 2 + 1])
        q_far = jnp.where(first, 0.0, q * jnp.exp(cum - ref_far))
        k_far = jnp.where(first, k * jnp.exp(ref_far - cum), 0.0)
        q_near = q * jnp.exp(cum - ref_near)
        k_near = k * jnp.exp(ref_near - cum)
        scores = (_dot_nt(stack_heads(q_far), k_far.astype(BF16))
                  + jnp.where(near, _dot_nt(stack_heads(q_near), k_near.astype(BF16)), 0.0))
        state = state_ref[bb]
        o = _dot_nt((q * jnp.exp(cum)).astype(BF16), state.astype(BF16))
        p = scores.astype(BF16)
        for h in range(GLA_HEADS):
            lo, hi = h * GLA_VAL_DIM, (h + 1) * GLA_VAL_DIM
            o_h = o[:, lo:hi] + _dot(p[h * c:(h + 1) * c], v[:, lo:hi])
            y = o_h * lax.rsqrt(jnp.mean(o_h * o_h, axis=-1, keepdims=True) + RMS_EPS) * gn_ref[:, lo:hi]
            o_ref[bb, r0:r0 + c, lo:hi] = y.astype(o_ref.dtype)
        k_tail = (k * jnp.exp(last - cum)).astype(BF16)
        update = _dot_tn(v, k_tail)
        state_ref[bb] = state * jnp.exp(last) + jnp.where(own_head, update, 0.0)


def _gla(gq, gk, gv, gates, w_gate, b_gate, g_norm):
    b, s, _ = gq.shape
    t = min(GLA_TOKENS, s)
    nb = GLA_BATCH if b % GLA_BATCH == 0 else 1
    tok = lambda width: pl.BlockSpec((nb, t, width), lambda bi, i: (bi, i, 0))
    block_bytes = nb * t * (2 * GLA_K_WIDTH * 2 + 2 * GLA_V_WIDTH * 2 + GATE_ROWS * 4)
    state_bytes = nb * GLA_V_WIDTH * GLA_K_WIDTH * 4
    return pl.pallas_call(
        _gla_kernel,
        grid=(b // nb, s // t),
        in_specs=[tok(GLA_K_WIDTH), tok(GLA_K_WIDTH), tok(GLA_V_WIDTH),
                  pl.BlockSpec((nb, GATE_ROWS, t), lambda bi, i: (bi, 0, i)),
                  _resident(w_gate.shape), _resident((1, GLA_K_WIDTH)), _resident((1, GLA_V_WIDTH))],
        out_specs=tok(GLA_V_WIDTH),
        out_shape=jax.ShapeDtypeStruct((b, s, GLA_V_WIDTH), BF16),
        scratch_shapes=[pltpu.VMEM((nb, GLA_V_WIDTH, GLA_K_WIDTH), F32)],
        compiler_params=pltpu.CompilerParams(
            dimension_semantics=("parallel", "arbitrary"),
            vmem_limit_bytes=_vmem_limit(block_bytes, state_bytes, 8 * state_bytes + 16 * t * GLA_K_WIDTH * 4)),
        name="gla",
    )(gq, gk, gv, gates, w_gate, b_gate, g_norm)


_B_POOL = 0
_B_Z = _B_POOL + POOL_WIDTH
_B_MERGE = _B_Z + FOX_WIDTH + GLA_V_WIDTH + POOL_WIDTH
_B_WIDTH = _B_MERGE + N_BRANCHES * D_MODEL


def _merge_kernel(x_ref, mod_ref, gpre_ref, gpost_ref, ofox_ref, ogla_ref, wb_ref, wpool_ref, pscale_ref,
                  wbr_ref, wout_ref, o_ref, u_ref, lvl_a_ref, lvl_b_ref):
    t = x_ref.shape[1]
    i = pl.program_id(1)
    x = x_ref[0]
    mod = mod_ref[0]
    h = _modulated_norm(x, gpre_ref[...], mod)

    @pl.when(i == 0)
    def _():
        u_ref[0:POOL_HALO, :] = jnp.zeros((POOL_HALO, POOL_WIDTH), F32)

    @pl.when(i > 0)
    def _():
        u_ref[0:POOL_HALO, :] = u_ref[t:t + POOL_HALO, :]

    u = _dot(h, wb_ref[:, _B_POOL:_B_Z])
    u_ref[POOL_HALO:POOL_HALO + t, :] = u
    count = (i * t + 1 + lax.broadcasted_iota(jnp.int32, (t, 1), 0)).astype(F32)
    levels = (u_ref, lvl_a_ref, lvl_b_ref, lvl_a_ref, lvl_b_ref)
    for n in range(1, len(POOL_WINDOWS) + 1):
        src, dst, back = levels[n - 1], levels[n], 2 ** (n - 1)
        first, rows = 8 * n, POOL_HALO + t - 8 * n
        lanes = slice((n - 1) * POOL_GROUP_DIM, POOL_WIDTH)
        dst[pl.ds(first, rows), lanes] = src[pl.ds(first, rows), lanes] + src[pl.ds(first - back, rows), lanes]
    pooled = []
    for g, w in enumerate(POOL_WINDOWS):
        lo, hi = g * POOL_GROUP_DIM, (g + 1) * POOL_GROUP_DIM
        window = levels[g + 1][pl.ds(POOL_HALO, t), lo:hi]
        mean = window / jnp.minimum(count, float(w))
        diff = (mean - u[:, lo:hi]).astype(BF16)
        pooled.append(_dot(diff, wpool_ref[g]) * pscale_ref[:, lo:hi])
    o_pool = jnp.concatenate(pooled, axis=-1)

    branches = (ofox_ref[0].astype(F32), ogla_ref[0].astype(F32), o_pool)
    merged = jnp.zeros((t, D_MODEL), F32)
    for br, o_br in enumerate(branches):
        z = _dot(h, wb_ref[:, _B_Z + br * FOX_WIDTH:_B_Z + (br + 1) * FOX_WIDTH])
        y = _dot((o_br * _silu(z)).astype(BF16), wbr_ref[br])
        m = _dot(h, wb_ref[:, _B_MERGE + br * D_MODEL:_B_MERGE + (br + 1) * D_MODEL])
        merged = merged + _sigmoid(m) * y
    out = _dot(merged.astype(BF16), wout_ref[...])
    out = out * lax.rsqrt(jnp.mean(out * out, axis=-1, keepdims=True) + RMS_EPS) * gpost_ref[...]
    gate = mod[:, 2 * D_MODEL:3 * D_MODEL]
    o_ref[0] = x + gate * out


def _merge(x, mod, g_pre, g_post, o_fox, o_gla, w_b, w_pool, pool_scale, w_br, w_out):
    b, s, d = x.shape
    t = min(MERGE_TOKENS, s)
    tok = lambda width: pl.BlockSpec((1, t, width), lambda bi, i: (bi, i, 0))
    block_bytes = 2 * t * d * 4 + 2 * t * FOX_WIDTH * 2
    weight_bytes = (w_b.size + w_pool.size + w_br.size + w_out.size) * 2
    return pl.pallas_call(
        _merge_kernel,
        grid=(b, s // t),
        in_specs=[tok(d),
                  pl.BlockSpec((1, 1, 3 * d), lambda bi, i: (bi, 0, 0)),
                  _resident((1, d)), _resident((1, d)),
                  tok(FOX_WIDTH), tok(GLA_V_WIDTH),
                  _resident(w_b.shape), _resident(w_pool.shape), _resident((1, POOL_WIDTH)),
                  _resident(w_br.shape), _resident(w_out.shape)],
        out_specs=tok(d),
        out_shape=jax.ShapeDtypeStruct((b, s, d), F32),
        scratch_shapes=[pltpu.VMEM((POOL_HALO + t, POOL_WIDTH), F32)] * 3,
        compiler_params=pltpu.CompilerParams(
            dimension_semantics=("parallel", "arbitrary"),
            vmem_limit_bytes=_vmem_limit(block_bytes, weight_bytes, 12 * t * d * 4)),
        name="merge",
    )(x, mod, g_pre, g_post, o_fox, o_gla, w_b, w_pool, pool_scale, w_br, w_out)


def _layer(x, mod, g_pre, g_post, w_in, b_forget, w_gla_gate, b_gla_gate, g_gla_norm, w_pool, pool_scale,
           w_br_fox, w_br_gla, w_br_pool, w_out):
    b, s, d = x.shape
    seg = lambda n: w_in[:, _OFF[n]:_OFF[n + 1]]
    w_a = jnp.concatenate([seg(1), seg(4) * GLA_KEY_DIM ** -0.5, seg(5), seg(6)], axis=1).astype(BF16)
    w_t = jnp.concatenate([seg(0) * (LOG2_E * FOX_HEAD_DIM ** -0.5), seg(3),
                           jnp.zeros((d, KN_ROWS - FOX_HEADS), w_in.dtype), seg(7), seg(2)], axis=1).T.astype(BF16)
    w_b = w_in[:, _OFF[8]:_OFF[13]].astype(BF16)
    b_forget_col = jnp.zeros((KN_ROWS, 1), F32).at[:FOX_HEADS, 0].set(b_forget)
    w_br = jnp.stack([w_br_fox, w_br_gla, w_br_pool]).astype(BF16)
    mod3 = mod.reshape(b, 1, 3 * d)
    g_pre = g_pre.reshape(1, d)

    head_sel = (jnp.arange(KN_ROWS)[:, None] == jnp.arange(FOX_WIDTH)[None, :] // FOX_HEAD_DIM).astype(BF16)
    fk, gq, gk, gv, fqt, fvt, gates, kn2 = _in_proj(x, mod3, g_pre, w_a, w_t, head_sel)
    kx, qx = _forget_cumsum(gates, b_forget_col)
    t_fox = min(FOX_TILE, s)
    assert s // t_fox <= LANES, "the tile-skip bound keeps one key tile per lane"
    flast = qx[:, :, t_fox - 1::t_fox]
    flast = jnp.pad(flast, ((0, 0), (0, 0), (0, -flast.shape[2] % LANES)))
    o_fox = _fox_attention(fqt, qx, fk, kx, fvt, kn2, flast)
    o_gla = _gla(gq, gk, gv, gates, w_gla_gate.astype(BF16), b_gla_gate.reshape(1, GLA_K_WIDTH),
                 g_gla_norm.reshape(1, GLA_V_WIDTH))
    return _merge(x, mod3, g_pre, g_post.reshape(1, d), o_fox, o_gla, w_b, w_pool.astype(BF16),
                  pool_scale.reshape(1, POOL_WIDTH), w_br, w_out.astype(BF16))


def kernel(x, c, w_ada, b_ada, g_pre, g_post, w_in, b_forget, w_gla_gate, b_gla_gate, g_gla_norm, w_pool,
           pool_scale, w_br_fox, w_br_gla, w_br_pool, w_out):
    mods = _adaln_mod(c, w_ada, b_ada)
    h = x
    for i in range(w_in.shape[0]):
        h = _layer(h, mods[i], g_pre[i], g_post[i], w_in[i], b_forget[i], w_gla_gate[i], b_gla_gate[i],
                   g_gla_norm[i], w_pool[i], pool_scale[i], w_br_fox[i], w_br_gla[i], w_br_pool[i], w_out[i])
    return h
```

```python
import functools

import numpy as np
import jax
import jax.numpy as jnp
from jax import lax
from jax.experimental import pallas as pl
from jax.experimental.pallas import tpu as pltpu

F32 = jnp.float32
BF16 = jnp.bfloat16

D_MODEL = 1024
FOX_HEADS = 8
FOX_HEAD_DIM = 64
FOX_WIDTH = FOX_HEADS * FOX_HEAD_DIM
GLA_HEADS = 4
GLA_KEY_DIM = 64
GLA_VAL_DIM = 128
GLA_K_WIDTH = GLA_HEADS * GLA_KEY_DIM
GLA_V_WIDTH = GLA_HEADS * GLA_VAL_DIM
GLA_GATE_RANK = 16
GLA_TAU = 16.0
POOL_WINDOWS = (2, 4, 8, 16)
POOL_GROUP_DIM = 128
POOL_WIDTH = len(POOL_WINDOWS) * POOL_GROUP_DIM
N_BRANCHES = 3
RMS_EPS = 1e-6
IN_SPLITS = (FOX_WIDTH, FOX_WIDTH, FOX_WIDTH, FOX_HEADS,
             GLA_K_WIDTH, GLA_K_WIDTH, GLA_V_WIDTH, GLA_GATE_RANK,
             POOL_WIDTH, FOX_WIDTH, GLA_V_WIDTH, POOL_WIDTH, N_BRANCHES * D_MODEL)
_OFF = np.concatenate([[0], np.cumsum(IN_SPLITS)]).tolist()

LANES = 128
MXU_WIDTH = 256
V_ROWS = FOX_HEAD_DIM + 16
POOL_HALO = 8 * len(POOL_WINDOWS)
MASK_VALUE = -1e30
LOG2_E = 1.4426950408889634
SKIP_LOG2 = 140.0
NORM_SLACK = 1.02
V7X_VMEM_BYTES = 64 * 1024 * 1024

IN_TOKENS = 512
FORGET_TOKENS = 2048
FOX_TILE = 256
FOX_PAIRS = 4
GLA_TOKENS = 256
GLA_CHUNK = 64
GLA_BATCH = 8
MERGE_TOKENS = 512


def _vmem_limit(block_bytes, resident_bytes, temp_bytes):
    need = 2 * block_bytes + resident_bytes + temp_bytes
    return int(min(max(need, 16 * 1024 * 1024), V7X_VMEM_BYTES - 8 * 1024 * 1024))


def _resident(shape):
    zeros = (0,) * len(shape)
    return pl.BlockSpec(shape, lambda *_: zeros, pipeline_mode=pl.Buffered(1))


def _div_pow2(i, n):
    shift = n.bit_length() - 1
    assert 1 << shift == n
    return jnp.right_shift(i, shift)


def _log_sigmoid(x):
    return jnp.minimum(x, 0.0) - jnp.log(1.0 + jnp.exp(-jnp.abs(x)))


def _sigmoid(x):
    return 0.5 * jnp.tanh(0.5 * x) + 0.5


def _silu(x):
    return x * _sigmoid(x)


def _dot(a, b):
    return jnp.dot(a, b, preferred_element_type=F32)


def _dot_nt(a, b):
    return lax.dot_general(a, b, (((1,), (1,)), ((), ())), preferred_element_type=F32)


def _dot_tn(a, b):
    return lax.dot_general(a, b, (((0,), (0,)), ((), ())), preferred_element_type=F32)


def _split3(x):
    hi = x.astype(BF16)
    r1 = x - hi.astype(F32)
    mid = r1.astype(BF16)
    lo = (r1 - mid.astype(F32)).astype(BF16)
    return hi, mid, lo


def _ones_dot_f32(ones_bf16, x):
    hi, mid, lo = _split3(x)
    return _dot(ones_bf16, hi) + _dot(ones_bf16, mid) + _dot(ones_bf16, lo)


def _modulated_norm(x, g, mod):
    y = x * lax.rsqrt(jnp.mean(x * x, axis=-1, keepdims=True) + RMS_EPS) * g
    shift = mod[:, 0:D_MODEL]
    scale = mod[:, D_MODEL:2 * D_MODEL]
    return (y * (1.0 + scale) + shift).astype(BF16)


def _mod_kernel(c_ref, w_ref, b_ref, o_ref):
    c = c_ref[...]
    o_ref[0] = jnp.dot(_silu(c), w_ref[0], preferred_element_type=F32,
                       precision=lax.Precision.HIGHEST) + b_ref[0]


def _adaln_mod(c, w_ada, b_ada):
    depth, d, d3 = w_ada.shape
    b = c.shape[0]
    n_col = d3 // d
    return pl.pallas_call(
        _mod_kernel,
        grid=(depth, n_col),
        in_specs=[pl.BlockSpec((b, d), lambda l, j: (0, 0)),
                  pl.BlockSpec((1, d, d), lambda l, j: (l, 0, j)),
                  pl.BlockSpec((1, 1, d), lambda l, j: (l, 0, j))],
        out_specs=pl.BlockSpec((1, b, d), lambda l, j: (l, 0, j)),
        out_shape=jax.ShapeDtypeStruct((depth, b, d3), F32),
        compiler_params=pltpu.CompilerParams(
            dimension_semantics=("parallel", "parallel"),
            vmem_limit_bytes=_vmem_limit(d * d * 4 + 2 * b * d * 4, 0, 4 * b * d * 4)),
        name="adaln_mod",
    )(c, w_ada, b_ada.reshape(depth, 1, d3))


_A_FOX_K = 0
_A_GLA_Q = _A_FOX_K + FOX_WIDTH
_A_GLA_K = _A_GLA_Q + GLA_K_WIDTH
_A_GLA_V = _A_GLA_K + GLA_K_WIDTH
_A_WIDTH = _A_GLA_V + GLA_V_WIDTH
KN_ROWS = 16
GATE_ROWS = KN_ROWS + GLA_GATE_RANK
_T_FOX_Q = 0
_T_GATES = _T_FOX_Q + FOX_WIDTH
_T_FOX_V = _T_GATES + GATE_ROWS
_T_ROWS = _T_FOX_V + FOX_WIDTH


def _in_proj_kernel(x_ref, mod_ref, g_ref, wa_ref, wt_ref, hsel_ref,
                    fk_ref, gq_ref, gk_ref, gv_ref, fqt_ref, fvt_ref, gates_ref, kn2_ref):
    h = _modulated_norm(x_ref[0], g_ref[...], mod_ref[0])
    fk = _dot(h, wa_ref[:, _A_FOX_K:_A_GLA_Q]).astype(BF16)
    fk_ref[0] = fk
    norms = _dot_nt(hsel_ref[...], fk * fk)
    tile_max = jnp.broadcast_to(jnp.max(norms, axis=1, keepdims=True), kn2_ref.shape[1:])

    @pl.when(pl.program_id(1) == 0)
    def _():
        kn2_ref[0] = tile_max

    @pl.when(pl.program_id(1) > 0)
    def _():
        kn2_ref[0] = jnp.maximum(kn2_ref[0], tile_max)

    gq_ref[0] = _dot(h, wa_ref[:, _A_GLA_Q:_A_GLA_K]).astype(BF16)
    gk_ref[0] = _dot(h, wa_ref[:, _A_GLA_K:_A_GLA_V]).astype(BF16)
    gv_ref[0] = _dot(h, wa_ref[:, _A_GLA_V:_A_WIDTH]).astype(BF16)
    tr = _dot_nt(wt_ref[...], h)
    fqt_ref[0] = tr[_T_FOX_Q:_T_GATES].astype(BF16)
    gates_ref[0] = tr[_T_GATES:_T_FOX_V]
    ones = jnp.ones((V_ROWS - FOX_HEAD_DIM, tr.shape[1]), BF16)
    for hd in range(FOX_HEADS):
        v_rows = tr[_T_FOX_V + hd * FOX_HEAD_DIM:_T_FOX_V + (hd + 1) * FOX_HEAD_DIM]
        fvt_ref[0, hd * V_ROWS:hd * V_ROWS + FOX_HEAD_DIM, :] = v_rows.astype(BF16)
        fvt_ref[0, hd * V_ROWS + FOX_HEAD_DIM:(hd + 1) * V_ROWS, :] = ones


def _in_proj(x, mod, g_pre, w_a, w_t, head_sel):
    b, s, d = x.shape
    t = min(IN_TOKENS, s)
    tok = lambda width: pl.BlockSpec((1, t, width), lambda bi, i: (bi, i, 0))
    tok_t = lambda rows: pl.BlockSpec((1, rows, t), lambda bi, i: (bi, 0, i))
    v_rows = FOX_HEADS * V_ROWS
    out_shapes = (
        jax.ShapeDtypeStruct((b, s, FOX_WIDTH), BF16),
        jax.ShapeDtypeStruct((b, s, GLA_K_WIDTH), BF16),
        jax.ShapeDtypeStruct((b, s, GLA_K_WIDTH), BF16),
        jax.ShapeDtypeStruct((b, s, GLA_V_WIDTH), BF16),
        jax.ShapeDtypeStruct((b, FOX_WIDTH, s), BF16),
        jax.ShapeDtypeStruct((b, v_rows, s), BF16),
        jax.ShapeDtypeStruct((b, GATE_ROWS, s), F32),
        jax.ShapeDtypeStruct((b, KN_ROWS, LANES), F32),
    )
    out_specs = (tok(FOX_WIDTH), tok(GLA_K_WIDTH), tok(GLA_K_WIDTH), tok(GLA_V_WIDTH), tok_t(FOX_WIDTH), tok_t(v_rows),
                 tok_t(GATE_ROWS), pl.BlockSpec((1, KN_ROWS, LANES), lambda bi, i: (bi, 0, 0)))
    n_out = _A_WIDTH + FOX_WIDTH + v_rows
    block_bytes = t * d * 4 + t * n_out * 2 + t * GATE_ROWS * 4
    weight_bytes = d * (_A_WIDTH + _T_ROWS) * 2
    return pl.pallas_call(
        _in_proj_kernel,
        grid=(b, s // t),
        in_specs=[tok(d),
                  pl.BlockSpec((1, 1, 3 * d), lambda bi, i: (bi, 0, 0)),
                  _resident((1, d)), _resident(w_a.shape), _resident(w_t.shape), _resident(head_sel.shape)],
        out_specs=out_specs,
        out_shape=out_shapes,
        compiler_params=pltpu.CompilerParams(
            dimension_semantics=("parallel", "arbitrary"),
            vmem_limit_bytes=_vmem_limit(block_bytes, weight_bytes, 6 * t * d * 4)),
        name="in_proj",
    )(x, mod, g_pre, w_a, w_t, head_sel)


BIAS_SLOTS = 16


def _bias_placement():
    pk = np.zeros((3 * KN_ROWS, FOX_WIDTH), np.float32)
    k_const = np.zeros((1, FOX_WIDTH), np.float32)
    pq = np.zeros((FOX_HEADS * BIAS_SLOTS, 3 * KN_ROWS), np.float32)
    q_const = np.zeros((FOX_HEADS * BIAS_SLOTS, 1), np.float32)
    for head in range(FOX_HEADS):
        pair, odd = divmod(head, 2)
        for part in range(3):
            pk[part * KN_ROWS + head, pair * LANES + 3 + 3 * odd + part] = -1.0
            pq[head * BIAS_SLOTS + part, part * KN_ROWS + head] = 1.0
            q_const[head * BIAS_SLOTS + 3 + 3 * odd + part, 0] = 1.0
            k_const[0, pair * LANES + part] = 1.0
    return pk, k_const, pq, q_const


def _forget_kernel(fft_ref, b_ref, pk_ref, kc_ref, pq_ref, qc_ref, kx_ref, qx_ref, carry_ref):
    @pl.when(pl.program_id(1) == 0)
    def _():
        carry_ref[...] = jnp.zeros_like(carry_ref)

    t = fft_ref.shape[2]
    w = min(MXU_WIDTH, t)
    log_f = _log_sigmoid(fft_ref[0] + b_ref[...])
    row = lax.broadcasted_iota(jnp.int32, (w, w), 0)
    col = lax.broadcasted_iota(jnp.int32, (w, w), 1)
    upper = jnp.where(row <= col, 1.0, 0.0).astype(BF16)
    split = jnp.concatenate(_split3(log_f), axis=0)
    carry = carry_ref[:, 0:1]
    blocks = []
    for j in range(t // w):
        sums = _dot(split[:, j * w:(j + 1) * w], upper)
        blocks.append(sums[0:KN_ROWS] + sums[KN_ROWS:2 * KN_ROWS] + sums[2 * KN_ROWS:3 * KN_ROWS] + carry)
        carry = blocks[-1][:, w - 1:w]
    carry_ref[...] = jnp.broadcast_to(carry, carry_ref.shape)
    cs = jnp.concatenate(blocks, axis=1)
    parts = jnp.concatenate(_split3(cs * LOG2_E), axis=0)
    qx_ref[0] = (_dot(pq_ref[...], parts) + qc_ref[...]).astype(BF16)
    kx_ref[0] = (_dot_tn(parts, pk_ref[...]) + kc_ref[...]).astype(BF16)


def _forget_cumsum(gates, b_forget_col):
    b, _, s = gates.shape
    t = min(FORGET_TOKENS, s)
    pk, k_const, pq, q_const = _bias_placement()
    q_rows = FOX_HEADS * BIAS_SLOTS
    return pl.pallas_call(
        _forget_kernel,
        grid=(b, s // t),
        in_specs=[pl.BlockSpec((1, KN_ROWS, t), lambda bi, i: (bi, 0, i)),
                  _resident((KN_ROWS, 1)), _resident(pk.shape), _resident(k_const.shape),
                  _resident(pq.shape), _resident(q_const.shape)],
        out_specs=(pl.BlockSpec((1, t, FOX_WIDTH), lambda bi, i: (bi, i, 0)),
                   pl.BlockSpec((1, q_rows, t), lambda bi, i: (bi, 0, i))),
        out_shape=(jax.ShapeDtypeStruct((b, s, FOX_WIDTH), BF16),
                   jax.ShapeDtypeStruct((b, q_rows, s), BF16)),
        scratch_shapes=[pltpu.VMEM((KN_ROWS, LANES), F32)],
        compiler_params=pltpu.CompilerParams(
            dimension_semantics=("parallel", "arbitrary"),
            vmem_limit_bytes=_vmem_limit(t * (KN_ROWS * 4 + FOX_WIDTH * 2 + q_rows * 2), pk.size * 2 + pq.size * 2,
                                         8 * t * FOX_WIDTH * 4)),
        name="forget_cumsum",
    )(gates, b_forget_col, jnp.asarray(pk, BF16), jnp.asarray(k_const), jnp.asarray(pq, BF16), jnp.asarray(q_const))


def _fox_kernel(qt_ref, qx_ref, k_ref, kx_ref, vt_ref, kn2_ref, flast_ref, o_ref, s_ref, p_ref, acc_ref):
    t = qt_ref.shape[2]
    i = pl.program_id(2)
    n_heads = 2 * FOX_PAIRS
    row = lax.broadcasted_iota(jnp.int32, (LANES, t), 0)
    low = row < FOX_HEAD_DIM
    pad = jnp.zeros((LANES - BIAS_SLOTS, t), BF16)
    q_aug = []
    for hd in range(n_heads):
        qt = qt_ref[0, (hd // 2) * LANES:(hd // 2 + 1) * LANES, :]
        own = jnp.where(low, qt, jnp.zeros_like(qt)) if hd % 2 == 0 else jnp.where(low, jnp.zeros_like(qt), qt)
        q_aug.append(jnp.concatenate([own, qx_ref[0, hd * BIAS_SLOTS:(hd + 1) * BIAS_SLOTS, :], pad], axis=0))

    def logits(j, masked):
        k0 = pl.multiple_of(j * t, t)
        out = []
        for pr in range(FOX_PAIRS):
            lanes = slice(pr * LANES, (pr + 1) * LANES)
            k_aug = jnp.concatenate([k_ref[0, pl.ds(k0, t), lanes], kx_ref[0, pl.ds(k0, t), lanes]], axis=1)
            for h in range(2):
                s = _dot(k_aug, q_aug[2 * pr + h])
                if masked:
                    r = lax.broadcasted_iota(jnp.int32, (t, t), 0)
                    c = lax.broadcasted_iota(jnp.int32, (t, t), 1)
                    s = jnp.where(r <= c, s, MASK_VALUE)
                out.append(s)
        return out

    def weighted_values(j, hd, p):
        k0 = pl.multiple_of(j * t, t)
        return _dot(vt_ref[0, hd * V_ROWS:(hd + 1) * V_ROWS, pl.ds(k0, t)], p)

    def stage_logits(j, masked, slot):
        maxes = []
        for hd, s in enumerate(logits(j, masked)):
            s_ref[slot, hd] = s
            maxes.append(jnp.max(s, axis=0, keepdims=True))
        return maxes

    def stage_softmax(hd, s_max, m, s_slot, p_slot):
        m_new = jnp.maximum(m, s_max)
        p_ref[p_slot, hd] = jnp.exp2(s_ref[s_slot, hd] - m_new).astype(BF16)
        return m_new, jnp.exp2(m - m_new)

    m0 = jnp.full((1, t), MASK_VALUE, F32)
    s_max = stage_logits(i, True, 1)
    stats = [stage_softmax(hd, s_max[hd], m0, 1, 0) for hd in range(n_heads)]
    s_max = stage_logits(jnp.maximum(i - 1, 0), False, 0)
    acc_ref[...] = jnp.zeros_like(acc_ref)

    tile_id = lax.broadcasted_iota(jnp.int32, (1, LANES), 1).astype(F32)
    first_head = pl.program_id(1) * n_heads
    first_needed = []
    for hd in range(n_heads):
        base = hd * BIAS_SLOTS
        f_k = sum(flast_ref[0, base + part:base + part + 1, :].astype(F32) for part in range(3))
        f_q = sum(qx_ref[0, base + part:base + part + 1, 0:1].astype(F32) for part in range(3))
        q = qt_ref[0, hd * FOX_HEAD_DIM:(hd + 1) * FOX_HEAD_DIM, :].astype(F32)
        qn2 = jnp.max(jnp.sum(q * q, axis=0, keepdims=True), axis=1, keepdims=True)
        qk = jnp.sqrt(qn2 * kn2_ref[0, pl.ds(first_head + hd, 1), 0:1]) * NORM_SLACK
        m_min = jnp.min(stats[hd][0], axis=1, keepdims=True)
        needed = (qk + f_q - f_k >= m_min - SKIP_LOG2) & (tile_id < i.astype(F32))
        first_needed.append(jnp.min(jnp.where(needed, tile_id, i.astype(F32)), axis=1, keepdims=True))
    first_tile = jnp.min(jnp.concatenate(first_needed, axis=1), axis=1, keepdims=True)[0, 0].astype(jnp.int32)
    n_trips = i - first_tile

    def trip(par, fetch, k, carry):
        prev, heads = carry
        for hd in range(n_heads):
            pv = weighted_values(prev, hd, p_ref[par, hd])
            acc_ref[hd] = heads[hd][1] * acc_ref[hd] + pv
        stats = [stage_softmax(hd, heads[hd][2], heads[hd][0], par, 1 - par) for hd in range(n_heads)]
        s_max = stage_logits(i - 2 - k, False, 1 - par) if fetch else [h[2] for h in heads]
        return i - 1 - k, tuple(stats[hd] + (s_max[hd],) for hd in range(n_heads))

    def either_trip(fetch, k, carry):
        return lax.cond((k & 1) == 0, functools.partial(trip, 0, fetch, k), functools.partial(trip, 1, fetch, k), carry)

    def finish(p_slot, carry):
        prev, heads = carry
        o_heads = []
        for hd in range(n_heads):
            acc = heads[hd][1] * acc_ref[hd] + weighted_values(prev, hd, p_ref[p_slot, hd])
            o_heads.append(acc[0:FOX_HEAD_DIM] * (1.0 / acc[FOX_HEAD_DIM:FOX_HEAD_DIM + 1]))
        o_ref[0] = jnp.concatenate(o_heads, axis=0).T.astype(o_ref.dtype)
        return prev

    def last_trip(par, carry):
        return finish(1 - par, trip(par, False, n_trips - 1, carry))

    carry = (i, tuple(stats[hd] + (s_max[hd],) for hd in range(n_heads)))
    carry = lax.fori_loop(0, n_trips - 1, functools.partial(either_trip, True), carry)
    last_par = (n_trips - 1) & 1
    lax.cond(n_trips == 0, functools.partial(finish, 0),
             lambda c: lax.cond(last_par == 0, functools.partial(last_trip, 0), functools.partial(last_trip, 1), c),
             carry)


def _fox_attention(fqt, qx, fk, kx, fvt, kn2, flast):
    b, s, _ = fk.shape
    t = min(FOX_TILE, s)
    width = FOX_PAIRS * LANES
    n_heads = 2 * FOX_PAIRS
    groups = FOX_WIDTH // width
    whole = lambda shape, index_map: pl.BlockSpec(shape, index_map, pipeline_mode=pl.Buffered(1))
    block_bytes = (width + n_heads * BIAS_SLOTS) * t * 2 + t * width * 2
    resident_bytes = (2 * width + n_heads * V_ROWS) * s * 2 + n_heads * t * (2 * t * 6 + V_ROWS * 4)
    return pl.pallas_call(
        _fox_kernel,
        grid=(b, groups, s // t),
        in_specs=[pl.BlockSpec((1, width, t), lambda bi, g, i: (bi, g, i)),
                  pl.BlockSpec((1, n_heads * BIAS_SLOTS, t), lambda bi, g, i: (bi, g, i)),
                  whole((1, s, width), lambda bi, g, i: (bi, 0, g)),
                  whole((1, s, width), lambda bi, g, i: (bi, 0, g)),
                  whole((1, n_heads * V_ROWS, s), lambda bi, g, i: (bi, g, 0)),
                  whole((1, KN_ROWS, LANES), lambda bi, g, i: (bi, 0, 0)),
                  whole((1, n_heads * BIAS_SLOTS, flast.shape[2]), lambda bi, g, i: (bi, g, 0))],
        out_specs=pl.BlockSpec((1, t, width), lambda bi, g, i: (bi, i, g)),
        out_shape=jax.ShapeDtypeStruct((b, s, FOX_WIDTH), BF16),
        scratch_shapes=[pltpu.VMEM((2, n_heads, t, t), F32), pltpu.VMEM((2, n_heads, t, t), BF16),
                        pltpu.VMEM((n_heads, V_ROWS, t), F32)],
        compiler_params=pltpu.CompilerParams(
            dimension_semantics=("parallel", "parallel", "parallel"),
            vmem_limit_bytes=_vmem_limit(block_bytes, resident_bytes, 8 * n_heads * t * t)),
        name="fox_attention",
    )(fqt, qx, fk, kx, fvt, kn2, flast)


def _gla_kernel(q_ref, k_ref, v_ref, gates_ref, wg_ref, bg_ref, gn_ref, o_ref, state_ref):
    @pl.when(pl.program_id(1) == 0)
    def _():
        state_ref[...] = jnp.zeros_like(state_ref)

    nb, t, _ = q_ref.shape
    c = GLA_CHUNK
    half = c // 2
    row = lax.broadcasted_iota(jnp.int32, (t, t), 0)
    col = lax.broadcasted_iota(jnp.int32, (t, t), 1)
    same_chunk = _div_pow2(row, c) == _div_pow2(col, c)
    tri = jnp.where(same_chunk & (col <= row), 1.0, 0.0).astype(BF16)
    cum_all = []
    for bb in range(nb):
        gate_logits = _dot_tn(gates_ref[bb, KN_ROWS:GATE_ROWS, :].astype(BF16), wg_ref[...]) + bg_ref[...]
        cum_all.append(_ones_dot_f32(tri, _log_sigmoid(gate_logits) * (1.0 / GLA_TAU)))

    crow = lax.broadcasted_iota(jnp.int32, (c, GLA_K_WIDTH), 0)
    first = crow < half
    lane = lax.broadcasted_iota(jnp.int32, (c, GLA_K_WIDTH), 1)
    head_lanes = [_div_pow2(lane, GLA_KEY_DIM) == h for h in range(GLA_HEADS)]
    sr = lax.broadcasted_iota(jnp.int32, (GLA_HEADS * c, c), 0) & (c - 1)
    sc = lax.broadcasted_iota(jnp.int32, (GLA_HEADS * c, c), 1)
    near = (_div_pow2(sr, half) == _div_pow2(sc, half)) & (sc <= sr)
    srow = lax.broadcasted_iota(jnp.int32, state_ref.shape[1:], 0)
    slane = lax.broadcasted_iota(jnp.int32, state_ref.shape[1:], 1)
    own_head = _div_pow2(srow, GLA_VAL_DIM) == _div_pow2(slane, GLA_KEY_DIM)

    def stack_heads(x):
        return jnp.concatenate([jnp.where(m, x, 0.0) for m in head_lanes], axis=0).astype(BF16)

    for ci, bb in [(ci, bb) for ci in range(t // c) for bb in range(nb)]:
        r0 = ci * c
        cum = cum_all[bb][r0:r0 + c]
        q = q_ref[bb, r0:r0 + c, :].astype(F32)
        k = k_ref[bb, r0:r0 + c, :].astype(F32)
        v = v_ref[bb, r0:r0 + c, :]
        last = cum[c - 1:c]
        ref_far = cum[half - 1:half]
        ref_near = jnp.where(first, cum[half // 2:half // 2 + 1], cum[half + half // 2:half + half // 2 + 1])
        q_far = jnp.where(first, 0.0, q * jnp.exp(cum - ref_far))
        k_far = jnp.where(first, k * jnp.exp(ref_far - cum), 0.0)
        q_near = q * jnp.exp(cum - ref_near)
        k_near = k * jnp.exp(ref_near - cum)
        scores = (_dot_nt(stack_heads(q_far), k_far.astype(BF16))
                  + jnp.where(near, _dot_nt(stack_heads(q_near), k_near.astype(BF16)), 0.0))
        state = state_ref[bb]
        o = _dot_nt((q * jnp.exp(cum)).astype(BF16), state.astype(BF16))
        p = scores.astype(BF16)
        for h in range(GLA_HEADS):
            lo, hi = h * GLA_VAL_DIM, (h + 1) * GLA_VAL_DIM
            o_h = o[:, lo:hi] + _dot(p[h * c:(h + 1) * c], v[:, lo:hi])
            y = o_h * lax.rsqrt(jnp.mean(o_h * o_h, axis=-1, keepdims=True) + RMS_EPS) * gn_ref[:, lo:hi]
            o_ref[bb, r0:r0 + c, lo:hi] = y.astype(o_ref.dtype)
        k_tail = (k * jnp.exp(last - cum)).astype(BF16)
        update = _dot_tn(v, k_tail)
        state_ref[bb] = state * jnp.exp(last) + jnp.where(own_head, update, 0.0)


def _gla(gq, gk, gv, gates, w_gate, b_gate, g_norm):
    b, s, _ = gq.shape
    t = min(GLA_TOKENS, s)
    nb = GLA_BATCH if b % GLA_BATCH == 0 else 1
    tok = lambda width: pl.BlockSpec((nb, t, width), lambda bi, i: (bi, i, 0))
    block_bytes = nb * t * (2 * GLA_K_WIDTH * 2 + 2 * GLA_V_WIDTH * 2 + GATE_ROWS * 4)
    state_bytes = nb * GLA_V_WIDTH * GLA_K_WIDTH * 4
    return pl.pallas_call(
        _gla_kernel,
        grid=(b // nb, s // t),
        in_specs=[tok(GLA_K_WIDTH), tok(GLA_K_WIDTH), tok(GLA_V_WIDTH),
                  pl.BlockSpec((nb, GATE_ROWS, t), lambda bi, i: (bi, 0, i)),
                  _resident(w_gate.shape), _resident((1, GLA_K_WIDTH)), _resident((1, GLA_V_WIDTH))],
        out_specs=tok(GLA_V_WIDTH),
        out_shape=jax.ShapeDtypeStruct((b, s, GLA_V_WIDTH), BF16),
        scratch_shapes=[pltpu.VMEM((nb, GLA_V_WIDTH, GLA_K_WIDTH), F32)],
        compiler_params=pltpu.CompilerParams(
            dimension_semantics=("parallel", "arbitrary"),
            vmem_limit_bytes=_vmem_limit(block_bytes, state_bytes, 8 * state_bytes + 16 * t * GLA_K_WIDTH * 4)),
        name="gla",
    )(gq, gk, gv, gates, w_gate, b_gate, g_norm)


_B_POOL = 0
_B_Z = _B_POOL + POOL_WIDTH
_B_MERGE = _B_Z + FOX_WIDTH + GLA_V_WIDTH + POOL_WIDTH
_B_WIDTH = _B_MERGE + N_BRANCHES * D_MODEL


def _merge_kernel(x_ref, mod_ref, gpre_ref, gpost_ref, ofox_ref, ogla_ref, wb_ref, wpool_ref, pscale_ref,
                  wbr_ref, wout_ref, o_ref, u_ref, lvl_a_ref, lvl_b_ref):
    t = x_ref.shape[1]
    i = pl.program_id(1)
    x = x_ref[0]
    mod = mod_ref[0]
    h = _modulated_norm(x, gpre_ref[...], mod)

    @pl.when(i == 0)
    def _():
        u_ref[0:POOL_HALO, :] = jnp.zeros((POOL_HALO, POOL_WIDTH), F32)

    @pl.when(i > 0)
    def _():
        u_ref[0:POOL_HALO, :] = u_ref[t:t + POOL_HALO, :]

    u = _dot(h, wb_ref[:, _B_POOL:_B_Z])
    u_ref[POOL_HALO:POOL_HALO + t, :] = u
    count = (i * t + 1 + lax.broadcasted_iota(jnp.int32, (t, 1), 0)).astype(F32)
    levels = (u_ref, lvl_a_ref, lvl_b_ref, lvl_a_ref, lvl_b_ref)
    for n in range(1, len(POOL_WINDOWS) + 1):
        src, dst, back = levels[n - 1], levels[n], 2 ** (n - 1)
        first, rows = 8 * n, POOL_HALO + t - 8 * n
        lanes = slice((n - 1) * POOL_GROUP_DIM, POOL_WIDTH)
        dst[pl.ds(first, rows), lanes] = src[pl.ds(first, rows), lanes] + src[pl.ds(first - back, rows), lanes]
    pooled = []
    for g, w in enumerate(POOL_WINDOWS):
        lo, hi = g * POOL_GROUP_DIM, (g + 1) * POOL_GROUP_DIM
        window = levels[g + 1][pl.ds(POOL_HALO, t), lo:hi]
        mean = window / jnp.minimum(count, float(w))
        diff = (mean - u[:, lo:hi]).astype(BF16)
        pooled.append(_dot(diff, wpool_ref[g]) * pscale_ref[:, lo:hi])
    o_pool = jnp.concatenate(pooled, axis=-1)

    branches = (ofox_ref[0].astype(F32), ogla_ref[0].astype(F32), o_pool)
    merged = jnp.zeros((t, D_MODEL), F32)
    for br, o_br in enumerate(branches):
        z = _dot(h, wb_ref[:, _B_Z + br * FOX_WIDTH:_B_Z + (br + 1) * FOX_WIDTH])
        y = _dot((o_br * _silu(z)).astype(BF16), wbr_ref[br])
        m = _dot(h, wb_ref[:, _B_MERGE + br * D_MODEL:_B_MERGE + (br + 1) * D_MODEL])
        merged = merged + _sigmoid(m) * y
    out = _dot(merged.astype(BF16), wout_ref[...])
    out = out * lax.rsqrt(jnp.mean(out * out, axis=-1, keepdims=True) + RMS_EPS) * gpost_ref[...]
    gate = mod[:, 2 * D_MODEL:3 * D_MODEL]
    o_ref[0] = x + gate * out


def _merge(x, mod, g_pre, g_post, o_fox, o_gla, w_b, w_pool, pool_scale, w_br, w_out):
    b, s, d = x.shape
    t = min(MERGE_TOKENS, s)
    tok = lambda width: pl.BlockSpec((1, t, width), lambda bi, i: (bi, i, 0))
    block_bytes = 2 * t * d * 4 + 2 * t * FOX_WIDTH * 2
    weight_bytes = (w_b.size + w_pool.size + w_br.size + w_out.size) * 2
    return pl.pallas_call(
        _merge_kernel,
        grid=(b, s // t),
        in_specs=[tok(d),
                  pl.BlockSpec((1, 1, 3 * d), lambda bi, i: (bi, 0, 0)),
                  _resident((1, d)), _resident((1, d)),
                  tok(FOX_WIDTH), tok(GLA_V_WIDTH),
                  _resident(w_b.shape), _resident(w_pool.shape), _resident((1, POOL_WIDTH)),
                  _resident(w_br.shape), _resident(w_out.shape)],
        out_specs=tok(d),
        out_shape=jax.ShapeDtypeStruct((b, s, d), F32),
        scratch_shapes=[pltpu.VMEM((POOL_HALO + t, POOL_WIDTH), F32)] * 3,
        compiler_params=pltpu.CompilerParams(
            dimension_semantics=("parallel", "arbitrary"),
            vmem_limit_bytes=_vmem_limit(block_bytes, weight_bytes, 12 * t * d * 4)),
        name="merge",
    )(x, mod, g_pre, g_post, o_fox, o_gla, w_b, w_pool, pool_scale, w_br, w_out)


def _layer(x, mod, g_pre, g_post, w_in, b_forget, w_gla_gate, b_gla_gate, g_gla_norm, w_pool, pool_scale,
           w_br_fox, w_br_gla, w_br_pool, w_out):
    b, s, d = x.shape
    seg = lambda n: w_in[:, _OFF[n]:_OFF[n + 1]]
    w_a = jnp.concatenate([seg(1), seg(4) * GLA_KEY_DIM ** -0.5, seg(5), seg(6)], axis=1).astype(BF16)
    w_t = jnp.concatenate([seg(0) * (LOG2_E * FOX_HEAD_DIM ** -0.5), seg(3),
                           jnp.zeros((d, KN_ROWS - FOX_HEADS), w_in.dtype), seg(7), seg(2)], axis=1).T.astype(BF16)
    w_b = w_in[:, _OFF[8]:_OFF[13]].astype(BF16)
    b_forget_col = jnp.zeros((KN_ROWS, 1), F32).at[:FOX_HEADS, 0].set(b_forget)
    w_br = jnp.stack([w_br_fox, w_br_gla, w_br_pool]).astype(BF16)
    mod3 = mod.reshape(b, 1, 3 * d)
    g_pre = g_pre.reshape(1, d)

    head_sel = (jnp.arange(KN_ROWS)[:, None] == jnp.arange(FOX_WIDTH)[None, :] // FOX_HEAD_DIM).astype(BF16)
    fk, gq, gk, gv, fqt, fvt, gates, kn2 = _in_proj(x, mod3, g_pre, w_a, w_t, head_sel)
    kx, qx = _forget_cumsum(gates, b_forget_col)
    t_fox = min(FOX_TILE, s)
    assert s // t_fox <= LANES, "the tile-skip bound keeps one key tile per lane"
    flast = qx[:, :, t_fox - 1::t_fox]
    flast = jnp.pad(flast, ((0, 0), (0, 0), (0, -flast.shape[2] % LANES)))
    o_fox = _fox_attention(fqt, qx, fk, kx, fvt, kn2, flast)
    o_gla = _gla(gq, gk, gv, gates, w_gla_gate.astype(BF16), b_gla_gate.reshape(1, GLA_K_WIDTH),
                 g_gla_norm.reshape(1, GLA_V_WIDTH))
    return _merge(x, mod3, g_pre, g_post.reshape(1, d), o_fox, o_gla, w_b, w_pool.astype(BF16),
                  pool_scale.reshape(1, POOL_WIDTH), w_br, w_out.astype(BF16))


def kernel(x, c, w_ada, b_ada, g_pre, g_post, w_in, b_forget, w_gla_gate, b_gla_gate, g_gla_norm, w_pool,
           pool_scale, w_br_fox, w_br_gla, w_br_pool, w_out):
    mods = _adaln_mod(c, w_ada, b_ada)
    h = x
    for i in range(w_in.shape[0]):
        h = _layer(h, mods[i], g_pre[i], g_post[i], w_in[i], b_forget[i], w_gla_gate[i], b_gla_gate[i],
                   g_gla_norm[i], w_pool[i], pool_scale[i], w_br_fox[i], w_br_gla[i], w_br_pool[i], w_out[i])
    return h
```

```python
import functools

import numpy as np
import jax
import jax.numpy as jnp
from jax import lax
from jax.experimental import pallas as pl
from jax.experimental.pallas import tpu as pltpu

F32 = jnp.float32
BF16 = jnp.bfloat16

D_MODEL = 1024
FOX_HEADS = 8
FOX_HEAD_DIM = 64
FOX_WIDTH = FOX_HEADS * FOX_HEAD_DIM
GLA_HEADS = 4
GLA_KEY_DIM = 64
GLA_VAL_DIM = 128
GLA_K_WIDTH = GLA_HEADS * GLA_KEY_DIM
GLA_V_WIDTH = GLA_HEADS * GLA_VAL_DIM
GLA_GATE_RANK = 16
GLA_TAU = 16.0
POOL_WINDOWS = (2, 4, 8, 16)
POOL_GROUP_DIM = 128
POOL_WIDTH = len(POOL_WINDOWS) * POOL_GROUP_DIM
N_BRANCHES = 3
RMS_EPS = 1e-6
IN_SPLITS = (FOX_WIDTH, FOX_WIDTH, FOX_WIDTH, FOX_HEADS,
             GLA_K_WIDTH, GLA_K_WIDTH, GLA_V_WIDTH, GLA_GATE_RANK,
             POOL_WIDTH, FOX_WIDTH, GLA_V_WIDTH, POOL_WIDTH, N_BRANCHES * D_MODEL)
_OFF = np.concatenate([[0], np.cumsum(IN_SPLITS)]).tolist()

LANES = 128
MXU_WIDTH = 256
V_ROWS = FOX_HEAD_DIM + 16
POOL_HALO = 8 * len(POOL_WINDOWS)
MASK_VALUE = -1e30
LOG2_E = 1.4426950408889634
SKIP_LOG2 = 140.0
NORM_SLACK = 1.02
V7X_VMEM_BYTES = 64 * 1024 * 1024

IN_TOKENS = 512
FORGET_TOKENS = 2048
FOX_TILE = 256
FOX_PAIRS = 4
GLA_TOKENS = 256
GLA_CHUNK = 64
GLA_BATCH = 8
MERGE_TOKENS = 512


def _vmem_limit(block_bytes, resident_bytes, temp_bytes):
    need = 2 * block_bytes + resident_bytes + temp_bytes
    return int(min(max(need, 16 * 1024 * 1024), V7X_VMEM_BYTES - 8 * 1024 * 1024))


def _resident(shape):
    zeros = (0,) * len(shape)
    return pl.BlockSpec(shape, lambda *_: zeros, pipeline_mode=pl.Buffered(1))


def _div_pow2(i, n):
    shift = n.bit_length() - 1
    assert 1 << shift == n
    return jnp.right_shift(i, shift)


def _log_sigmoid(x):
    return jnp.minimum(x, 0.0) - jnp.log(1.0 + jnp.exp(-jnp.abs(x)))


def _sigmoid(x):
    return 0.5 * jnp.tanh(0.5 * x) + 0.5


def _silu(x):
    half = 0.5 * x
    return half * jnp.tanh(half) + half


def _dot(a, b):
    return jnp.dot(a, b, preferred_element_type=F32)


def _dot_nt(a, b):
    return lax.dot_general(a, b, (((1,), (1,)), ((), ())), preferred_element_type=F32)


def _dot_tn(a, b):
    return lax.dot_general(a, b, (((0,), (0,)), ((), ())), preferred_element_type=F32)


def _split3(x):
    hi = x.astype(BF16)
    r1 = x - hi.astype(F32)
    mid = r1.astype(BF16)
    lo = (r1 - mid.astype(F32)).astype(BF16)
    return hi, mid, lo


def _ones_dot_f32(ones_bf16, x):
    hi, mid, lo = _split3(x)
    return _dot(ones_bf16, hi) + _dot(ones_bf16, mid) + _dot(ones_bf16, lo)


def _modulated_norm(x, g, mod):
    shift = mod[:, 0:D_MODEL]
    scale = mod[:, D_MODEL:2 * D_MODEL]
    gain = g * (1.0 + scale)
    return (x * lax.rsqrt(jnp.mean(x * x, axis=-1, keepdims=True) + RMS_EPS) * gain + shift).astype(BF16)


def _mod_kernel(c_ref, w_ref, b_ref, o_ref):
    c = c_ref[...]
    o_ref[0] = jnp.dot(_silu(c), w_ref[0], preferred_element_type=F32,
                       precision=lax.Precision.HIGHEST) + b_ref[0]


def _adaln_mod(c, w_ada, b_ada):
    depth, d, d3 = w_ada.shape
    b = c.shape[0]
    n_col = d3 // d
    return pl.pallas_call(
        _mod_kernel,
        grid=(depth, n_col),
        in_specs=[pl.BlockSpec((b, d), lambda l, j: (0, 0)),
                  pl.BlockSpec((1, d, d), lambda l, j: (l, 0, j)),
                  pl.BlockSpec((1, 1, d), lambda l, j: (l, 0, j))],
        out_specs=pl.BlockSpec((1, b, d), lambda l, j: (l, 0, j)),
        out_shape=jax.ShapeDtypeStruct((depth, b, d3), F32),
        compiler_params=pltpu.CompilerParams(
            dimension_semantics=("parallel", "parallel"),
            vmem_limit_bytes=_vmem_limit(d * d * 4 + 2 * b * d * 4, 0, 4 * b * d * 4)),
        name="adaln_mod",
    )(c, w_ada, b_ada.reshape(depth, 1, d3))


_A_FOX_K = 0
_A_GLA_Q = _A_FOX_K + FOX_WIDTH
_A_GLA_K = _A_GLA_Q + GLA_K_WIDTH
_A_GLA_V = _A_GLA_K + GLA_K_WIDTH
_A_WIDTH = _A_GLA_V + GLA_V_WIDTH
KN_ROWS = 16
GATE_ROWS = KN_ROWS + GLA_GATE_RANK
_T_FOX_Q = 0
_T_GATES = _T_FOX_Q + FOX_WIDTH
_T_FOX_V = _T_GATES + GATE_ROWS
_T_ROWS = _T_FOX_V + FOX_WIDTH


def _in_proj_kernel(x_ref, mod_ref, g_ref, wa_ref, wt_ref, hsel_ref,
                    fk_ref, gq_ref, gk_ref, gv_ref, fqt_ref, fvt_ref, gates_ref, kn2_ref):
    h = _modulated_norm(x_ref[0], g_ref[...], mod_ref[0])
    fk = _dot(h, wa_ref[:, _A_FOX_K:_A_GLA_Q]).astype(BF16)
    fk_ref[0] = fk
    norms = _dot_nt(hsel_ref[...], fk * fk)
    tile_max = jnp.broadcast_to(jnp.max(norms, axis=1, keepdims=True), kn2_ref.shape[1:])

    @pl.when(pl.program_id(1) == 0)
    def _():
        kn2_ref[0] = tile_max

    @pl.when(pl.program_id(1) > 0)
    def _():
        kn2_ref[0] = jnp.maximum(kn2_ref[0], tile_max)

    gq_ref[0] = _dot(h, wa_ref[:, _A_GLA_Q:_A_GLA_K]).astype(BF16)
    gk_ref[0] = _dot(h, wa_ref[:, _A_GLA_K:_A_GLA_V]).astype(BF16)
    gv_ref[0] = _dot(h, wa_ref[:, _A_GLA_V:_A_WIDTH]).astype(BF16)
    tr = _dot_nt(wt_ref[...], h)
    fqt_ref[0] = tr[_T_FOX_Q:_T_GATES].astype(BF16)
    gates_ref[0] = tr[_T_GATES:_T_FOX_V]
    ones = jnp.ones((V_ROWS - FOX_HEAD_DIM, tr.shape[1]), BF16)
    for hd in range(FOX_HEADS):
        v_rows = tr[_T_FOX_V + hd * FOX_HEAD_DIM:_T_FOX_V + (hd + 1) * FOX_HEAD_DIM]
        fvt_ref[0, hd * V_ROWS:hd * V_ROWS + FOX_HEAD_DIM, :] = v_rows.astype(BF16)
        fvt_ref[0, hd * V_ROWS + FOX_HEAD_DIM:(hd + 1) * V_ROWS, :] = ones


def _in_proj(x, mod, g_pre, w_a, w_t, head_sel):
    b, s, d = x.shape
    t = min(IN_TOKENS, s)
    tok = lambda width: pl.BlockSpec((1, t, width), lambda bi, i: (bi, i, 0))
    tok_t = lambda rows: pl.BlockSpec((1, rows, t), lambda bi, i: (bi, 0, i))
    v_rows = FOX_HEADS * V_ROWS
    out_shapes = (
        jax.ShapeDtypeStruct((b, s, FOX_WIDTH), BF16),
        jax.ShapeDtypeStruct((b, s, GLA_K_WIDTH), BF16),
        jax.ShapeDtypeStruct((b, s, GLA_K_WIDTH), BF16),
        jax.ShapeDtypeStruct((b, s, GLA_V_WIDTH), BF16),
        jax.ShapeDtypeStruct((b, FOX_WIDTH, s), BF16),
        jax.ShapeDtypeStruct((b, v_rows, s), BF16),
        jax.ShapeDtypeStruct((b, GATE_ROWS, s), F32),
        jax.ShapeDtypeStruct((b, KN_ROWS, LANES), F32),
    )
    out_specs = (tok(FOX_WIDTH), tok(GLA_K_WIDTH), tok(GLA_K_WIDTH), tok(GLA_V_WIDTH), tok_t(FOX_WIDTH), tok_t(v_rows),
                 tok_t(GATE_ROWS), pl.BlockSpec((1, KN_ROWS, LANES), lambda bi, i: (bi, 0, 0)))
    n_out = _A_WIDTH + FOX_WIDTH + v_rows
    block_bytes = t * d * 4 + t * n_out * 2 + t * GATE_ROWS * 4
    weight_bytes = d * (_A_WIDTH + _T_ROWS) * 2
    return pl.pallas_call(
        _in_proj_kernel,
        grid=(b, s // t),
        in_specs=[tok(d),
                  pl.BlockSpec((1, 1, 3 * d), lambda bi, i: (bi, 0, 0)),
                  _resident((1, d)), _resident(w_a.shape), _resident(w_t.shape), _resident(head_sel.shape)],
        out_specs=out_specs,
        out_shape=out_shapes,
        compiler_params=pltpu.CompilerParams(
            dimension_semantics=("parallel", "arbitrary"),
            vmem_limit_bytes=_vmem_limit(block_bytes, weight_bytes, 6 * t * d * 4)),
        name="in_proj",
    )(x, mod, g_pre, w_a, w_t, head_sel)


BIAS_SLOTS = 16


def _bias_placement():
    pk = np.zeros((3 * KN_ROWS, FOX_WIDTH), np.float32)
    k_const = np.zeros((1, FOX_WIDTH), np.float32)
    pq = np.zeros((FOX_HEADS * BIAS_SLOTS, 3 * KN_ROWS), np.float32)
    q_const = np.zeros((FOX_HEADS * BIAS_SLOTS, 1), np.float32)
    for head in range(FOX_HEADS):
        pair, odd = divmod(head, 2)
        for part in range(3):
            pk[part * KN_ROWS + head, pair * LANES + 3 + 3 * odd + part] = -1.0
            pq[head * BIAS_SLOTS + part, part * KN_ROWS + head] = 1.0
            q_const[head * BIAS_SLOTS + 3 + 3 * odd + part, 0] = 1.0
            k_const[0, pair * LANES + part] = 1.0
    return pk, k_const, pq, q_const


def _forget_kernel(fft_ref, b_ref, pk_ref, kc_ref, pq_ref, qc_ref, kx_ref, qx_ref, carry_ref):
    @pl.when(pl.program_id(1) == 0)
    def _():
        carry_ref[...] = jnp.zeros_like(carry_ref)

    t = fft_ref.shape[2]
    w = min(MXU_WIDTH, t)
    log_f = _log_sigmoid(fft_ref[0] + b_ref[...])
    row = lax.broadcasted_iota(jnp.int32, (w, w), 0)
    col = lax.broadcasted_iota(jnp.int32, (w, w), 1)
    upper = jnp.where(row <= col, 1.0, 0.0).astype(BF16)
    split = jnp.concatenate(_split3(log_f), axis=0)
    carry = carry_ref[:, 0:1]
    blocks = []
    for j in range(t // w):
        sums = _dot(split[:, j * w:(j + 1) * w], upper)
        blocks.append(sums[0:KN_ROWS] + sums[KN_ROWS:2 * KN_ROWS] + sums[2 * KN_ROWS:3 * KN_ROWS] + carry)
        carry = blocks[-1][:, w - 1:w]
    carry_ref[...] = jnp.broadcast_to(carry, carry_ref.shape)
    cs = jnp.concatenate(blocks, axis=1)
    parts = jnp.concatenate(_split3(cs * LOG2_E), axis=0)
    qx_ref[0] = (_dot(pq_ref[...], parts) + qc_ref[...]).astype(BF16)
    kx_ref[0] = (_dot_tn(parts, pk_ref[...]) + kc_ref[...]).astype(BF16)


def _forget_cumsum(gates, b_forget_col):
    b, _, s = gates.shape
    t = min(FORGET_TOKENS, s)
    pk, k_const, pq, q_const = _bias_placement()
    q_rows = FOX_HEADS * BIAS_SLOTS
    return pl.pallas_call(
        _forget_kernel,
        grid=(b, s // t),
        in_specs=[pl.BlockSpec((1, KN_ROWS, t), lambda bi, i: (bi, 0, i)),
                  _resident((KN_ROWS, 1)), _resident(pk.shape), _resident(k_const.shape),
                  _resident(pq.shape), _resident(q_const.shape)],
        out_specs=(pl.BlockSpec((1, t, FOX_WIDTH), lambda bi, i: (bi, i, 0)),
                   pl.BlockSpec((1, q_rows, t), lambda bi, i: (bi, 0, i))),
        out_shape=(jax.ShapeDtypeStruct((b, s, FOX_WIDTH), BF16),
                   jax.ShapeDtypeStruct((b, q_rows, s), BF16)),
        scratch_shapes=[pltpu.VMEM((KN_ROWS, LANES), F32)],
        compiler_params=pltpu.CompilerParams(
            dimension_semantics=("parallel", "arbitrary"),
            vmem_limit_bytes=_vmem_limit(t * (KN_ROWS * 4 + FOX_WIDTH * 2 + q_rows * 2), pk.size * 2 + pq.size * 2,
                                         8 * t * FOX_WIDTH * 4)),
        name="forget_cumsum",
    )(gates, b_forget_col, jnp.asarray(pk, BF16), jnp.asarray(k_const), jnp.asarray(pq, BF16), jnp.asarray(q_const))


def _fox_kernel(qt_ref, qx_ref, k_ref, kx_ref, vt_ref, kn2_ref, flast_ref, o_ref, s_ref, p_ref, acc_ref):
    t = qt_ref.shape[2]
    i = pl.program_id(2)
    n_heads = 2 * FOX_PAIRS
    row = lax.broadcasted_iota(jnp.int32, (LANES, t), 0)
    low = row < FOX_HEAD_DIM
    pad = jnp.zeros((LANES - BIAS_SLOTS, t), BF16)
    q_aug = []
    for hd in range(n_heads):
        qt = qt_ref[0, (hd // 2) * LANES:(hd // 2 + 1) * LANES, :]
        own = jnp.where(low, qt, jnp.zeros_like(qt)) if hd % 2 == 0 else jnp.where(low, jnp.zeros_like(qt), qt)
        q_aug.append(jnp.concatenate([own, qx_ref[0, hd * BIAS_SLOTS:(hd + 1) * BIAS_SLOTS, :], pad], axis=0))

    def logits(j, masked):
        k0 = pl.multiple_of(j * t, t)
        out = []
        for pr in range(FOX_PAIRS):
            lanes = slice(pr * LANES, (pr + 1) * LANES)
            k_aug = jnp.concatenate([k_ref[0, pl.ds(k0, t), lanes], kx_ref[0, pl.ds(k0, t), lanes]], axis=1)
            for h in range(2):
                s = _dot(k_aug, q_aug[2 * pr + h])
                if masked:
                    r = lax.broadcasted_iota(jnp.int32, (t, t), 0)
                    c = lax.broadcasted_iota(jnp.int32, (t, t), 1)
                    s = jnp.where(r <= c, s, MASK_VALUE)
                out.append(s)
        return out

    def weighted_values(j, hd, p):
        k0 = pl.multiple_of(j * t, t)
        return _dot(vt_ref[0, hd * V_ROWS:(hd + 1) * V_ROWS, pl.ds(k0, t)], p)

    def stage_logits(j, masked, slot):
        maxes = []
        for hd, s in enumerate(logits(j, masked)):
            s_ref[slot, hd] = s
            maxes.append(jnp.max(s, axis=0, keepdims=True))
        return maxes

    def stage_softmax(hd, s_max, m, s_slot, p_slot):
        m_new = jnp.maximum(m, s_max)
        p_ref[p_slot, hd] = jnp.exp2(s_ref[s_slot, hd] - m_new).astype(BF16)
        return m_new, jnp.exp2(m - m_new)

    m0 = jnp.full((1, t), MASK_VALUE, F32)
    s_max = stage_logits(i, True, 1)
    stats = [stage_softmax(hd, s_max[hd], m0, 1, 0) for hd in range(n_heads)]
    s_max = stage_logits(jnp.maximum(i - 1, 0), False, 0)
    acc_ref[...] = jnp.zeros_like(acc_ref)

    tile_id = lax.broadcasted_iota(jnp.int32, (1, LANES), 1).astype(F32)
    first_head = pl.program_id(1) * n_heads
    first_needed = []
    for hd in range(n_heads):
        base = hd * BIAS_SLOTS
        f_k = sum(flast_ref[0, base + part:base + part + 1, :].astype(F32) for part in range(3))
        f_q = sum(qx_ref[0, base + part:base + part + 1, 0:1].astype(F32) for part in range(3))
        q = qt_ref[0, hd * FOX_HEAD_DIM:(hd + 1) * FOX_HEAD_DIM, :].astype(F32)
        qn2 = jnp.max(jnp.sum(q * q, axis=0, keepdims=True), axis=1, keepdims=True)
        qk = jnp.sqrt(qn2 * kn2_ref[0, pl.ds(first_head + hd, 1), 0:1]) * NORM_SLACK
        m_min = jnp.min(stats[hd][0], axis=1, keepdims=True)
        needed = (qk + f_q - f_k >= m_min - SKIP_LOG2) & (tile_id < i.astype(F32))
        first_needed.append(jnp.min(jnp.where(needed, tile_id, i.astype(F32)), axis=1, keepdims=True))
    first_tile = jnp.min(jnp.concatenate(first_needed, axis=1), axis=1, keepdims=True)[0, 0].astype(jnp.int32)
    n_trips = i - first_tile

    def trip(par, fetch, k, carry):
        prev, heads = carry
        for hd in range(n_heads):
            pv = weighted_values(prev, hd, p_ref[par, hd])
            acc_ref[hd] = heads[hd][1] * acc_ref[hd] + pv
        stats = [stage_softmax(hd, heads[hd][2], heads[hd][0], par, 1 - par) for hd in range(n_heads)]
        s_max = stage_logits(i - 2 - k, False, 1 - par) if fetch else [h[2] for h in heads]
        return i - 1 - k, tuple(stats[hd] + (s_max[hd],) for hd in range(n_heads))

    def either_trip(fetch, k, carry):
        return lax.cond((k & 1) == 0, functools.partial(trip, 0, fetch, k), functools.partial(trip, 1, fetch, k), carry)

    def finish(p_slot, carry):
        prev, heads = carry
        o_heads = []
        for hd in range(n_heads):
            acc = heads[hd][1] * acc_ref[hd] + weighted_values(prev, hd, p_ref[p_slot, hd])
            o_heads.append(acc[0:FOX_HEAD_DIM] * (1.0 / acc[FOX_HEAD_DIM:FOX_HEAD_DIM + 1]))
        o_ref[0] = jnp.concatenate(o_heads, axis=0).T.astype(o_ref.dtype)
        return prev

    def last_trip(par, carry):
        return finish(1 - par, trip(par, False, n_trips - 1, carry))

    carry = (i, tuple(stats[hd] + (s_max[hd],) for hd in range(n_heads)))
    carry = lax.fori_loop(0, n_trips - 1, functools.partial(either_trip, True), carry)
    last_par = (n_trips - 1) & 1
    lax.cond(n_trips == 0, functools.partial(finish, 0),
             lambda c: lax.cond(last_par == 0, functools.partial(last_trip, 0), functools.partial(last_trip, 1), c),
             carry)


def _fox_attention(fqt, qx, fk, kx, fvt, kn2, flast):
    b, s, _ = fk.shape
    t = min(FOX_TILE, s)
    width = FOX_PAIRS * LANES
    n_heads = 2 * FOX_PAIRS
    groups = FOX_WIDTH // width
    whole = lambda shape, index_map: pl.BlockSpec(shape, index_map, pipeline_mode=pl.Buffered(1))
    block_bytes = (width + n_heads * BIAS_SLOTS) * t * 2 + t * width * 2
    resident_bytes = (2 * width + n_heads * V_ROWS) * s * 2 + n_heads * t * (2 * t * 6 + V_ROWS * 4)
    return pl.pallas_call(
        _fox_kernel,
        grid=(b, groups, s // t),
        in_specs=[pl.BlockSpec((1, width, t), lambda bi, g, i: (bi, g, i)),
                  pl.BlockSpec((1, n_heads * BIAS_SLOTS, t), lambda bi, g, i: (bi, g, i)),
                  whole((1, s, width), lambda bi, g, i: (bi, 0, g)),
                  whole((1, s, width), lambda bi, g, i: (bi, 0, g)),
                  whole((1, n_heads * V_ROWS, s), lambda bi, g, i: (bi, g, 0)),
                  whole((1, KN_ROWS, LANES), lambda bi, g, i: (bi, 0, 0)),
                  whole((1, n_heads * BIAS_SLOTS, flast.shape[2]), lambda bi, g, i: (bi, g, 0))],
        out_specs=pl.BlockSpec((1, t, width), lambda bi, g, i: (bi, i, g)),
        out_shape=jax.ShapeDtypeStruct((b, s, FOX_WIDTH), BF16),
        scratch_shapes=[pltpu.VMEM((2, n_heads, t, t), F32), pltpu.VMEM((2, n_heads, t, t), BF16),
                        pltpu.VMEM((n_heads, V_ROWS, t), F32)],
        compiler_params=pltpu.CompilerParams(
            dimension_semantics=("parallel", "parallel", "parallel"),
            vmem_limit_bytes=_vmem_limit(block_bytes, resident_bytes, 8 * n_heads * t * t)),
        name="fox_attention",
    )(fqt, qx, fk, kx, fvt, kn2, flast)


def _gla_kernel(q_ref, k_ref, v_ref, gates_ref, wg_ref, bg_ref, gn_ref, o_ref, state_ref):
    @pl.when(pl.program_id(1) == 0)
    def _():
        state_ref[...] = jnp.zeros_like(state_ref)

    nb, t, _ = q_ref.shape
    c = GLA_CHUNK
    half = c // 2
    row = lax.broadcasted_iota(jnp.int32, (t, t), 0)
    col = lax.broadcasted_iota(jnp.int32, (t, t), 1)
    same_chunk = _div_pow2(row, c) == _div_pow2(col, c)
    tri = jnp.where(same_chunk & (col <= row), 1.0, 0.0).astype(BF16)
    cum_all = []
    for bb in range(nb):
        gate_logits = _dot_tn(gates_ref[bb, KN_ROWS:GATE_ROWS, :].astype(BF16), wg_ref[...]) + bg_ref[...]
        cum_all.append(_ones_dot_f32(tri, _log_sigmoid(gate_logits) * (1.0 / GLA_TAU)))

    crow = lax.broadcasted_iota(jnp.int32, (c, GLA_K_WIDTH), 0)
    first = crow < half
    lane = lax.broadcasted_iota(jnp.int32, (c, GLA_K_WIDTH), 1)
    head_lanes = [_div_pow2(lane, GLA_KEY_DIM) == h for h in range(GLA_HEADS)]
    sr = lax.broadcasted_iota(jnp.int32, (GLA_HEADS * c, c), 0) & (c - 1)
    sc = lax.broadcasted_iota(jnp.int32, (GLA_HEADS * c, c), 1)
    near = (_div_pow2(sr, half) == _div_pow2(sc, half)) & (sc <= sr)
    srow = lax.broadcasted_iota(jnp.int32, state_ref.shape[1:], 0)
    slane = lax.broadcasted_iota(jnp.int32, state_ref.shape[1:], 1)
    own_head = _div_pow2(srow, GLA_VAL_DIM) == _div_pow2(slane, GLA_KEY_DIM)

    def stack_heads(x):
        return jnp.concatenate([jnp.where(m, x, 0.0) for m in head_lanes], axis=0).astype(BF16)

    for ci, bb in [(ci, bb) for ci in range(t // c) for bb in range(nb)]:
        r0 = ci * c
        cum = cum_all[bb][r0:r0 + c]
        q = q_ref[bb, r0:r0 + c, :].astype(F32)
        k = k_ref[bb, r0:r0 + c, :].astype(F32)
        v = v_ref[bb, r0:r0 + c, :]
        last = cum[c - 1:c]
        ref_far = cum[half - 1:half]
        ref_near = jnp.where(first, cum[half // 2:half // 2 + 1], cum[half + half // 2:half + half // 2 + 1])
        q_far = jnp.where(first, 0.0, q * jnp.exp(cum - ref_far))
        k_far = jnp.where(first, k * jnp.exp(ref_far - cum), 0.0)
        q_near = q * jnp.exp(cum - ref_near)
        k_near = k * jnp.exp(ref_near - cum)
        scores = (_dot_nt(stack_heads(q_far), k_far.astype(BF16))
                  + jnp.where(near, _dot_nt(stack_heads(q_near), k_near.astype(BF16)), 0.0))
        state = state_ref[bb]
        o = _dot_nt((q * jnp.exp(cum)).astype(BF16), state.astype(BF16))
        p = scores.astype(BF16)
        for h in range(GLA_HEADS):
            lo, hi = h * GLA_VAL_DIM, (h + 1) * GLA_VAL_DIM
            o_h = o[:, lo:hi] + _dot(p[h * c:(h + 1) * c], v[:, lo:hi])
            y = o_h * lax.rsqrt(jnp.mean(o_h * o_h, axis=-1, keepdims=True) + RMS_EPS) * gn_ref[:, lo:hi]
            o_ref[bb, r0:r0 + c, lo:hi] = y.astype(o_ref.dtype)
        k_tail = (k * jnp.exp(last - cum)).astype(BF16)
        update = _dot_tn(v, k_tail)
        state_ref[bb] = state * jnp.exp(last) + jnp.where(own_head, update, 0.0)


def _gla(gq, gk, gv, gates, w_gate, b_gate, g_norm):
    b, s, _ = gq.shape
    t = min(GLA_TOKENS, s)
    nb = GLA_BATCH if b % GLA_BATCH == 0 else 1
    tok = lambda width: pl.BlockSpec((nb, t, width), lambda bi, i: (bi, i, 0))
    block_bytes = nb * t * (2 * GLA_K_WIDTH * 2 + 2 * GLA_V_WIDTH * 2 + GATE_ROWS * 4)
    state_bytes = nb * GLA_V_WIDTH * GLA_K_WIDTH * 4
    return pl.pallas_call(
        _gla_kernel,
        grid=(b // nb, s // t),
        in_specs=[tok(GLA_K_WIDTH), tok(GLA_K_WIDTH), tok(GLA_V_WIDTH),
                  pl.BlockSpec((nb, GATE_ROWS, t), lambda bi, i: (bi, 0, i)),
                  _resident(w_gate.shape), _resident((1, GLA_K_WIDTH)), _resident((1, GLA_V_WIDTH))],
        out_specs=tok(GLA_V_WIDTH),
        out_shape=jax.ShapeDtypeStruct((b, s, GLA_V_WIDTH), BF16),
        scratch_shapes=[pltpu.VMEM((nb, GLA_V_WIDTH, GLA_K_WIDTH), F32)],
        compiler_params=pltpu.CompilerParams(
            dimension_semantics=("parallel", "arbitrary"),
            vmem_limit_bytes=_vmem_limit(block_bytes, state_bytes, 8 * state_bytes + 16 * t * GLA_K_WIDTH * 4)),
        name="gla",
    )(gq, gk, gv, gates, w_gate, b_gate, g_norm)


_B_POOL = 0
_B_Z = _B_POOL + POOL_WIDTH
_B_MERGE = _B_Z + FOX_WIDTH + GLA_V_WIDTH + POOL_WIDTH
_B_WIDTH = _B_MERGE + N_BRANCHES * D_MODEL


def _merge_kernel(x_ref, mod_ref, gpre_ref, gpost_ref, ofox_ref, ogla_ref, wb_ref, wpool_ref, pscale_ref,
                  wbr_ref, wout_ref, o_ref, u_ref, lvl_a_ref, lvl_b_ref):
    t = x_ref.shape[1]
    i = pl.program_id(1)
    x = x_ref[0]
    mod = mod_ref[0]
    h = _modulated_norm(x, gpre_ref[...], mod)

    @pl.when(i == 0)
    def _():
        u_ref[0:POOL_HALO, :] = jnp.zeros((POOL_HALO, POOL_WIDTH), F32)

    @pl.when(i > 0)
    def _():
        u_ref[0:POOL_HALO, :] = u_ref[t:t + POOL_HALO, :]

    u = _dot(h, wb_ref[:, _B_POOL:_B_Z])
    u_ref[POOL_HALO:POOL_HALO + t, :] = u
    count = (i * t + 1 + lax.broadcasted_iota(jnp.int32, (t, 1), 0)).astype(F32)
    levels = (u_ref, lvl_a_ref, lvl_b_ref, lvl_a_ref, lvl_b_ref)
    for n in range(1, len(POOL_WINDOWS) + 1):
        src, dst, back = levels[n - 1], levels[n], 2 ** (n - 1)
        first, rows = 8 * n, POOL_HALO + t - 8 * n
        lanes = slice((n - 1) * POOL_GROUP_DIM, POOL_WIDTH)
        dst[pl.ds(first, rows), lanes] = src[pl.ds(first, rows), lanes] + src[pl.ds(first - back, rows), lanes]
    pooled = []
    for g, w in enumerate(POOL_WINDOWS):
        lo, hi = g * POOL_GROUP_DIM, (g + 1) * POOL_GROUP_DIM
        window = levels[g + 1][pl.ds(POOL_HALO, t), lo:hi]
        mean = window * (1.0 / jnp.minimum(count, float(w)))
        diff = (mean - u[:, lo:hi]).astype(BF16)
        pooled.append(_dot(diff, wpool_ref[g]) * pscale_ref[:, lo:hi])
    o_pool = jnp.concatenate(pooled, axis=-1)

    branches = (ofox_ref[0].astype(F32), ogla_ref[0].astype(F32), o_pool)
    merged = jnp.zeros((t, D_MODEL), F32)
    for br, o_br in enumerate(branches):
        z = _dot(h, wb_ref[:, _B_Z + br * FOX_WIDTH:_B_Z + (br + 1) * FOX_WIDTH])
        y = _dot((o_br * _silu(z)).astype(BF16), wbr_ref[br])
        m = _dot(h, wb_ref[:, _B_MERGE + br * D_MODEL:_B_MERGE + (br + 1) * D_MODEL])
        merged = merged + _sigmoid(m) * y
    out = _dot(merged.astype(BF16), wout_ref[...])
    gated_gain = mod[:, 2 * D_MODEL:3 * D_MODEL] * gpost_ref[...]
    o_ref[0] = x + out * lax.rsqrt(jnp.mean(out * out, axis=-1, keepdims=True) + RMS_EPS) * gated_gain


def _merge(x, mod, g_pre, g_post, o_fox, o_gla, w_b, w_pool, pool_scale, w_br, w_out):
    b, s, d = x.shape
    t = min(MERGE_TOKENS, s)
    tok = lambda width: pl.BlockSpec((1, t, width), lambda bi, i: (bi, i, 0))
    block_bytes = 2 * t * d * 4 + 2 * t * FOX_WIDTH * 2
    weight_bytes = (w_b.size + w_pool.size + w_br.size + w_out.size) * 2
    return pl.pallas_call(
        _merge_kernel,
        grid=(b, s // t),
        in_specs=[tok(d),
                  pl.BlockSpec((1, 1, 3 * d), lambda bi, i: (bi, 0, 0)),
                  _resident((1, d)), _resident((1, d)),
                  tok(FOX_WIDTH), tok(GLA_V_WIDTH),
                  _resident(w_b.shape), _resident(w_pool.shape), _resident((1, POOL_WIDTH)),
                  _resident(w_br.shape), _resident(w_out.shape)],
        out_specs=tok(d),
        out_shape=jax.ShapeDtypeStruct((b, s, d), F32),
        scratch_shapes=[pltpu.VMEM((POOL_HALO + t, POOL_WIDTH), F32)] * 3,
        compiler_params=pltpu.CompilerParams(
            dimension_semantics=("parallel", "arbitrary"),
            vmem_limit_bytes=_vmem_limit(block_bytes, weight_bytes, 12 * t * d * 4)),
        name="merge",
    )(x, mod, g_pre, g_post, o_fox, o_gla, w_b, w_pool, pool_scale, w_br, w_out)


def _layer(x, mod, g_pre, g_post, w_in, b_forget, w_gla_gate, b_gla_gate, g_gla_norm, w_pool, pool_scale,
           w_br_fox, w_br_gla, w_br_pool, w_out):
    b, s, d = x.shape
    seg = lambda n: w_in[:, _OFF[n]:_OFF[n + 1]]
    w_a = jnp.concatenate([seg(1), seg(4) * GLA_KEY_DIM ** -0.5, seg(5), seg(6)], axis=1).astype(BF16)
    w_t = jnp.concatenate([seg(0) * (LOG2_E * FOX_HEAD_DIM ** -0.5), seg(3),
                           jnp.zeros((d, KN_ROWS - FOX_HEADS), w_in.dtype), seg(7), seg(2)], axis=1).T.astype(BF16)
    w_b = w_in[:, _OFF[8]:_OFF[13]].astype(BF16)
    b_forget_col = jnp.zeros((KN_ROWS, 1), F32).at[:FOX_HEADS, 0].set(b_forget)
    w_br = jnp.stack([w_br_fox, w_br_gla, w_br_pool]).astype(BF16)
    mod3 = mod.reshape(b, 1, 3 * d)
    g_pre = g_pre.reshape(1, d)

    head_sel = (jnp.arange(KN_ROWS)[:, None] == jnp.arange(FOX_WIDTH)[None, :] // FOX_HEAD_DIM).astype(BF16)
    fk, gq, gk, gv, fqt, fvt, gates, kn2 = _in_proj(x, mod3, g_pre, w_a, w_t, head_sel)
    kx, qx = _forget_cumsum(gates, b_forget_col)
    t_fox = min(FOX_TILE, s)
    assert s // t_fox <= LANES, "the tile-skip bound keeps one key tile per lane"
    flast = qx[:, :, t_fox - 1::t_fox]
    flast = jnp.pad(flast, ((0, 0), (0, 0), (0, -flast.shape[2] % LANES)))
    o_fox = _fox_attention(fqt, qx, fk, kx, fvt, kn2, flast)
    o_gla = _gla(gq, gk, gv, gates, w_gla_gate.astype(BF16), b_gla_gate.reshape(1, GLA_K_WIDTH),
                 g_gla_norm.reshape(1, GLA_V_WIDTH))
    return _merge(x, mod3, g_pre, g_post.reshape(1, d), o_fox, o_gla, w_b, w_pool.astype(BF16),
                  pool_scale.reshape(1, POOL_WIDTH), w_br, w_out.astype(BF16))


def kernel(x, c, w_ada, b_ada, g_pre, g_post, w_in, b_forget, w_gla_gate, b_gla_gate, g_gla_norm, w_pool,
           pool_scale, w_br_fox, w_br_gla, w_br_pool, w_out):
    mods = _adaln_mod(c, w_ada, b_ada)
    h = x
    for i in range(w_in.shape[0]):
        h = _layer(h, mods[i], g_pre[i], g_post[i], w_in[i], b_forget[i], w_gla_gate[i], b_gla_gate[i],
                   g_gla_norm[i], w_pool[i], pool_scale[i], w_br_fox[i], w_br_gla[i], w_br_pool[i], w_out[i])
    return h
```

```python
import functools

import numpy as np
import jax
import jax.numpy as jnp
from jax import lax
from jax.experimental import pallas as pl
from jax.experimental.pallas import tpu as pltpu

F32 = jnp.float32
BF16 = jnp.bfloat16

D_MODEL = 1024
FOX_HEADS = 8
FOX_HEAD_DIM = 64
FOX_WIDTH = FOX_HEADS * FOX_HEAD_DIM
GLA_HEADS = 4
GLA_KEY_DIM = 64
GLA_VAL_DIM = 128
GLA_K_WIDTH = GLA_HEADS * GLA_KEY_DIM
GLA_V_WIDTH = GLA_HEADS * GLA_VAL_DIM
GLA_GATE_RANK = 16
GLA_TAU = 16.0
POOL_WINDOWS = (2, 4, 8, 16)
POOL_GROUP_DIM = 128
POOL_WIDTH = len(POOL_WINDOWS) * POOL_GROUP_DIM
N_BRANCHES = 3
RMS_EPS = 1e-6
IN_SPLITS = (FOX_WIDTH, FOX_WIDTH, FOX_WIDTH, FOX_HEADS,
             GLA_K_WIDTH, GLA_K_WIDTH, GLA_V_WIDTH, GLA_GATE_RANK,
             POOL_WIDTH, FOX_WIDTH, GLA_V_WIDTH, POOL_WIDTH, N_BRANCHES * D_MODEL)
_OFF = np.concatenate([[0], np.cumsum(IN_SPLITS)]).tolist()

LANES = 128
MXU_WIDTH = 256
V_ROWS = FOX_HEAD_DIM + 16
POOL_HALO = 8 * len(POOL_WINDOWS)
MASK_VALUE = -1e30
LOG2_E = 1.4426950408889634
SKIP_LOG2 = 136.0
NORM_SLACK = 1.02
V7X_VMEM_BYTES = 64 * 1024 * 1024

IN_TOKENS = 512
FORGET_TOKENS = 2048
FOX_TILE = 256
FOX_PAIRS = 4
GLA_TOKENS = 256
GLA_CHUNK = 64
GLA_BATCH = 8
MERGE_TOKENS = 512


def _vmem_limit(block_bytes, resident_bytes, temp_bytes):
    need = 2 * block_bytes + resident_bytes + temp_bytes
    return int(min(max(need, 16 * 1024 * 1024), V7X_VMEM_BYTES - 8 * 1024 * 1024))


def _resident(shape):
    zeros = (0,) * len(shape)
    return pl.BlockSpec(shape, lambda *_: zeros, pipeline_mode=pl.Buffered(1))


def _div_pow2(i, n):
    shift = n.bit_length() - 1
    assert 1 << shift == n
    return jnp.right_shift(i, shift)


def _log_sigmoid(x):
    return jnp.minimum(x, 0.0) - jnp.log(1.0 + jnp.exp(-jnp.abs(x)))


def _sigmoid(x):
    return 0.5 * jnp.tanh(0.5 * x) + 0.5


def _silu(x):
    half = 0.5 * x
    return half * jnp.tanh(half) + half


def _dot(a, b):
    return jnp.dot(a, b, preferred_element_type=F32)


def _dot_nt(a, b):
    return lax.dot_general(a, b, (((1,), (1,)), ((), ())), preferred_element_type=F32)


def _dot_tn(a, b):
    return lax.dot_general(a, b, (((0,), (0,)), ((), ())), preferred_element_type=F32)


def _split3(x):
    hi = x.astype(BF16)
    r1 = x - hi.astype(F32)
    mid = r1.astype(BF16)
    lo = (r1 - mid.astype(F32)).astype(BF16)
    return hi, mid, lo


def _ones_dot_f32(ones_bf16, x):
    hi, mid, lo = _split3(x)
    return _dot(ones_bf16, hi) + _dot(ones_bf16, mid) + _dot(ones_bf16, lo)


def _modulated_norm(x, g, mod):
    shift = mod[:, 0:D_MODEL]
    scale = mod[:, D_MODEL:2 * D_MODEL]
    gain = g * (1.0 + scale)
    return (x * lax.rsqrt(jnp.mean(x * x, axis=-1, keepdims=True) + RMS_EPS) * gain + shift).astype(BF16)


def _mod_kernel(c_ref, w_ref, b_ref, o_ref):
    c = c_ref[...]
    o_ref[0] = jnp.dot(_silu(c), w_ref[0], preferred_element_type=F32,
                       precision=lax.Precision.HIGHEST) + b_ref[0]


def _adaln_mod(c, w_ada, b_ada):
    depth, d, d3 = w_ada.shape
    b = c.shape[0]
    n_col = d3 // d
    return pl.pallas_call(
        _mod_kernel,
        grid=(depth, n_col),
        in_specs=[pl.BlockSpec((b, d), lambda l, j: (0, 0)),
                  pl.BlockSpec((1, d, d), lambda l, j: (l, 0, j)),
                  pl.BlockSpec((1, 1, d), lambda l, j: (l, 0, j))],
        out_specs=pl.BlockSpec((1, b, d), lambda l, j: (l, 0, j)),
        out_shape=jax.ShapeDtypeStruct((depth, b, d3), F32),
        compiler_params=pltpu.CompilerParams(
            dimension_semantics=("parallel", "parallel"),
            vmem_limit_bytes=_vmem_limit(d * d * 4 + 2 * b * d * 4, 0, 4 * b * d * 4)),
        name="adaln_mod",
    )(c, w_ada, b_ada.reshape(depth, 1, d3))


_A_FOX_K = 0
_A_GLA_Q = _A_FOX_K + FOX_WIDTH
_A_GLA_K = _A_GLA_Q + GLA_K_WIDTH
_A_GLA_V = _A_GLA_K + GLA_K_WIDTH
_A_WIDTH = _A_GLA_V + GLA_V_WIDTH
KN_ROWS = 16
GATE_ROWS = KN_ROWS + GLA_GATE_RANK
_T_FOX_Q = 0
_T_GATES = _T_FOX_Q + FOX_WIDTH
_T_FOX_V = _T_GATES + GATE_ROWS
_T_ROWS = _T_FOX_V + FOX_WIDTH


def _in_proj_kernel(x_ref, mod_ref, g_ref, wa_ref, wt_ref, hsel_ref,
                    fk_ref, gq_ref, gk_ref, gv_ref, fqt_ref, fvt_ref, gates_ref, kn2_ref):
    h = _modulated_norm(x_ref[0], g_ref[...], mod_ref[0])
    fk = _dot(h, wa_ref[:, _A_FOX_K:_A_GLA_Q]).astype(BF16)
    fk_ref[0] = fk
    norms = _dot_nt(hsel_ref[...], fk * fk)
    tile_max = jnp.broadcast_to(jnp.max(norms, axis=1, keepdims=True), kn2_ref.shape[1:])

    @pl.when(pl.program_id(1) == 0)
    def _():
        kn2_ref[0] = tile_max

    @pl.when(pl.program_id(1) > 0)
    def _():
        kn2_ref[0] = jnp.maximum(kn2_ref[0], tile_max)

    gq_ref[0] = _dot(h, wa_ref[:, _A_GLA_Q:_A_GLA_K]).astype(BF16)
    gk_ref[0] = _dot(h, wa_ref[:, _A_GLA_K:_A_GLA_V]).astype(BF16)
    gv_ref[0] = _dot(h, wa_ref[:, _A_GLA_V:_A_WIDTH]).astype(BF16)
    tr = _dot_nt(wt_ref[...], h)
    fqt_ref[0] = tr[_T_FOX_Q:_T_GATES].astype(BF16)
    gates_ref[0] = tr[_T_GATES:_T_FOX_V]
    ones = jnp.ones((V_ROWS - FOX_HEAD_DIM, tr.shape[1]), BF16)
    for hd in range(FOX_HEADS):
        v_rows = tr[_T_FOX_V + hd * FOX_HEAD_DIM:_T_FOX_V + (hd + 1) * FOX_HEAD_DIM]
        fvt_ref[0, hd * V_ROWS:hd * V_ROWS + FOX_HEAD_DIM, :] = v_rows.astype(BF16)
        fvt_ref[0, hd * V_ROWS + FOX_HEAD_DIM:(hd + 1) * V_ROWS, :] = ones


def _in_proj(x, mod, g_pre, w_a, w_t, head_sel):
    b, s, d = x.shape
    t = min(IN_TOKENS, s)
    tok = lambda width: pl.BlockSpec((1, t, width), lambda bi, i: (bi, i, 0))
    tok_t = lambda rows: pl.BlockSpec((1, rows, t), lambda bi, i: (bi, 0, i))
    v_rows = FOX_HEADS * V_ROWS
    out_shapes = (
        jax.ShapeDtypeStruct((b, s, FOX_WIDTH), BF16),
        jax.ShapeDtypeStruct((b, s, GLA_K_WIDTH), BF16),
        jax.ShapeDtypeStruct((b, s, GLA_K_WIDTH), BF16),
        jax.ShapeDtypeStruct((b, s, GLA_V_WIDTH), BF16),
        jax.ShapeDtypeStruct((b, FOX_WIDTH, s), BF16),
        jax.ShapeDtypeStruct((b, v_rows, s), BF16),
        jax.ShapeDtypeStruct((b, GATE_ROWS, s), F32),
        jax.ShapeDtypeStruct((b, KN_ROWS, LANES), F32),
    )
    out_specs = (tok(FOX_WIDTH), tok(GLA_K_WIDTH), tok(GLA_K_WIDTH), tok(GLA_V_WIDTH), tok_t(FOX_WIDTH), tok_t(v_rows),
                 tok_t(GATE_ROWS), pl.BlockSpec((1, KN_ROWS, LANES), lambda bi, i: (bi, 0, 0)))
    n_out = _A_WIDTH + FOX_WIDTH + v_rows
    block_bytes = t * d * 4 + t * n_out * 2 + t * GATE_ROWS * 4
    weight_bytes = d * (_A_WIDTH + _T_ROWS) * 2
    return pl.pallas_call(
        _in_proj_kernel,
        grid=(b, s // t),
        in_specs=[tok(d),
                  pl.BlockSpec((1, 1, 3 * d), lambda bi, i: (bi, 0, 0)),
                  _resident((1, d)), _resident(w_a.shape), _resident(w_t.shape), _resident(head_sel.shape)],
        out_specs=out_specs,
        out_shape=out_shapes,
        compiler_params=pltpu.CompilerParams(
            dimension_semantics=("parallel", "arbitrary"),
            vmem_limit_bytes=_vmem_limit(block_bytes, weight_bytes, 6 * t * d * 4)),
        name="in_proj",
    )(x, mod, g_pre, w_a, w_t, head_sel)


BIAS_SLOTS = 16


def _bias_placement():
    pk = np.zeros((3 * KN_ROWS, FOX_WIDTH), np.float32)
    k_const = np.zeros((1, FOX_WIDTH), np.float32)
    pq = np.zeros((FOX_HEADS * BIAS_SLOTS, 3 * KN_ROWS), np.float32)
    q_const = np.zeros((FOX_HEADS * BIAS_SLOTS, 1), np.float32)
    for head in range(FOX_HEADS):
        pair, odd = divmod(head, 2)
        for part in range(3):
            pk[part * KN_ROWS + head, pair * LANES + 3 + 3 * odd + part] = -1.0
            pq[head * BIAS_SLOTS + part, part * KN_ROWS + head] = 1.0
            q_const[head * BIAS_SLOTS + 3 + 3 * odd + part, 0] = 1.0
            k_const[0, pair * LANES + part] = 1.0
    return pk, k_const, pq, q_const


def _forget_kernel(fft_ref, b_ref, pk_ref, kc_ref, pq_ref, qc_ref, kx_ref, qx_ref, carry_ref):
    @pl.when(pl.program_id(1) == 0)
    def _():
        carry_ref[...] = jnp.zeros_like(carry_ref)

    t = fft_ref.shape[2]
    w = min(MXU_WIDTH, t)
    log_f = _log_sigmoid(fft_ref[0] + b_ref[...])
    row = lax.broadcasted_iota(jnp.int32, (w, w), 0)
    col = lax.broadcasted_iota(jnp.int32, (w, w), 1)
    upper = jnp.where(row <= col, 1.0, 0.0).astype(BF16)
    split = jnp.concatenate(_split3(log_f), axis=0)
    carry = carry_ref[:, 0:1]
    blocks = []
    for j in range(t // w):
        sums = _dot(split[:, j * w:(j + 1) * w], upper)
        blocks.append(sums[0:KN_ROWS] + sums[KN_ROWS:2 * KN_ROWS] + sums[2 * KN_ROWS:3 * KN_ROWS] + carry)
        carry = blocks[-1][:, w - 1:w]
    carry_ref[...] = jnp.broadcast_to(carry, carry_ref.shape)
    cs = jnp.concatenate(blocks, axis=1)
    parts = jnp.concatenate(_split3(cs * LOG2_E), axis=0)
    qx_ref[0] = (_dot(pq_ref[...], parts) + qc_ref[...]).astype(BF16)
    kx_ref[0] = (_dot_tn(parts, pk_ref[...]) + kc_ref[...]).astype(BF16)


def _forget_cumsum(gates, b_forget_col):
    b, _, s = gates.shape
    t = min(FORGET_TOKENS, s)
    pk, k_const, pq, q_const = _bias_placement()
    q_rows = FOX_HEADS * BIAS_SLOTS
    return pl.pallas_call(
        _forget_kernel,
        grid=(b, s // t),
        in_specs=[pl.BlockSpec((1, KN_ROWS, t), lambda bi, i: (bi, 0, i)),
                  _resident((KN_ROWS, 1)), _resident(pk.shape), _resident(k_const.shape),
                  _resident(pq.shape), _resident(q_const.shape)],
        out_specs=(pl.BlockSpec((1, t, FOX_WIDTH), lambda bi, i: (bi, i, 0)),
                   pl.BlockSpec((1, q_rows, t), lambda bi, i: (bi, 0, i))),
        out_shape=(jax.ShapeDtypeStruct((b, s, FOX_WIDTH), BF16),
                   jax.ShapeDtypeStruct((b, q_rows, s), BF16)),
        scratch_shapes=[pltpu.VMEM((KN_ROWS, LANES), F32)],
        compiler_params=pltpu.CompilerParams(
            dimension_semantics=("parallel", "arbitrary"),
            vmem_limit_bytes=_vmem_limit(t * (KN_ROWS * 4 + FOX_WIDTH * 2 + q_rows * 2), pk.size * 2 + pq.size * 2,
                                         8 * t * FOX_WIDTH * 4)),
        name="forget_cumsum",
    )(gates, b_forget_col, jnp.asarray(pk, BF16), jnp.asarray(k_const), jnp.asarray(pq, BF16), jnp.asarray(q_const))


def _fox_kernel(qt_ref, qx_ref, k_ref, kx_ref, vt_ref, kn2_ref, flast_ref, o_ref, s_ref, p_ref, acc_ref):
    t = qt_ref.shape[2]
    i = pl.program_id(2)
    n_heads = 2 * FOX_PAIRS
    row = lax.broadcasted_iota(jnp.int32, (LANES, t), 0)
    low = row < FOX_HEAD_DIM
    pad = jnp.zeros((LANES - BIAS_SLOTS, t), BF16)
    q_aug = []
    for hd in range(n_heads):
        qt = qt_ref[0, (hd // 2) * LANES:(hd // 2 + 1) * LANES, :]
        own = jnp.where(low, qt, jnp.zeros_like(qt)) if hd % 2 == 0 else jnp.where(low, jnp.zeros_like(qt), qt)
        q_aug.append(jnp.concatenate([own, qx_ref[0, hd * BIAS_SLOTS:(hd + 1) * BIAS_SLOTS, :], pad], axis=0))

    def logits(j, masked):
        k0 = pl.multiple_of(j * t, t)
        if masked:
            causal = lax.broadcasted_iota(jnp.int32, (t, t), 0) <= lax.broadcasted_iota(jnp.int32, (t, t), 1)
        out = []
        for pr in range(FOX_PAIRS):
            lanes = slice(pr * LANES, (pr + 1) * LANES)
            k_aug = jnp.concatenate([k_ref[0, pl.ds(k0, t), lanes], kx_ref[0, pl.ds(k0, t), lanes]], axis=1)
            for h in range(2):
                s = _dot(k_aug, q_aug[2 * pr + h])
                out.append(jnp.where(causal, s, MASK_VALUE) if masked else s)
        return out

    def weighted_values(j, hd, p):
        k0 = pl.multiple_of(j * t, t)
        return _dot(vt_ref[0, hd * V_ROWS:(hd + 1) * V_ROWS, pl.ds(k0, t)], p)

    def stage_logits(j, masked, slot):
        maxes = []
        for hd, s in enumerate(logits(j, masked)):
            s_ref[slot, hd] = s
            maxes.append(jnp.max(s, axis=0, keepdims=True))
        return maxes

    def stage_softmax(hd, s_max, m, s_slot, p_slot):
        m_new = jnp.maximum(m, s_max)
        p_ref[p_slot, hd] = jnp.exp2(s_ref[s_slot, hd] - m_new).astype(BF16)
        return m_new, jnp.exp2(m - m_new)

    m0 = jnp.full((1, t), MASK_VALUE, F32)
    s_max = stage_logits(i, True, 1)
    stats = [stage_softmax(hd, s_max[hd], m0, 1, 0) for hd in range(n_heads)]
    s_max = stage_logits(jnp.maximum(i - 1, 0), False, 0)
    acc_ref[...] = jnp.zeros_like(acc_ref)

    tile_id = lax.broadcasted_iota(jnp.int32, (1, LANES), 1).astype(F32)
    first_head = pl.program_id(1) * n_heads
    first_needed = []
    for hd in range(n_heads):
        base = hd * BIAS_SLOTS
        f_k = sum(flast_ref[0, base + part:base + part + 1, :].astype(F32) for part in range(3))
        f_q = sum(qx_ref[0, base + part:base + part + 1, 0:1].astype(F32) for part in range(3))
        q = qt_ref[0, hd * FOX_HEAD_DIM:(hd + 1) * FOX_HEAD_DIM, :].astype(F32)
        qn2 = jnp.max(jnp.sum(q * q, axis=0, keepdims=True), axis=1, keepdims=True)
        qk = jnp.sqrt(qn2 * kn2_ref[0, pl.ds(first_head + hd, 1), 0:1]) * NORM_SLACK
        m_min = jnp.min(stats[hd][0], axis=1, keepdims=True)
        needed = (qk + f_q - f_k >= m_min - SKIP_LOG2) & (tile_id < i.astype(F32))
        first_needed.append(jnp.min(jnp.where(needed, tile_id, i.astype(F32)), axis=1, keepdims=True))
    first_tile = jnp.min(jnp.concatenate(first_needed, axis=1), axis=1, keepdims=True)[0, 0].astype(jnp.int32)
    n_trips = i - first_tile

    def trip(par, fetch, k, carry):
        prev, heads = carry
        for hd in range(n_heads):
            pv = weighted_values(prev, hd, p_ref[par, hd])
            acc_ref[hd] = heads[hd][1] * acc_ref[hd] + pv
        stats = [stage_softmax(hd, heads[hd][2], heads[hd][0], par, 1 - par) for hd in range(n_heads)]
        s_max = stage_logits(i - 2 - k, False, 1 - par) if fetch else [h[2] for h in heads]
        return i - 1 - k, tuple(stats[hd] + (s_max[hd],) for hd in range(n_heads))

    def either_trip(fetch, k, carry):
        return lax.cond((k & 1) == 0, functools.partial(trip, 0, fetch, k), functools.partial(trip, 1, fetch, k), carry)

    def finish(p_slot, carry):
        prev, heads = carry
        o_heads = []
        for hd in range(n_heads):
            acc = heads[hd][1] * acc_ref[hd] + weighted_values(prev, hd, p_ref[p_slot, hd])
            o_heads.append(acc[0:FOX_HEAD_DIM] * (1.0 / acc[FOX_HEAD_DIM:FOX_HEAD_DIM + 1]))
        o_ref[0] = jnp.concatenate(o_heads, axis=0).T.astype(o_ref.dtype)
        return prev

    def last_trip(par, carry):
        return finish(1 - par, trip(par, False, n_trips - 1, carry))

    carry = (i, tuple(stats[hd] + (s_max[hd],) for hd in range(n_heads)))
    carry = lax.fori_loop(0, n_trips - 1, functools.partial(either_trip, True), carry)
    last_par = (n_trips - 1) & 1
    lax.cond(n_trips == 0, functools.partial(finish, 0),
             lambda c: lax.cond(last_par == 0, functools.partial(last_trip, 0), functools.partial(last_trip, 1), c),
             carry)


def _fox_attention(fqt, qx, fk, kx, fvt, kn2, flast):
    b, s, _ = fk.shape
    t = min(FOX_TILE, s)
    width = FOX_PAIRS * LANES
    n_heads = 2 * FOX_PAIRS
    groups = FOX_WIDTH // width
    whole = lambda shape, index_map: pl.BlockSpec(shape, index_map, pipeline_mode=pl.Buffered(1))
    block_bytes = (width + n_heads * BIAS_SLOTS) * t * 2 + t * width * 2
    resident_bytes = (2 * width + n_heads * V_ROWS) * s * 2 + n_heads * t * (2 * t * 6 + V_ROWS * 4)
    return pl.pallas_call(
        _fox_kernel,
        grid=(b, groups, s // t),
        in_specs=[pl.BlockSpec((1, width, t), lambda bi, g, i: (bi, g, i)),
                  pl.BlockSpec((1, n_heads * BIAS_SLOTS, t), lambda bi, g, i: (bi, g, i)),
                  whole((1, s, width), lambda bi, g, i: (bi, 0, g)),
                  whole((1, s, width), lambda bi, g, i: (bi, 0, g)),
                  whole((1, n_heads * V_ROWS, s), lambda bi, g, i: (bi, g, 0)),
                  whole((1, KN_ROWS, LANES), lambda bi, g, i: (bi, 0, 0)),
                  whole((1, n_heads * BIAS_SLOTS, flast.shape[2]), lambda bi, g, i: (bi, g, 0))],
        out_specs=pl.BlockSpec((1, t, width), lambda bi, g, i: (bi, i, g)),
        out_shape=jax.ShapeDtypeStruct((b, s, FOX_WIDTH), BF16),
        scratch_shapes=[pltpu.VMEM((2, n_heads, t, t), F32), pltpu.VMEM((2, n_heads, t, t), BF16),
                        pltpu.VMEM((n_heads, V_ROWS, t), F32)],
        compiler_params=pltpu.CompilerParams(
            dimension_semantics=("parallel", "parallel", "parallel"),
            vmem_limit_bytes=_vmem_limit(block_bytes, resident_bytes, 8 * n_heads * t * t)),
        name="fox_attention",
    )(fqt, qx, fk, kx, fvt, kn2, flast)


def _gla_kernel(q_ref, k_ref, v_ref, gates_ref, wg_ref, bg_ref, gn_ref, o_ref, state_ref):
    @pl.when(pl.program_id(1) == 0)
    def _():
        state_ref[...] = jnp.zeros_like(state_ref)

    nb, t, _ = q_ref.shape
    c = GLA_CHUNK
    half = c // 2
    row = lax.broadcasted_iota(jnp.int32, (t, t), 0)
    col = lax.broadcasted_iota(jnp.int32, (t, t), 1)
    same_chunk = _div_pow2(row, c) == _div_pow2(col, c)
    tri = jnp.where(same_chunk & (col <= row), 1.0, 0.0).astype(BF16)
    cum_all = []
    for bb in range(nb):
        gate_logits = _dot_tn(gates_ref[bb, KN_ROWS:GATE_ROWS, :].astype(BF16), wg_ref[...]) + bg_ref[...]
        cum_all.append(_ones_dot_f32(tri, _log_sigmoid(gate_logits) * (1.0 / GLA_TAU)))

    crow = lax.broadcasted_iota(jnp.int32, (c, GLA_K_WIDTH), 0)
    first = crow < half
    lane = lax.broadcasted_iota(jnp.int32, (c, GLA_K_WIDTH), 1)
    head_lanes = [_div_pow2(lane, GLA_KEY_DIM) == h for h in range(GLA_HEADS)]
    sr = lax.broadcasted_iota(jnp.int32, (GLA_HEADS * c, c), 0) & (c - 1)
    sc = lax.broadcasted_iota(jnp.int32, (GLA_HEADS * c, c), 1)
    near = (_div_pow2(sr, half) == _div_pow2(sc, half)) & (sc <= sr)
    srow = lax.broadcasted_iota(jnp.int32, state_ref.shape[1:], 0)
    slane = lax.broadcasted_iota(jnp.int32, state_ref.shape[1:], 1)
    own_head = _div_pow2(srow, GLA_VAL_DIM) == _div_pow2(slane, GLA_KEY_DIM)

    def stack_heads(x):
        return jnp.concatenate([jnp.where(m, x, 0.0) for m in head_lanes], axis=0).astype(BF16)

    for ci, bb in [(ci, bb) for ci in range(t // c) for bb in range(nb)]:
        r0 = ci * c
        cum = cum_all[bb][r0:r0 + c]
        q = q_ref[bb, r0:r0 + c, :].astype(F32)
        k = k_ref[bb, r0:r0 + c, :].astype(F32)
        v = v_ref[bb, r0:r0 + c, :]
        last = cum[c - 1:c]
        ref_far = cum[half - 1:half]
        ref_near = jnp.where(first, cum[half // 2:half // 2 + 1], cum[half + half // 2:half + half // 2 + 1])
        q_far = jnp.where(first, 0.0, q * jnp.exp(cum - ref_far))
        k_far = jnp.where(first, k * jnp.exp(ref_far - cum), 0.0)
        q_near = q * jnp.exp(cum - ref_near)
        k_near = k * jnp.exp(ref_near - cum)
        scores = (_dot_nt(stack_heads(q_far), k_far.astype(BF16))
                  + jnp.where(near, _dot_nt(stack_heads(q_near), k_near.astype(BF16)), 0.0))
        state = state_ref[bb]
        o = _dot_nt((q * jnp.exp(cum)).astype(BF16), state.astype(BF16))
        p = scores.astype(BF16)
        for h in range(GLA_HEADS):
            lo, hi = h * GLA_VAL_DIM, (h + 1) * GLA_VAL_DIM
            o_h = o[:, lo:hi] + _dot(p[h * c:(h + 1) * c], v[:, lo:hi])
            y = o_h * lax.rsqrt(jnp.mean(o_h * o_h, axis=-1, keepdims=True) + RMS_EPS) * gn_ref[:, lo:hi]
            o_ref[bb, r0:r0 + c, lo:hi] = y.astype(o_ref.dtype)
        k_tail = (k * jnp.exp(last - cum)).astype(BF16)
        update = _dot_tn(v, k_tail)
        state_ref[bb] = state * jnp.exp(last) + jnp.where(own_head, update, 0.0)


def _gla(gq, gk, gv, gates, w_gate, b_gate, g_norm):
    b, s, _ = gq.shape
    t = min(GLA_TOKENS, s)
    nb = GLA_BATCH if b % GLA_BATCH == 0 else 1
    tok = lambda width: pl.BlockSpec((nb, t, width), lambda bi, i: (bi, i, 0))
    block_bytes = nb * t * (2 * GLA_K_WIDTH * 2 + 2 * GLA_V_WIDTH * 2 + GATE_ROWS * 4)
    state_bytes = nb * GLA_V_WIDTH * GLA_K_WIDTH * 4
    return pl.pallas_call(
        _gla_kernel,
        grid=(b // nb, s // t),
        in_specs=[tok(GLA_K_WIDTH), tok(GLA_K_WIDTH), tok(GLA_V_WIDTH),
                  pl.BlockSpec((nb, GATE_ROWS, t), lambda bi, i: (bi, 0, i)),
                  _resident(w_gate.shape), _resident((1, GLA_K_WIDTH)), _resident((1, GLA_V_WIDTH))],
        out_specs=tok(GLA_V_WIDTH),
        out_shape=jax.ShapeDtypeStruct((b, s, GLA_V_WIDTH), BF16),
        scratch_shapes=[pltpu.VMEM((nb, GLA_V_WIDTH, GLA_K_WIDTH), F32)],
        compiler_params=pltpu.CompilerParams(
            dimension_semantics=("parallel", "arbitrary"),
            vmem_limit_bytes=_vmem_limit(block_bytes, state_bytes, 8 * state_bytes + 16 * t * GLA_K_WIDTH * 4)),
        name="gla",
    )(gq, gk, gv, gates, w_gate, b_gate, g_norm)


_B_POOL = 0
_B_Z = _B_POOL + POOL_WIDTH
_B_MERGE = _B_Z + FOX_WIDTH + GLA_V_WIDTH + POOL_WIDTH


def _merge_kernel(x_ref, mod_ref, gpre_ref, gpost_ref, ofox_ref, ogla_ref, wb_ref, wpool_ref, pscale_ref,
                  wbr_ref, wout_ref, o_ref, u_ref, lvl_a_ref, lvl_b_ref):
    t = x_ref.shape[1]
    i = pl.program_id(1)
    x = x_ref[0]
    mod = mod_ref[0]
    h = _modulated_norm(x, gpre_ref[...], mod)

    @pl.when(i == 0)
    def _():
        u_ref[0:POOL_HALO, :] = jnp.zeros((POOL_HALO, POOL_WIDTH), F32)

    @pl.when(i > 0)
    def _():
        u_ref[0:POOL_HALO, :] = u_ref[t:t + POOL_HALO, :]

    u = _dot(h, wb_ref[:, _B_POOL:_B_Z])
    u_ref[POOL_HALO:POOL_HALO + t, :] = u
    count = (i * t + 1 + lax.broadcasted_iota(jnp.int32, (t, 1), 0)).astype(F32)
    levels = (u_ref, lvl_a_ref, lvl_b_ref, lvl_a_ref, lvl_b_ref)
    for n in range(1, len(POOL_WINDOWS) + 1):
        src, dst, back = levels[n - 1], levels[n], 2 ** (n - 1)
        first, rows = 8 * n, POOL_HALO + t - 8 * n
        lanes = slice((n - 1) * POOL_GROUP_DIM, POOL_WIDTH)
        dst[pl.ds(first, rows), lanes] = src[pl.ds(first, rows), lanes] + src[pl.ds(first - back, rows), lanes]
    pooled = []
    for g, w in enumerate(POOL_WINDOWS):
        lo, hi = g * POOL_GROUP_DIM, (g + 1) * POOL_GROUP_DIM
        window = levels[g + 1][pl.ds(POOL_HALO, t), lo:hi]
        mean = window * (1.0 / jnp.minimum(count, float(w)))
        diff = (mean - u[:, lo:hi]).astype(BF16)
        pooled.append(_dot(diff, wpool_ref[g]) * pscale_ref[:, lo:hi])
    o_pool = jnp.concatenate(pooled, axis=-1)

    branches = (ofox_ref[0].astype(F32), ogla_ref[0].astype(F32), o_pool)
    merged = jnp.zeros((t, D_MODEL), F32)
    for br, o_br in enumerate(branches):
        z = _dot(h, wb_ref[:, _B_Z + br * FOX_WIDTH:_B_Z + (br + 1) * FOX_WIDTH])
        y = _dot((o_br * _silu(z)).astype(BF16), wbr_ref[br])
        m = _dot(h, wb_ref[:, _B_MERGE + br * D_MODEL:_B_MERGE + (br + 1) * D_MODEL])
        merged = merged + _sigmoid(m) * y
    out = _dot(merged.astype(BF16), wout_ref[...])
    gated_gain = mod[:, 2 * D_MODEL:3 * D_MODEL] * gpost_ref[...]
    o_ref[0] = x + out * lax.rsqrt(jnp.mean(out * out, axis=-1, keepdims=True) + RMS_EPS) * gated_gain


def _merge(x, mod, g_pre, g_post, o_fox, o_gla, w_b, w_pool, pool_scale, w_br, w_out):
    b, s, d = x.shape
    t = min(MERGE_TOKENS, s)
    tok = lambda width: pl.BlockSpec((1, t, width), lambda bi, i: (bi, i, 0))
    block_bytes = 2 * t * d * 4 + 2 * t * FOX_WIDTH * 2
    weight_bytes = (w_b.size + w_pool.size + w_br.size + w_out.size) * 2
    return pl.pallas_call(
        _merge_kernel,
        grid=(b, s // t),
        in_specs=[tok(d),
                  pl.BlockSpec((1, 1, 3 * d), lambda bi, i: (bi, 0, 0)),
                  _resident((1, d)), _resident((1, d)),
                  tok(FOX_WIDTH), tok(GLA_V_WIDTH),
                  _resident(w_b.shape), _resident(w_pool.shape), _resident((1, POOL_WIDTH)),
                  _resident(w_br.shape), _resident(w_out.shape)],
        out_specs=tok(d),
        out_shape=jax.ShapeDtypeStruct((b, s, d), F32),
        scratch_shapes=[pltpu.VMEM((POOL_HALO + t, POOL_WIDTH), F32)] * 3,
        compiler_params=pltpu.CompilerParams(
            dimension_semantics=("parallel", "arbitrary"),
            vmem_limit_bytes=_vmem_limit(block_bytes, weight_bytes, 12 * t * d * 4)),
        name="merge",
    )(x, mod, g_pre, g_post, o_fox, o_gla, w_b, w_pool, pool_scale, w_br, w_out)


def _layer(x, mod, g_pre, g_post, w_in, b_forget, w_gla_gate, b_gla_gate, g_gla_norm, w_pool, pool_scale,
           w_br_fox, w_br_gla, w_br_pool, w_out):
    b, s, d = x.shape
    seg = lambda n: w_in[:, _OFF[n]:_OFF[n + 1]]
    w_a = jnp.concatenate([seg(1), seg(4) * GLA_KEY_DIM ** -0.5, seg(5), seg(6)], axis=1).astype(BF16)
    w_t = jnp.concatenate([seg(0) * (LOG2_E * FOX_HEAD_DIM ** -0.5), seg(3),
                           jnp.zeros((d, KN_ROWS - FOX_HEADS), w_in.dtype), seg(7), seg(2)], axis=1).T.astype(BF16)
    w_b = w_in[:, _OFF[8]:_OFF[13]].astype(BF16)
    b_forget_col = jnp.zeros((KN_ROWS, 1), F32).at[:FOX_HEADS, 0].set(b_forget)
    w_br = jnp.stack([w_br_fox, w_br_gla, w_br_pool]).astype(BF16)
    mod3 = mod.reshape(b, 1, 3 * d)
    g_pre = g_pre.reshape(1, d)

    head_sel = (jnp.arange(KN_ROWS)[:, None] == jnp.arange(FOX_WIDTH)[None, :] // FOX_HEAD_DIM).astype(BF16)
    fk, gq, gk, gv, fqt, fvt, gates, kn2 = _in_proj(x, mod3, g_pre, w_a, w_t, head_sel)
    kx, qx = _forget_cumsum(gates, b_forget_col)
    t_fox = min(FOX_TILE, s)
    assert s // t_fox <= LANES, "the tile-skip bound keeps one key tile per lane"
    flast = qx[:, :, t_fox - 1::t_fox]
    flast = jnp.pad(flast, ((0, 0), (0, 0), (0, -flast.shape[2] % LANES)))
    o_fox = _fox_attention(fqt, qx, fk, kx, fvt, kn2, flast)
    o_gla = _gla(gq, gk, gv, gates, w_gla_gate.astype(BF16), b_gla_gate.reshape(1, GLA_K_WIDTH),
                 g_gla_norm.reshape(1, GLA_V_WIDTH))
    return _merge(x, mod3, g_pre, g_post.reshape(1, d), o_fox, o_gla, w_b, w_pool.astype(BF16),
                  pool_scale.reshape(1, POOL_WIDTH), w_br, w_out.astype(BF16))


def kernel(x, c, w_ada, b_ada, g_pre, g_post, w_in, b_forget, w_gla_gate, b_gla_gate, g_gla_norm, w_pool,
           pool_scale, w_br_fox, w_br_gla, w_br_pool, w_out):
    mods = _adaln_mod(c, w_ada, b_ada)
    h = x
    for i in range(w_in.shape[0]):
        h = _layer(h, mods[i], g_pre[i], g_post[i], w_in[i], b_forget[i], w_gla_gate[i], b_gla_gate[i],
                   g_gla_norm[i], w_pool[i], pool_scale[i], w_br_fox[i], w_br_gla[i], w_br_pool[i], w_out[i])
    return h
```

```python
import functools

import numpy as np
import jax
import jax.numpy as jnp
from jax import lax
from jax.experimental import pallas as pl
from jax.experimental.pallas import tpu as pltpu

F32 = jnp.float32
BF16 = jnp.bfloat16

D_MODEL = 1024
FOX_HEADS = 8
FOX_HEAD_DIM = 64
FOX_WIDTH = FOX_HEADS * FOX_HEAD_DIM
GLA_HEADS = 4
GLA_KEY_DIM = 64
GLA_VAL_DIM = 128
GLA_K_WIDTH = GLA_HEADS * GLA_KEY_DIM
GLA_V_WIDTH = GLA_HEADS * GLA_VAL_DIM
GLA_GATE_RANK = 16
GLA_TAU = 16.0
POOL_WINDOWS = (2, 4, 8, 16)
POOL_GROUP_DIM = 128
POOL_WIDTH = len(POOL_WINDOWS) * POOL_GROUP_DIM
N_BRANCHES = 3
RMS_EPS = 1e-6
IN_SPLITS = (FOX_WIDTH, FOX_WIDTH, FOX_WIDTH, FOX_HEADS,
             GLA_K_WIDTH, GLA_K_WIDTH, GLA_V_WIDTH, GLA_GATE_RANK,
             POOL_WIDTH, FOX_WIDTH, GLA_V_WIDTH, POOL_WIDTH, N_BRANCHES * D_MODEL)
_OFF = np.concatenate([[0], np.cumsum(IN_SPLITS)]).tolist()

LANES = 128
MXU_WIDTH = 256
V_ROWS = FOX_HEAD_DIM + 16
POOL_HALO = 8 * len(POOL_WINDOWS)
MASK_VALUE = -1e30
LOG2_E = 1.4426950408889634
SKIP_LOG2 = 136.0
NORM_SLACK = 1.02
V7X_VMEM_BYTES = 64 * 1024 * 1024

IN_TOKENS = 512
FORGET_TOKENS = 2048
FOX_TILE = 256
FOX_PAIRS = 4
GLA_TOKENS = 256
GLA_CHUNK = 64
GLA_BATCH = 8
MERGE_TOKENS = 512


def _vmem_limit(block_bytes, resident_bytes, temp_bytes):
    need = 2 * block_bytes + resident_bytes + temp_bytes
    return int(min(max(need, 16 * 1024 * 1024), V7X_VMEM_BYTES - 8 * 1024 * 1024))


def _resident(shape):
    zeros = (0,) * len(shape)
    return pl.BlockSpec(shape, lambda *_: zeros, pipeline_mode=pl.Buffered(1))


def _div_pow2(i, n):
    shift = n.bit_length() - 1
    assert 1 << shift == n
    return jnp.right_shift(i, shift)


def _log_sigmoid(x):
    return jnp.minimum(x, 0.0) - jnp.log(1.0 + jnp.exp(-jnp.abs(x)))


def _sigmoid(x):
    return 0.5 * jnp.tanh(0.5 * x) + 0.5


def _silu(x):
    half = 0.5 * x
    return half * jnp.tanh(half) + half


def _dot(a, b):
    return jnp.dot(a, b, preferred_element_type=F32)


def _dot_nt(a, b):
    return lax.dot_general(a, b, (((1,), (1,)), ((), ())), preferred_element_type=F32)


def _dot_tn(a, b):
    return lax.dot_general(a, b, (((0,), (0,)), ((), ())), preferred_element_type=F32)


def _split3(x):
    hi = x.astype(BF16)
    r1 = x - hi.astype(F32)
    mid = r1.astype(BF16)
    lo = (r1 - mid.astype(F32)).astype(BF16)
    return hi, mid, lo


def _ones_dot_f32(ones_bf16, x):
    hi, mid, lo = _split3(x)
    return _dot(ones_bf16, hi) + _dot(ones_bf16, mid) + _dot(ones_bf16, lo)


def _modulated_norm(x, g, mod):
    shift = mod[:, 0:D_MODEL]
    scale = mod[:, D_MODEL:2 * D_MODEL]
    gain = g * (1.0 + scale)
    return (x * lax.rsqrt(jnp.mean(x * x, axis=-1, keepdims=True) + RMS_EPS) * gain + shift).astype(BF16)


def _mod_kernel(c_ref, w_ref, b_ref, o_ref):
    c = c_ref[...]
    o_ref[0] = jnp.dot(_silu(c), w_ref[0], preferred_element_type=F32,
                       precision=lax.Precision.HIGHEST) + b_ref[0]


def _adaln_mod(c, w_ada, b_ada):
    depth, d, d3 = w_ada.shape
    b = c.shape[0]
    n_col = d3 // d
    return pl.pallas_call(
        _mod_kernel,
        grid=(depth, n_col),
        in_specs=[pl.BlockSpec((b, d), lambda l, j: (0, 0)),
                  pl.BlockSpec((1, d, d), lambda l, j: (l, 0, j)),
                  pl.BlockSpec((1, 1, d), lambda l, j: (l, 0, j))],
        out_specs=pl.BlockSpec((1, b, d), lambda l, j: (l, 0, j)),
        out_shape=jax.ShapeDtypeStruct((depth, b, d3), F32),
        compiler_params=pltpu.CompilerParams(
            dimension_semantics=("parallel", "parallel"),
            vmem_limit_bytes=_vmem_limit(d * d * 4 + 2 * b * d * 4, 0, 4 * b * d * 4)),
        name="adaln_mod",
    )(c, w_ada, b_ada.reshape(depth, 1, d3))


_A_FOX_K = 0
_A_GLA_Q = _A_FOX_K + FOX_WIDTH
_A_GLA_K = _A_GLA_Q + GLA_K_WIDTH
_A_GLA_V = _A_GLA_K + GLA_K_WIDTH
_A_WIDTH = _A_GLA_V + GLA_V_WIDTH
KN_ROWS = 16
GATE_ROWS = KN_ROWS + GLA_GATE_RANK
_T_FOX_Q = 0
_T_GATES = _T_FOX_Q + FOX_WIDTH
_T_FOX_V = _T_GATES + GATE_ROWS
_T_ROWS = _T_FOX_V + FOX_WIDTH


def _in_proj_kernel(x_ref, mod_ref, g_ref, wa_ref, wt_ref, hsel_ref,
                    fk_ref, gq_ref, gk_ref, gv_ref, fqt_ref, fvt_ref, gates_ref, kn2_ref):
    h = _modulated_norm(x_ref[0], g_ref[...], mod_ref[0])
    fk = _dot(h, wa_ref[:, _A_FOX_K:_A_GLA_Q]).astype(BF16)
    fk_ref[0] = fk
    norms = _dot_nt(hsel_ref[...], fk * fk)
    tile_max = jnp.broadcast_to(jnp.max(norms, axis=1, keepdims=True), kn2_ref.shape[1:])

    @pl.when(pl.program_id(1) == 0)
    def _():
        kn2_ref[0] = tile_max

    @pl.when(pl.program_id(1) > 0)
    def _():
        kn2_ref[0] = jnp.maximum(kn2_ref[0], tile_max)

    gq_ref[0] = _dot(h, wa_ref[:, _A_GLA_Q:_A_GLA_K]).astype(BF16)
    gk_ref[0] = _dot(h, wa_ref[:, _A_GLA_K:_A_GLA_V]).astype(BF16)
    gv_ref[0] = _dot(h, wa_ref[:, _A_GLA_V:_A_WIDTH]).astype(BF16)
    tr = _dot_nt(wt_ref[...], h)
    fqt_ref[0] = tr[_T_FOX_Q:_T_GATES].astype(BF16)
    gates_ref[0] = tr[_T_GATES:_T_FOX_V]
    ones = jnp.ones((V_ROWS - FOX_HEAD_DIM, tr.shape[1]), BF16)
    for hd in range(FOX_HEADS):
        v_rows = tr[_T_FOX_V + hd * FOX_HEAD_DIM:_T_FOX_V + (hd + 1) * FOX_HEAD_DIM]
        fvt_ref[0, hd * V_ROWS:hd * V_ROWS + FOX_HEAD_DIM, :] = v_rows.astype(BF16)
        fvt_ref[0, hd * V_ROWS + FOX_HEAD_DIM:(hd + 1) * V_ROWS, :] = ones


def _in_proj(x, mod, g_pre, w_a, w_t, head_sel):
    b, s, d = x.shape
    t = min(IN_TOKENS, s)
    tok = lambda width: pl.BlockSpec((1, t, width), lambda bi, i: (bi, i, 0))
    tok_t = lambda rows: pl.BlockSpec((1, rows, t), lambda bi, i: (bi, 0, i))
    v_rows = FOX_HEADS * V_ROWS
    out_shapes = (
        jax.ShapeDtypeStruct((b, s, FOX_WIDTH), BF16),
        jax.ShapeDtypeStruct((b, s, GLA_K_WIDTH), BF16),
        jax.ShapeDtypeStruct((b, s, GLA_K_WIDTH), BF16),
        jax.ShapeDtypeStruct((b, s, GLA_V_WIDTH), BF16),
        jax.ShapeDtypeStruct((b, FOX_WIDTH, s), BF16),
        jax.ShapeDtypeStruct((b, v_rows, s), BF16),
        jax.ShapeDtypeStruct((b, GATE_ROWS, s), F32),
        jax.ShapeDtypeStruct((b, KN_ROWS, LANES), F32),
    )
    out_specs = (tok(FOX_WIDTH), tok(GLA_K_WIDTH), tok(GLA_K_WIDTH), tok(GLA_V_WIDTH), tok_t(FOX_WIDTH), tok_t(v_rows),
                 tok_t(GATE_ROWS), pl.BlockSpec((1, KN_ROWS, LANES), lambda bi, i: (bi, 0, 0)))
    n_out = _A_WIDTH + FOX_WIDTH + v_rows
    block_bytes = t * d * 4 + t * n_out * 2 + t * GATE_ROWS * 4
    weight_bytes = d * (_A_WIDTH + _T_ROWS) * 2
    return pl.pallas_call(
        _in_proj_kernel,
        grid=(b, s // t),
        in_specs=[tok(d),
                  pl.BlockSpec((1, 1, 3 * d), lambda bi, i: (bi, 0, 0)),
                  _resident((1, d)), _resident(w_a.shape), _resident(w_t.shape), _resident(head_sel.shape)],
        out_specs=out_specs,
        out_shape=out_shapes,
        compiler_params=pltpu.CompilerParams(
            dimension_semantics=("parallel", "arbitrary"),
            vmem_limit_bytes=_vmem_limit(block_bytes, weight_bytes, 6 * t * d * 4)),
        name="in_proj",
    )(x, mod, g_pre, w_a, w_t, head_sel)


BIAS_SLOTS = 16


def _bias_placement():
    pk = np.zeros((3 * KN_ROWS, FOX_WIDTH), np.float32)
    k_const = np.zeros((1, FOX_WIDTH), np.float32)
    pq = np.zeros((FOX_HEADS * BIAS_SLOTS, 3 * KN_ROWS), np.float32)
    q_const = np.zeros((FOX_HEADS * BIAS_SLOTS, 1), np.float32)
    for head in range(FOX_HEADS):
        pair, odd = divmod(head, 2)
        for part in range(3):
            pk[part * KN_ROWS + head, pair * LANES + 3 + 3 * odd + part] = -1.0
            pq[head * BIAS_SLOTS + part, part * KN_ROWS + head] = 1.0
            q_const[head * BIAS_SLOTS + 3 + 3 * odd + part, 0] = 1.0
            k_const[0, pair * LANES + part] = 1.0
    return pk, k_const, pq, q_const


def _forget_kernel(fft_ref, b_ref, pk_ref, kc_ref, pq_ref, qc_ref, kx_ref, qx_ref, carry_ref):
    @pl.when(pl.program_id(1) == 0)
    def _():
        carry_ref[...] = jnp.zeros_like(carry_ref)

    t = fft_ref.shape[2]
    w = min(MXU_WIDTH, t)
    log_f = _log_sigmoid(fft_ref[0] + b_ref[...])
    row = lax.broadcasted_iota(jnp.int32, (w, w), 0)
    col = lax.broadcasted_iota(jnp.int32, (w, w), 1)
    upper = jnp.where(row <= col, 1.0, 0.0).astype(BF16)
    split = jnp.concatenate(_split3(log_f), axis=0)
    carry = carry_ref[:, 0:1]
    blocks = []
    for j in range(t // w):
        sums = _dot(split[:, j * w:(j + 1) * w], upper)
        blocks.append(sums[0:KN_ROWS] + sums[KN_ROWS:2 * KN_ROWS] + sums[2 * KN_ROWS:3 * KN_ROWS] + carry)
        carry = blocks[-1][:, w - 1:w]
    carry_ref[...] = jnp.broadcast_to(carry, carry_ref.shape)
    cs = jnp.concatenate(blocks, axis=1)
    parts = jnp.concatenate(_split3(cs * LOG2_E), axis=0)
    qx_ref[0] = (_dot(pq_ref[...], parts) + qc_ref[...]).astype(BF16)
    kx_ref[0] = (_dot_tn(parts, pk_ref[...]) + kc_ref[...]).astype(BF16)


def _forget_cumsum(gates, b_forget_col):
    b, _, s = gates.shape
    t = min(FORGET_TOKENS, s)
    pk, k_const, pq, q_const = _bias_placement()
    q_rows = FOX_HEADS * BIAS_SLOTS
    return pl.pallas_call(
        _forget_kernel,
        grid=(b, s // t),
        in_specs=[pl.BlockSpec((1, KN_ROWS, t), lambda bi, i: (bi, 0, i)),
                  _resident((KN_ROWS, 1)), _resident(pk.shape), _resident(k_const.shape),
                  _resident(pq.shape), _resident(q_const.shape)],
        out_specs=(pl.BlockSpec((1, t, FOX_WIDTH), lambda bi, i: (bi, i, 0)),
                   pl.BlockSpec((1, q_rows, t), lambda bi, i: (bi, 0, i))),
        out_shape=(jax.ShapeDtypeStruct((b, s, FOX_WIDTH), BF16),
                   jax.ShapeDtypeStruct((b, q_rows, s), BF16)),
        scratch_shapes=[pltpu.VMEM((KN_ROWS, LANES), F32)],
        compiler_params=pltpu.CompilerParams(
            dimension_semantics=("parallel", "arbitrary"),
            vmem_limit_bytes=_vmem_limit(t * (KN_ROWS * 4 + FOX_WIDTH * 2 + q_rows * 2), pk.size * 2 + pq.size * 2,
                                         8 * t * FOX_WIDTH * 4)),
        name="forget_cumsum",
    )(gates, b_forget_col, jnp.asarray(pk, BF16), jnp.asarray(k_const), jnp.asarray(pq, BF16), jnp.asarray(q_const))


def _fox_kernel(qt_ref, qx_ref, k_ref, kx_ref, vt_ref, kn2_ref, flast_ref, o_ref, s_ref, p_ref, acc_ref):
    t = qt_ref.shape[2]
    i = pl.program_id(2)
    n_heads = 2 * FOX_PAIRS
    row = lax.broadcasted_iota(jnp.int32, (LANES, t), 0)
    low = row < FOX_HEAD_DIM
    pad = jnp.zeros((LANES - BIAS_SLOTS, t), BF16)
    q_aug = []
    for hd in range(n_heads):
        qt = qt_ref[0, (hd // 2) * LANES:(hd // 2 + 1) * LANES, :]
        own = jnp.where(low, qt, jnp.zeros_like(qt)) if hd % 2 == 0 else jnp.where(low, jnp.zeros_like(qt), qt)
        q_aug.append(jnp.concatenate([own, qx_ref[0, hd * BIAS_SLOTS:(hd + 1) * BIAS_SLOTS, :], pad], axis=0))

    def logits(j, masked):
        k0 = pl.multiple_of(j * t, t)
        if masked:
            causal = lax.broadcasted_iota(jnp.int32, (t, t), 0) <= lax.broadcasted_iota(jnp.int32, (t, t), 1)
        out = []
        for pr in range(FOX_PAIRS):
            lanes = slice(pr * LANES, (pr + 1) * LANES)
            k_aug = jnp.concatenate([k_ref[0, pl.ds(k0, t), lanes], kx_ref[0, pl.ds(k0, t), lanes]], axis=1)
            for h in range(2):
                s = _dot(k_aug, q_aug[2 * pr + h])
                out.append(jnp.where(causal, s, MASK_VALUE) if masked else s)
        return out

    def weighted_values(j, hd, p):
        k0 = pl.multiple_of(j * t, t)
        return _dot(vt_ref[0, hd * V_ROWS:(hd + 1) * V_ROWS, pl.ds(k0, t)], p)

    def stage_logits(j, masked, slot):
        maxes = []
        for hd, s in enumerate(logits(j, masked)):
            s_ref[slot, hd] = s
            maxes.append(jnp.max(s, axis=0, keepdims=True))
        return maxes

    def stage_softmax(hd, s_max, m, s_slot, p_slot):
        m_new = jnp.maximum(m, s_max)
        p_ref[p_slot, hd] = jnp.exp2(s_ref[s_slot, hd] - m_new).astype(BF16)
        return m_new, jnp.exp2(m - m_new)

    m0 = jnp.full((1, t), MASK_VALUE, F32)
    s_max = stage_logits(i, True, 1)
    stats = [stage_softmax(hd, s_max[hd], m0, 1, 0) for hd in range(n_heads)]
    s_max = stage_logits(jnp.maximum(i - 1, 0), False, 0)
    acc_ref[...] = jnp.zeros_like(acc_ref)

    tile_id = lax.broadcasted_iota(jnp.int32, (1, LANES), 1).astype(F32)
    first_head = pl.program_id(1) * n_heads
    first_needed = []
    for hd in range(n_heads):
        base = hd * BIAS_SLOTS
        f_k = sum(flast_ref[0, base + part:base + part + 1, :].astype(F32) for part in range(3))
        f_q = sum(qx_ref[0, base + part:base + part + 1, 0:1].astype(F32) for part in range(3))
        q = qt_ref[0, hd * FOX_HEAD_DIM:(hd + 1) * FOX_HEAD_DIM, :].astype(F32)
        qn2 = jnp.max(jnp.sum(q * q, axis=0, keepdims=True), axis=1, keepdims=True)
        qk = jnp.sqrt(qn2 * kn2_ref[0, pl.ds(first_head + hd, 1), 0:1]) * NORM_SLACK
        m_min = jnp.min(stats[hd][0], axis=1, keepdims=True)
        needed = (qk + f_q - f_k >= m_min - SKIP_LOG2) & (tile_id < i.astype(F32))
        first_needed.append(jnp.min(jnp.where(needed, tile_id, i.astype(F32)), axis=1, keepdims=True))
    first_tile = jnp.min(jnp.concatenate(first_needed, axis=1), axis=1, keepdims=True)[0, 0].astype(jnp.int32)
    n_trips = i - first_tile

    def trip(par, fetch, k, carry):
        prev, heads = carry
        for hd in range(n_heads):
            pv = weighted_values(prev, hd, p_ref[par, hd])
            acc_ref[hd] = heads[hd][1] * acc_ref[hd] + pv
        stats = [stage_softmax(hd, heads[hd][2], heads[hd][0], par, 1 - par) for hd in range(n_heads)]
        s_max = stage_logits(i - 2 - k, False, 1 - par) if fetch else [h[2] for h in heads]
        return i - 1 - k, tuple(stats[hd] + (s_max[hd],) for hd in range(n_heads))

    def either_trip(fetch, k, carry):
        return lax.cond((k & 1) == 0, functools.partial(trip, 0, fetch, k), functools.partial(trip, 1, fetch, k), carry)

    def finish(p_slot, carry):
        prev, heads = carry
        o_heads = []
        for hd in range(n_heads):
            acc = heads[hd][1] * acc_ref[hd] + weighted_values(prev, hd, p_ref[p_slot, hd])
            o_heads.append(acc[0:FOX_HEAD_DIM] * (1.0 / acc[FOX_HEAD_DIM:FOX_HEAD_DIM + 1]))
        o_ref[0] = jnp.concatenate(o_heads, axis=0).T.astype(o_ref.dtype)
        return prev

    def last_trip(par, carry):
        return finish(1 - par, trip(par, False, n_trips - 1, carry))

    carry = (i, tuple(stats[hd] + (s_max[hd],) for hd in range(n_heads)))
    carry = lax.fori_loop(0, n_trips - 1, functools.partial(either_trip, True), carry)
    last_par = (n_trips - 1) & 1
    lax.cond(n_trips == 0, functools.partial(finish, 0),
             lambda c: lax.cond(last_par == 0, functools.partial(last_trip, 0), functools.partial(last_trip, 1), c),
             carry)


def _fox_attention(fqt, qx, fk, kx, fvt, kn2, flast):
    b, s, _ = fk.shape
    t = min(FOX_TILE, s)
    width = FOX_PAIRS * LANES
    n_heads = 2 * FOX_PAIRS
    groups = FOX_WIDTH // width
    whole = lambda shape, index_map: pl.BlockSpec(shape, index_map, pipeline_mode=pl.Buffered(1))
    block_bytes = (width + n_heads * BIAS_SLOTS) * t * 2 + t * width * 2
    resident_bytes = (2 * width + n_heads * V_ROWS) * s * 2 + n_heads * t * (2 * t * 6 + V_ROWS * 4)
    return pl.pallas_call(
        _fox_kernel,
        grid=(b, groups, s // t),
        in_specs=[pl.BlockSpec((1, width, t), lambda bi, g, i: (bi, g, i)),
                  pl.BlockSpec((1, n_heads * BIAS_SLOTS, t), lambda bi, g, i: (bi, g, i)),
                  whole((1, s, width), lambda bi, g, i: (bi, 0, g)),
                  whole((1, s, width), lambda bi, g, i: (bi, 0, g)),
                  whole((1, n_heads * V_ROWS, s), lambda bi, g, i: (bi, g, 0)),
                  whole((1, KN_ROWS, LANES), lambda bi, g, i: (bi, 0, 0)),
                  whole((1, n_heads * BIAS_SLOTS, flast.shape[2]), lambda bi, g, i: (bi, g, 0))],
        out_specs=pl.BlockSpec((1, t, width), lambda bi, g, i: (bi, i, g)),
        out_shape=jax.ShapeDtypeStruct((b, s, FOX_WIDTH), BF16),
        scratch_shapes=[pltpu.VMEM((2, n_heads, t, t), F32), pltpu.VMEM((2, n_heads, t, t), BF16),
                        pltpu.VMEM((n_heads, V_ROWS, t), F32)],
        compiler_params=pltpu.CompilerParams(
            dimension_semantics=("parallel", "parallel", "parallel"),
            vmem_limit_bytes=_vmem_limit(block_bytes, resident_bytes, 8 * n_heads * t * t)),
        name="fox_attention",
    )(fqt, qx, fk, kx, fvt, kn2, flast)


def _gla_kernel(q_ref, k_ref, v_ref, gates_ref, wg_ref, bg_ref, gn_ref, o_ref, state_ref):
    @pl.when(pl.program_id(1) == 0)
    def _():
        state_ref[...] = jnp.zeros_like(state_ref)

    nb, t, _ = q_ref.shape
    c = GLA_CHUNK
    half = c // 2
    row = lax.broadcasted_iota(jnp.int32, (t, t), 0)
    col = lax.broadcasted_iota(jnp.int32, (t, t), 1)
    same_chunk = _div_pow2(row, c) == _div_pow2(col, c)
    tri = jnp.where(same_chunk & (col <= row), 1.0, 0.0).astype(BF16)
    cum_all = []
    for bb in range(nb):
        gate_logits = _dot_tn(gates_ref[bb, KN_ROWS:GATE_ROWS, :].astype(BF16), wg_ref[...]) + bg_ref[...]
        cum_all.append(_ones_dot_f32(tri, _log_sigmoid(gate_logits) * (1.0 / GLA_TAU)))

    crow = lax.broadcasted_iota(jnp.int32, (c, GLA_K_WIDTH), 0)
    first = crow < half
    lane = lax.broadcasted_iota(jnp.int32, (c, GLA_K_WIDTH), 1)
    head_lanes = [_div_pow2(lane, GLA_KEY_DIM) == h for h in range(GLA_HEADS)]
    sr = lax.broadcasted_iota(jnp.int32, (GLA_HEADS * c, c), 0) & (c - 1)
    sc = lax.broadcasted_iota(jnp.int32, (GLA_HEADS * c, c), 1)
    near = (_div_pow2(sr, half) == _div_pow2(sc, half)) & (sc <= sr)
    srow = lax.broadcasted_iota(jnp.int32, state_ref.shape[1:], 0)
    slane = lax.broadcasted_iota(jnp.int32, state_ref.shape[1:], 1)
    own_head = _div_pow2(srow, GLA_VAL_DIM) == _div_pow2(slane, GLA_KEY_DIM)

    def stack_heads(x):
        return jnp.concatenate([jnp.where(m, x, 0.0) for m in head_lanes], axis=0).astype(BF16)

    for ci, bb in [(ci, bb) for ci in range(t // c) for bb in range(nb)]:
        r0 = ci * c
        cum = cum_all[bb][r0:r0 + c]
        q = q_ref[bb, r0:r0 + c, :].astype(F32)
        k = k_ref[bb, r0:r0 + c, :].astype(F32)
        v = v_ref[bb, r0:r0 + c, :]
        last = cum[c - 1:c]
        ref_far = cum[half - 1:half]
        ref_near = jnp.where(first, cum[half // 2:half // 2 + 1], cum[half + half // 2:half + half // 2 + 1])
        q_far = jnp.where(first, 0.0, q * jnp.exp(cum - ref_far))
        k_far = jnp.where(first, k * jnp.exp(ref_far - cum), 0.0)
        q_near = q * jnp.exp(cum - ref_near)
        k_near = k * jnp.exp(ref_near - cum)
        scores = (_dot_nt(stack_heads(q_far), k_far.astype(BF16))
                  + jnp.where(near, _dot_nt(stack_heads(q_near), k_near.astype(BF16)), 0.0))
        state = state_ref[bb]
        o = _dot_nt((q * jnp.exp(cum)).astype(BF16), state.astype(BF16))
        p = scores.astype(BF16)
        for h in range(GLA_HEADS):
            lo, hi = h * GLA_VAL_DIM, (h + 1) * GLA_VAL_DIM
            o_h = o[:, lo:hi] + _dot(p[h * c:(h + 1) * c], v[:, lo:hi])
            y = o_h * lax.rsqrt(jnp.mean(o_h * o_h, axis=-1, keepdims=True) + RMS_EPS) * gn_ref[:, lo:hi]
            o_ref[bb, r0:r0 + c, lo:hi] = y.astype(o_ref.dtype)
        k_tail = (k * jnp.exp(last - cum)).astype(BF16)
        update = _dot_tn(v, k_tail)
        state_ref[bb] = state * jnp.exp(last) + jnp.where(own_head, update, 0.0)


def _gla(gq, gk, gv, gates, w_gate, b_gate, g_norm):
    b, s, _ = gq.shape
    t = min(GLA_TOKENS, s)
    nb = GLA_BATCH if b % GLA_BATCH == 0 else 1
    tok = lambda width: pl.BlockSpec((nb, t, width), lambda bi, i: (bi, i, 0))
    block_bytes = nb * t * (2 * GLA_K_WIDTH * 2 + 2 * GLA_V_WIDTH * 2 + GATE_ROWS * 4)
    state_bytes = nb * GLA_V_WIDTH * GLA_K_WIDTH * 4
    return pl.pallas_call(
        _gla_kernel,
        grid=(b // nb, s // t),
        in_specs=[tok(GLA_K_WIDTH), tok(GLA_K_WIDTH), tok(GLA_V_WIDTH),
                  pl.BlockSpec((nb, GATE_ROWS, t), lambda bi, i: (bi, 0, i)),
                  _resident(w_gate.shape), _resident((1, GLA_K_WIDTH)), _resident((1, GLA_V_WIDTH))],
        out_specs=tok(GLA_V_WIDTH),
        out_shape=jax.ShapeDtypeStruct((b, s, GLA_V_WIDTH), BF16),
        scratch_shapes=[pltpu.VMEM((nb, GLA_V_WIDTH, GLA_K_WIDTH), F32)],
        compiler_params=pltpu.CompilerParams(
            dimension_semantics=("parallel", "arbitrary"),
            vmem_limit_bytes=_vmem_limit(block_bytes, state_bytes, 8 * state_bytes + 16 * t * GLA_K_WIDTH * 4)),
        name="gla",
    )(gq, gk, gv, gates, w_gate, b_gate, g_norm)


_B_POOL = 0
_B_Z = _B_POOL + POOL_WIDTH
_B_MERGE = _B_Z + FOX_WIDTH + GLA_V_WIDTH + POOL_WIDTH


def _merge_kernel(x_ref, mod_ref, gpre_ref, gpost_ref, ofox_ref, ogla_ref, wb_ref, wpool_ref, pscale_ref,
                  wbr_ref, wout_ref, o_ref, u_ref, lvl_a_ref, lvl_b_ref):
    t = x_ref.shape[1]
    i = pl.program_id(1)
    x = x_ref[0]
    mod = mod_ref[0]
    rows = t // 4
    h_parts = [_modulated_norm(x[r * rows:(r + 1) * rows], gpre_ref[...], mod) for r in range(4)]
    h = jnp.concatenate(h_parts, axis=0)

    @pl.when(i == 0)
    def _():
        u_ref[0:POOL_HALO, :] = jnp.zeros((POOL_HALO, POOL_WIDTH), F32)

    @pl.when(i > 0)
    def _():
        u_ref[0:POOL_HALO, :] = u_ref[t:t + POOL_HALO, :]

    u = jnp.concatenate([_dot(hp, wb_ref[:, _B_POOL:_B_Z]) for hp in h_parts], axis=0)
    u_ref[POOL_HALO:POOL_HALO + t, :] = u

    def branch(br, o_br, merged):
        z = _dot(h, wb_ref[:, _B_Z + br * FOX_WIDTH:_B_Z + (br + 1) * FOX_WIDTH])
        y = _dot((o_br * _silu(z)).astype(BF16), wbr_ref[br])
        m = _dot(h, wb_ref[:, _B_MERGE + br * D_MODEL:_B_MERGE + (br + 1) * D_MODEL])
        return merged + _sigmoid(m) * y

    merged = branch(0, ofox_ref[0].astype(F32), jnp.zeros((t, D_MODEL), F32))
    merged = branch(1, ogla_ref[0].astype(F32), merged)

    count = (i * t + 1 + lax.broadcasted_iota(jnp.int32, (t, 1), 0)).astype(F32)
    levels = (u_ref, lvl_a_ref, lvl_b_ref, lvl_a_ref, lvl_b_ref)
    for n in range(1, len(POOL_WINDOWS) + 1):
        src, dst, back = levels[n - 1], levels[n], 2 ** (n - 1)
        first, rows = 8 * n, POOL_HALO + t - 8 * n
        lanes = slice((n - 1) * POOL_GROUP_DIM, POOL_WIDTH)
        dst[pl.ds(first, rows), lanes] = src[pl.ds(first, rows), lanes] + src[pl.ds(first - back, rows), lanes]
    pooled = []
    for g, w in enumerate(POOL_WINDOWS):
        lo, hi = g * POOL_GROUP_DIM, (g + 1) * POOL_GROUP_DIM
        window = levels[g + 1][pl.ds(POOL_HALO, t), lo:hi]
        mean = window * (1.0 / jnp.minimum(count, float(w)))
        diff = (mean - u[:, lo:hi]).astype(BF16)
        pooled.append(_dot(diff, wpool_ref[g]) * pscale_ref[:, lo:hi])
    o_pool = jnp.concatenate(pooled, axis=-1)

    merged = branch(2, o_pool, merged)
    out = _dot(merged.astype(BF16), wout_ref[...])
    gated_gain = mod[:, 2 * D_MODEL:3 * D_MODEL] * gpost_ref[...]
    o_ref[0] = x + out * lax.rsqrt(jnp.mean(out * out, axis=-1, keepdims=True) + RMS_EPS) * gated_gain


def _merge(x, mod, g_pre, g_post, o_fox, o_gla, w_b, w_pool, pool_scale, w_br, w_out):
    b, s, d = x.shape
    t = min(MERGE_TOKENS, s)
    tok = lambda width: pl.BlockSpec((1, t, width), lambda bi, i: (bi, i, 0))
    block_bytes = 2 * t * d * 4 + 2 * t * FOX_WIDTH * 2
    weight_bytes = (w_b.size + w_pool.size + w_br.size + w_out.size) * 2
    return pl.pallas_call(
        _merge_kernel,
        grid=(b, s // t),
        in_specs=[tok(d),
                  pl.BlockSpec((1, 1, 3 * d), lambda bi, i: (bi, 0, 0)),
                  _resident((1, d)), _resident((1, d)),
                  tok(FOX_WIDTH), tok(GLA_V_WIDTH),
                  _resident(w_b.shape), _resident(w_pool.shape), _resident((1, POOL_WIDTH)),
                  _resident(w_br.shape), _resident(w_out.shape)],
        out_specs=tok(d),
        out_shape=jax.ShapeDtypeStruct((b, s, d), F32),
        scratch_shapes=[pltpu.VMEM((POOL_HALO + t, POOL_WIDTH), F32)] * 3,
        compiler_params=pltpu.CompilerParams(
            dimension_semantics=("parallel", "arbitrary"),
            vmem_limit_bytes=_vmem_limit(block_bytes, weight_bytes, 12 * t * d * 4)),
        name="merge",
    )(x, mod, g_pre, g_post, o_fox, o_gla, w_b, w_pool, pool_scale, w_br, w_out)


def _layer(x, mod, g_pre, g_post, w_in, b_forget, w_gla_gate, b_gla_gate, g_gla_norm, w_pool, pool_scale,
           w_br_fox, w_br_gla, w_br_pool, w_out):
    b, s, d = x.shape
    seg = lambda n: w_in[:, _OFF[n]:_OFF[n + 1]]
    w_a = jnp.concatenate([seg(1), seg(4) * GLA_KEY_DIM ** -0.5, seg(5), seg(6)], axis=1).astype(BF16)
    w_t = jnp.concatenate([seg(0) * (LOG2_E * FOX_HEAD_DIM ** -0.5), seg(3),
                           jnp.zeros((d, KN_ROWS - FOX_HEADS), w_in.dtype), seg(7), seg(2)], axis=1).T.astype(BF16)
    w_b = w_in[:, _OFF[8]:_OFF[13]].astype(BF16)
    b_forget_col = jnp.zeros((KN_ROWS, 1), F32).at[:FOX_HEADS, 0].set(b_forget)
    w_br = jnp.stack([w_br_fox, w_br_gla, w_br_pool]).astype(BF16)
    mod3 = mod.reshape(b, 1, 3 * d)
    g_pre = g_pre.reshape(1, d)

    head_sel = (jnp.arange(KN_ROWS)[:, None] == jnp.arange(FOX_WIDTH)[None, :] // FOX_HEAD_DIM).astype(BF16)
    fk, gq, gk, gv, fqt, fvt, gates, kn2 = _in_proj(x, mod3, g_pre, w_a, w_t, head_sel)
    kx, qx = _forget_cumsum(gates, b_forget_col)
    t_fox = min(FOX_TILE, s)
    assert s // t_fox <= LANES, "the tile-skip bound keeps one key tile per lane"
    flast = qx[:, :, t_fox - 1::t_fox]
    flast = jnp.pad(flast, ((0, 0), (0, 0), (0, -flast.shape[2] % LANES)))
    o_fox = _fox_attention(fqt, qx, fk, kx, fvt, kn2, flast)
    o_gla = _gla(gq, gk, gv, gates, w_gla_gate.astype(BF16), b_gla_gate.reshape(1, GLA_K_WIDTH),
                 g_gla_norm.reshape(1, GLA_V_WIDTH))
    return _merge(x, mod3, g_pre, g_post.reshape(1, d), o_fox, o_gla, w_b, w_pool.astype(BF16),
                  pool_scale.reshape(1, POOL_WIDTH), w_br, w_out.astype(BF16))


def kernel(x, c, w_ada, b_ada, g_pre, g_post, w_in, b_forget, w_gla_gate, b_gla_gate, g_gla_norm, w_pool,
           pool_scale, w_br_fox, w_br_gla, w_br_pool, w_out):
    mods = _adaln_mod(c, w_ada, b_ada)
    h = x
    for i in range(w_in.shape[0]):
        h = _layer(h, mods[i], g_pre[i], g_post[i], w_in[i], b_forget[i], w_gla_gate[i], b_gla_gate[i],
                   g_gla_norm[i], w_pool[i], pool_scale[i], w_br_fox[i], w_br_gla[i], w_br_pool[i], w_out[i])
    return h
```

```python
import functools

import numpy as np
import jax
import jax.numpy as jnp
from jax import lax
from jax.experimental import pallas as pl
from jax.experimental.pallas import tpu as pltpu

F32 = jnp.float32
BF16 = jnp.bfloat16

D_MODEL = 1024
FOX_HEADS = 8
FOX_HEAD_DIM = 64
FOX_WIDTH = FOX_HEADS * FOX_HEAD_DIM
GLA_HEADS = 4
GLA_KEY_DIM = 64
GLA_VAL_DIM = 128
GLA_K_WIDTH = GLA_HEADS * GLA_KEY_DIM
GLA_V_WIDTH = GLA_HEADS * GLA_VAL_DIM
GLA_GATE_RANK = 16
GLA_TAU = 16.0
POOL_WINDOWS = (2, 4, 8, 16)
POOL_GROUP_DIM = 128
POOL_WIDTH = len(POOL_WINDOWS) * POOL_GROUP_DIM
N_BRANCHES = 3
RMS_EPS = 1e-6
IN_SPLITS = (FOX_WIDTH, FOX_WIDTH, FOX_WIDTH, FOX_HEADS,
             GLA_K_WIDTH, GLA_K_WIDTH, GLA_V_WIDTH, GLA_GATE_RANK,
             POOL_WIDTH, FOX_WIDTH, GLA_V_WIDTH, POOL_WIDTH, N_BRANCHES * D_MODEL)
_OFF = np.concatenate([[0], np.cumsum(IN_SPLITS)]).tolist()

LANES = 128
MXU_WIDTH = 256
V_ROWS = FOX_HEAD_DIM + 16
POOL_HALO = 8 * len(POOL_WINDOWS)
MASK_VALUE = -1e30
LOG2_E = 1.4426950408889634
SKIP_LOG2 = 136.0
NORM_SLACK = 1.02
V7X_VMEM_BYTES = 64 * 1024 * 1024

IN_TOKENS = 512
FORGET_TOKENS = 2048
FOX_TILE = 256
FOX_PAIRS = 4
GLA_TOKENS = 256
GLA_CHUNK = 64
GLA_BATCH = 8
MERGE_TOKENS = 512


def _vmem_limit(block_bytes, resident_bytes, temp_bytes):
    need = 2 * block_bytes + resident_bytes + temp_bytes
    return int(min(max(need, 16 * 1024 * 1024), V7X_VMEM_BYTES - 8 * 1024 * 1024))


def _resident(shape):
    zeros = (0,) * len(shape)
    return pl.BlockSpec(shape, lambda *_: zeros, pipeline_mode=pl.Buffered(1))


def _div_pow2(i, n):
    shift = n.bit_length() - 1
    assert 1 << shift == n
    return jnp.right_shift(i, shift)


def _log_sigmoid(x):
    return jnp.minimum(x, 0.0) - jnp.log(1.0 + jnp.exp(-jnp.abs(x)))


def _sigmoid(x):
    return 0.5 * jnp.tanh(0.5 * x) + 0.5


def _silu(x):
    half = 0.5 * x
    return half * jnp.tanh(half) + half


def _dot(a, b):
    return jnp.dot(a, b, preferred_element_type=F32)


def _dot_nt(a, b):
    return lax.dot_general(a, b, (((1,), (1,)), ((), ())), preferred_element_type=F32)


def _dot_tn(a, b):
    return lax.dot_general(a, b, (((0,), (0,)), ((), ())), preferred_element_type=F32)


def _split3(x):
    hi = x.astype(BF16)
    r1 = x - hi.astype(F32)
    mid = r1.astype(BF16)
    lo = (r1 - mid.astype(F32)).astype(BF16)
    return hi, mid, lo


def _ones_dot_f32(ones_bf16, x):
    hi, mid, lo = _split3(x)
    return _dot(ones_bf16, hi) + _dot(ones_bf16, mid) + _dot(ones_bf16, lo)


def _modulated_norm(x, g, mod):
    shift = mod[:, 0:D_MODEL]
    scale = mod[:, D_MODEL:2 * D_MODEL]
    gain = g * (1.0 + scale)
    return (x * lax.rsqrt(jnp.mean(x * x, axis=-1, keepdims=True) + RMS_EPS) * gain + shift).astype(BF16)


def _mod_kernel(c_ref, w_ref, b_ref, o_ref):
    c = c_ref[...]
    o_ref[0] = jnp.dot(_silu(c), w_ref[0], preferred_element_type=F32,
                       precision=lax.Precision.HIGHEST) + b_ref[0]


def _adaln_mod(c, w_ada, b_ada):
    depth, d, d3 = w_ada.shape
    b = c.shape[0]
    n_col = d3 // d
    return pl.pallas_call(
        _mod_kernel,
        grid=(depth, n_col),
        in_specs=[pl.BlockSpec((b, d), lambda l, j: (0, 0)),
                  pl.BlockSpec((1, d, d), lambda l, j: (l, 0, j)),
                  pl.BlockSpec((1, 1, d), lambda l, j: (l, 0, j))],
        out_specs=pl.BlockSpec((1, b, d), lambda l, j: (l, 0, j)),
        out_shape=jax.ShapeDtypeStruct((depth, b, d3), F32),
        compiler_params=pltpu.CompilerParams(
            dimension_semantics=("parallel", "parallel"),
            vmem_limit_bytes=_vmem_limit(d * d * 4 + 2 * b * d * 4, 0, 4 * b * d * 4)),
        name="adaln_mod",
    )(c, w_ada, b_ada.reshape(depth, 1, d3))


_A_FOX_K = 0
_A_GLA_Q = _A_FOX_K + FOX_WIDTH
_A_GLA_K = _A_GLA_Q + GLA_K_WIDTH
_A_GLA_V = _A_GLA_K + GLA_K_WIDTH
_A_WIDTH = _A_GLA_V + GLA_V_WIDTH
KN_ROWS = 16
GATE_ROWS = KN_ROWS + GLA_GATE_RANK
_T_FOX_Q = 0
_T_GATES = _T_FOX_Q + FOX_WIDTH
_T_FOX_V = _T_GATES + GATE_ROWS
_T_ROWS = _T_FOX_V + FOX_WIDTH


def _in_proj_kernel(x_ref, mod_ref, g_ref, wa_ref, wt_ref, hsel_ref,
                    fk_ref, gq_ref, gk_ref, gv_ref, fqt_ref, fvt_ref, gates_ref, kn2_ref):
    h = _modulated_norm(x_ref[0], g_ref[...], mod_ref[0])
    fk = _dot(h, wa_ref[:, _A_FOX_K:_A_GLA_Q]).astype(BF16)
    fk_ref[0] = fk
    norms = _dot_nt(hsel_ref[...], fk * fk)
    tile_max = jnp.broadcast_to(jnp.max(norms, axis=1, keepdims=True), kn2_ref.shape[1:])

    @pl.when(pl.program_id(1) == 0)
    def _():
        kn2_ref[0] = tile_max

    @pl.when(pl.program_id(1) > 0)
    def _():
        kn2_ref[0] = jnp.maximum(kn2_ref[0], tile_max)

    gq_ref[0] = _dot(h, wa_ref[:, _A_GLA_Q:_A_GLA_K]).astype(BF16)
    gk_ref[0] = _dot(h, wa_ref[:, _A_GLA_K:_A_GLA_V]).astype(BF16)
    gv_ref[0] = _dot(h, wa_ref[:, _A_GLA_V:_A_WIDTH]).astype(BF16)
    tr = _dot_nt(wt_ref[...], h)
    fqt_ref[0] = tr[_T_FOX_Q:_T_GATES].astype(BF16)
    gates_ref[0] = tr[_T_GATES:_T_FOX_V]
    ones = jnp.ones((V_ROWS - FOX_HEAD_DIM, tr.shape[1]), BF16)
    for hd in range(FOX_HEADS):
        v_rows = tr[_T_FOX_V + hd * FOX_HEAD_DIM:_T_FOX_V + (hd + 1) * FOX_HEAD_DIM]
        fvt_ref[0, hd * V_ROWS:hd * V_ROWS + FOX_HEAD_DIM, :] = v_rows.astype(BF16)
        fvt_ref[0, hd * V_ROWS + FOX_HEAD_DIM:(hd + 1) * V_ROWS, :] = ones


def _in_proj(x, mod, g_pre, w_a, w_t, head_sel):
    b, s, d = x.shape
    t = min(IN_TOKENS, s)
    tok = lambda width: pl.BlockSpec((1, t, width), lambda bi, i: (bi, i, 0))
    tok_t = lambda rows: pl.BlockSpec((1, rows, t), lambda bi, i: (bi, 0, i))
    v_rows = FOX_HEADS * V_ROWS
    out_shapes = (
        jax.ShapeDtypeStruct((b, s, FOX_WIDTH), BF16),
        jax.ShapeDtypeStruct((b, s, GLA_K_WIDTH), BF16),
        jax.ShapeDtypeStruct((b, s, GLA_K_WIDTH), BF16),
        jax.ShapeDtypeStruct((b, s, GLA_V_WIDTH), BF16),
        jax.ShapeDtypeStruct((b, FOX_WIDTH, s), BF16),
        jax.ShapeDtypeStruct((b, v_rows, s), BF16),
        jax.ShapeDtypeStruct((b, GATE_ROWS, s), F32),
        jax.ShapeDtypeStruct((b, KN_ROWS, LANES), F32),
    )
    out_specs = (tok(FOX_WIDTH), tok(GLA_K_WIDTH), tok(GLA_K_WIDTH), tok(GLA_V_WIDTH), tok_t(FOX_WIDTH), tok_t(v_rows),
                 tok_t(GATE_ROWS), pl.BlockSpec((1, KN_ROWS, LANES), lambda bi, i: (bi, 0, 0)))
    n_out = _A_WIDTH + FOX_WIDTH + v_rows
    block_bytes = t * d * 4 + t * n_out * 2 + t * GATE_ROWS * 4
    weight_bytes = d * (_A_WIDTH + _T_ROWS) * 2
    return pl.pallas_call(
        _in_proj_kernel,
        grid=(b, s // t),
        in_specs=[tok(d),
                  pl.BlockSpec((1, 1, 3 * d), lambda bi, i: (bi, 0, 0)),
                  _resident((1, d)), _resident(w_a.shape), _resident(w_t.shape), _resident(head_sel.shape)],
        out_specs=out_specs,
        out_shape=out_shapes,
        compiler_params=pltpu.CompilerParams(
            dimension_semantics=("parallel", "arbitrary"),
            vmem_limit_bytes=_vmem_limit(block_bytes, weight_bytes, 6 * t * d * 4)),
        name="in_proj",
    )(x, mod, g_pre, w_a, w_t, head_sel)


BIAS_SLOTS = 16


def _bias_placement():
    pk = np.zeros((3 * KN_ROWS, FOX_WIDTH), np.float32)
    k_const = np.zeros((1, FOX_WIDTH), np.float32)
    pq = np.zeros((FOX_HEADS * BIAS_SLOTS, 3 * KN_ROWS), np.float32)
    q_const = np.zeros((FOX_HEADS * BIAS_SLOTS, 1), np.float32)
    for head in range(FOX_HEADS):
        pair, odd = divmod(head, 2)
        for part in range(3):
            pk[part * KN_ROWS + head, pair * LANES + 3 + 3 * odd + part] = -1.0
            pq[head * BIAS_SLOTS + part, part * KN_ROWS + head] = 1.0
            q_const[head * BIAS_SLOTS + 3 + 3 * odd + part, 0] = 1.0
            k_const[0, pair * LANES + part] = 1.0
    return pk, k_const, pq, q_const


def _forget_kernel(fft_ref, b_ref, pk_ref, kc_ref, pq_ref, qc_ref, kx_ref, qx_ref, carry_ref):
    @pl.when(pl.program_id(1) == 0)
    def _():
        carry_ref[...] = jnp.zeros_like(carry_ref)

    t = fft_ref.shape[2]
    w = min(MXU_WIDTH, t)
    log_f = _log_sigmoid(fft_ref[0] + b_ref[...])
    row = lax.broadcasted_iota(jnp.int32, (w, w), 0)
    col = lax.broadcasted_iota(jnp.int32, (w, w), 1)
    upper = jnp.where(row <= col, 1.0, 0.0).astype(BF16)
    split = jnp.concatenate(_split3(log_f), axis=0)
    carry = carry_ref[:, 0:1]
    blocks = []
    for j in range(t // w):
        sums = _dot(split[:, j * w:(j + 1) * w], upper)
        blocks.append(sums[0:KN_ROWS] + sums[KN_ROWS:2 * KN_ROWS] + sums[2 * KN_ROWS:3 * KN_ROWS] + carry)
        carry = blocks[-1][:, w - 1:w]
    carry_ref[...] = jnp.broadcast_to(carry, carry_ref.shape)
    cs = jnp.concatenate(blocks, axis=1)
    parts = jnp.concatenate(_split3(cs * LOG2_E), axis=0)
    qx_ref[0] = (_dot(pq_ref[...], parts) + qc_ref[...]).astype(BF16)
    kx_ref[0] = (_dot_tn(parts, pk_ref[...]) + kc_ref[...]).astype(BF16)


def _forget_cumsum(gates, b_forget_col):
    b, _, s = gates.shape
    t = min(FORGET_TOKENS, s)
    pk, k_const, pq, q_const = _bias_placement()
    q_rows = FOX_HEADS * BIAS_SLOTS
    return pl.pallas_call(
        _forget_kernel,
        grid=(b, s // t),
        in_specs=[pl.BlockSpec((1, KN_ROWS, t), lambda bi, i: (bi, 0, i)),
                  _resident((KN_ROWS, 1)), _resident(pk.shape), _resident(k_const.shape),
                  _resident(pq.shape), _resident(q_const.shape)],
        out_specs=(pl.BlockSpec((1, t, FOX_WIDTH), lambda bi, i: (bi, i, 0)),
                   pl.BlockSpec((1, q_rows, t), lambda bi, i: (bi, 0, i))),
        out_shape=(jax.ShapeDtypeStruct((b, s, FOX_WIDTH), BF16),
                   jax.ShapeDtypeStruct((b, q_rows, s), BF16)),
        scratch_shapes=[pltpu.VMEM((KN_ROWS, LANES), F32)],
        compiler_params=pltpu.CompilerParams(
            dimension_semantics=("parallel", "arbitrary"),
            vmem_limit_bytes=_vmem_limit(t * (KN_ROWS * 4 + FOX_WIDTH * 2 + q_rows * 2), pk.size * 2 + pq.size * 2,
                                         8 * t * FOX_WIDTH * 4)),
        name="forget_cumsum",
    )(gates, b_forget_col, jnp.asarray(pk, BF16), jnp.asarray(k_const), jnp.asarray(pq, BF16), jnp.asarray(q_const))


def _fox_kernel(qt_ref, qx_ref, k_ref, kx_ref, vt_ref, kn2_ref, flast_ref, o_ref, s_ref, p_ref, acc_ref):
    t = qt_ref.shape[2]
    i = pl.program_id(2)
    n_heads = 2 * FOX_PAIRS
    row = lax.broadcasted_iota(jnp.int32, (LANES, t), 0)
    low = row < FOX_HEAD_DIM
    pad = jnp.zeros((LANES - BIAS_SLOTS, t), BF16)
    q_aug = []
    for hd in range(n_heads):
        qt = qt_ref[0, (hd // 2) * LANES:(hd // 2 + 1) * LANES, :]
        own = jnp.where(low, qt, jnp.zeros_like(qt)) if hd % 2 == 0 else jnp.where(low, jnp.zeros_like(qt), qt)
        q_aug.append(jnp.concatenate([own, qx_ref[0, hd * BIAS_SLOTS:(hd + 1) * BIAS_SLOTS, :], pad], axis=0))

    def logits(j, masked):
        k0 = pl.multiple_of(j * t, t)
        if masked:
            causal = lax.broadcasted_iota(jnp.int32, (t, t), 0) <= lax.broadcasted_iota(jnp.int32, (t, t), 1)
        out = []
        for pr in range(FOX_PAIRS):
            lanes = slice(pr * LANES, (pr + 1) * LANES)
            k_aug = jnp.concatenate([k_ref[0, pl.ds(k0, t), lanes], kx_ref[0, pl.ds(k0, t), lanes]], axis=1)
            for h in range(2):
                s = _dot(k_aug, q_aug[2 * pr + h])
                out.append(jnp.where(causal, s, MASK_VALUE) if masked else s)
        return out

    def weighted_values(j, hd, p):
        k0 = pl.multiple_of(j * t, t)
        return _dot(vt_ref[0, hd * V_ROWS:(hd + 1) * V_ROWS, pl.ds(k0, t)], p)

    def stage_logits(j, masked, slot):
        maxes = []
        for hd, s in enumerate(logits(j, masked)):
            s_ref[slot, hd] = s
            maxes.append(jnp.max(s, axis=0, keepdims=True))
        return maxes

    def stage_softmax(hd, s_max, m, s_slot, p_slot):
        m_new = jnp.maximum(m, s_max)
        p_ref[p_slot, hd] = jnp.exp2(s_ref[s_slot, hd] - m_new).astype(BF16)
        return m_new, jnp.exp2(m - m_new)

    m0 = jnp.full((1, t), MASK_VALUE, F32)
    s_max = stage_logits(i, True, 1)
    stats = [stage_softmax(hd, s_max[hd], m0, 1, 0) for hd in range(n_heads)]
    s_max = stage_logits(jnp.maximum(i - 1, 0), False, 0)
    acc_ref[...] = jnp.zeros_like(acc_ref)

    tile_id = lax.broadcasted_iota(jnp.int32, (1, LANES), 1).astype(F32)
    first_head = pl.program_id(1) * n_heads
    first_needed = []
    for hd in range(n_heads):
        base = hd * BIAS_SLOTS
        f_k = sum(flast_ref[0, base + part:base + part + 1, :].astype(F32) for part in range(3))
        f_q = sum(qx_ref[0, base + part:base + part + 1, 0:1].astype(F32) for part in range(3))
        q = qt_ref[0, hd * FOX_HEAD_DIM:(hd + 1) * FOX_HEAD_DIM, :].astype(F32)
        qn2 = jnp.max(jnp.sum(q * q, axis=0, keepdims=True), axis=1, keepdims=True)
        qk = jnp.sqrt(qn2 * kn2_ref[0, pl.ds(first_head + hd, 1), 0:1]) * NORM_SLACK
        m_min = jnp.min(stats[hd][0], axis=1, keepdims=True)
        needed = (qk + f_q - f_k >= m_min - SKIP_LOG2) & (tile_id < i.astype(F32))
        first_needed.append(jnp.min(jnp.where(needed, tile_id, i.astype(F32)), axis=1, keepdims=True))
    first_tile = jnp.min(jnp.concatenate(first_needed, axis=1), axis=1, keepdims=True)[0, 0].astype(jnp.int32)
    n_trips = i - first_tile

    def trip(par, fetch, k, carry):
        prev, heads = carry
        for hd in range(n_heads):
            pv = weighted_values(prev, hd, p_ref[par, hd])
            acc_ref[hd] = heads[hd][1] * acc_ref[hd] + pv
        stats = [stage_softmax(hd, heads[hd][2], heads[hd][0], par, 1 - par) for hd in range(n_heads)]
        s_max = stage_logits(i - 2 - k, False, 1 - par) if fetch else [h[2] for h in heads]
        return i - 1 - k, tuple(stats[hd] + (s_max[hd],) for hd in range(n_heads))

    def either_trip(fetch, k, carry):
        return lax.cond((k & 1) == 0, functools.partial(trip, 0, fetch, k), functools.partial(trip, 1, fetch, k), carry)

    def finish(p_slot, carry):
        prev, heads = carry
        o_heads = []
        for hd in range(n_heads):
            acc = heads[hd][1] * acc_ref[hd] + weighted_values(prev, hd, p_ref[p_slot, hd])
            o_heads.append(acc[0:FOX_HEAD_DIM] * (1.0 / acc[FOX_HEAD_DIM:FOX_HEAD_DIM + 1]))
        o_ref[0] = jnp.concatenate(o_heads, axis=0).T.astype(o_ref.dtype)
        return prev

    def last_trip(par, carry):
        return finish(1 - par, trip(par, False, n_trips - 1, carry))

    carry = (i, tuple(stats[hd] + (s_max[hd],) for hd in range(n_heads)))
    carry = lax.fori_loop(0, n_trips - 1, functools.partial(either_trip, True), carry)
    last_par = (n_trips - 1) & 1
    lax.cond(n_trips == 0, functools.partial(finish, 0),
             lambda c: lax.cond(last_par == 0, functools.partial(last_trip, 0), functools.partial(last_trip, 1), c),
             carry)


def _fox_attention(fqt, qx, fk, kx, fvt, kn2, flast):
    b, s, _ = fk.shape
    t = min(FOX_TILE, s)
    width = FOX_PAIRS * LANES
    n_heads = 2 * FOX_PAIRS
    groups = FOX_WIDTH // width
    whole = lambda shape, index_map: pl.BlockSpec(shape, index_map, pipeline_mode=pl.Buffered(1))
    block_bytes = (width + n_heads * BIAS_SLOTS) * t * 2 + t * width * 2
    resident_bytes = (2 * width + n_heads * V_ROWS) * s * 2 + n_heads * t * (2 * t * 6 + V_ROWS * 4)
    return pl.pallas_call(
        _fox_kernel,
        grid=(b, groups, s // t),
        in_specs=[pl.BlockSpec((1, width, t), lambda bi, g, i: (bi, g, i)),
                  pl.BlockSpec((1, n_heads * BIAS_SLOTS, t), lambda bi, g, i: (bi, g, i)),
                  whole((1, s, width), lambda bi, g, i: (bi, 0, g)),
                  whole((1, s, width), lambda bi, g, i: (bi, 0, g)),
                  whole((1, n_heads * V_ROWS, s), lambda bi, g, i: (bi, g, 0)),
                  whole((1, KN_ROWS, LANES), lambda bi, g, i: (bi, 0, 0)),
                  whole((1, n_heads * BIAS_SLOTS, flast.shape[2]), lambda bi, g, i: (bi, g, 0))],
        out_specs=pl.BlockSpec((1, t, width), lambda bi, g, i: (bi, i, g)),
        out_shape=jax.ShapeDtypeStruct((b, s, FOX_WIDTH), BF16),
        scratch_shapes=[pltpu.VMEM((2, n_heads, t, t), F32), pltpu.VMEM((2, n_heads, t, t), BF16),
                        pltpu.VMEM((n_heads, V_ROWS, t), F32)],
        compiler_params=pltpu.CompilerParams(
            dimension_semantics=("parallel", "parallel", "parallel"),
            vmem_limit_bytes=_vmem_limit(block_bytes, resident_bytes, 8 * n_heads * t * t)),
        name="fox_attention",
    )(fqt, qx, fk, kx, fvt, kn2, flast)


def _gla_kernel(q_ref, k_ref, v_ref, gates_ref, wg_ref, bg_ref, gn_ref, o_ref, state_ref):
    @pl.when(pl.program_id(1) == 0)
    def _():
        state_ref[...] = jnp.zeros_like(state_ref)

    nb, t, _ = q_ref.shape
    c = GLA_CHUNK
    half = c // 2
    row = lax.broadcasted_iota(jnp.int32, (t, t), 0)
    col = lax.broadcasted_iota(jnp.int32, (t, t), 1)
    same_chunk = _div_pow2(row, c) == _div_pow2(col, c)
    tri = jnp.where(same_chunk & (col <= row), 1.0, 0.0).astype(BF16)
    cum_all = []
    for bb in range(nb):
        gate_logits = _dot_tn(gates_ref[bb, KN_ROWS:GATE_ROWS, :].astype(BF16), wg_ref[...]) + bg_ref[...]
        cum_all.append(_ones_dot_f32(tri, _log_sigmoid(gate_logits) * (1.0 / GLA_TAU)))

    crow = lax.broadcasted_iota(jnp.int32, (c, GLA_K_WIDTH), 0)
    first = crow < half
    lane = lax.broadcasted_iota(jnp.int32, (c, GLA_K_WIDTH), 1)
    head_lanes = [_div_pow2(lane, GLA_KEY_DIM) == h for h in range(GLA_HEADS)]
    sr = lax.broadcasted_iota(jnp.int32, (GLA_HEADS * c, c), 0) & (c - 1)
    sc = lax.broadcasted_iota(jnp.int32, (GLA_HEADS * c, c), 1)
    near = (_div_pow2(sr, half) == _div_pow2(sc, half)) & (sc <= sr)
    srow = lax.broadcasted_iota(jnp.int32, state_ref.shape[1:], 0)
    slane = lax.broadcasted_iota(jnp.int32, state_ref.shape[1:], 1)
    own_head = _div_pow2(srow, GLA_VAL_DIM) == _div_pow2(slane, GLA_KEY_DIM)

    def stack_heads(x):
        return jnp.concatenate([jnp.where(m, x, 0.0) for m in head_lanes], axis=0).astype(BF16)

    for ci, bb in [(ci, bb) for ci in range(t // c) for bb in range(nb)]:
        r0 = ci * c
        cum = cum_all[bb][r0:r0 + c]
        q = q_ref[bb, r0:r0 + c, :].astype(F32)
        k = k_ref[bb, r0:r0 + c, :].astype(F32)
        v = v_ref[bb, r0:r0 + c, :]
        last = cum[c - 1:c]
        ref_far = cum[half - 1:half]
        ref_near = jnp.where(first, cum[half // 2:half // 2 + 1], cum[half + half // 2:half + half // 2 + 1])
        q_far = jnp.where(first, 0.0, q * jnp.exp(cum - ref_far))
        k_far = jnp.where(first, k * jnp.exp(ref_far - cum), 0.0)
        q_near = q * jnp.exp(cum - ref_near)
        k_near = k * jnp.exp(ref_near - cum)
        scores = (_dot_nt(stack_heads(q_far), k_far.astype(BF16))
                  + jnp.where(near, _dot_nt(stack_heads(q_near), k_near.astype(BF16)), 0.0))
        state = state_ref[bb]
        o = _dot_nt((q * jnp.exp(cum)).astype(BF16), state.astype(BF16))
        p = scores.astype(BF16)
        for h in range(GLA_HEADS):
            lo, hi = h * GLA_VAL_DIM, (h + 1) * GLA_VAL_DIM
            o_h = o[:, lo:hi] + _dot(p[h * c:(h + 1) * c], v[:, lo:hi])
            y = o_h * lax.rsqrt(jnp.mean(o_h * o_h, axis=-1, keepdims=True) + RMS_EPS) * gn_ref[:, lo:hi]
            o_ref[bb, r0:r0 + c, lo:hi] = y.astype(o_ref.dtype)
        k_tail = (k * jnp.exp(last - cum)).astype(BF16)
        update = _dot_tn(v, k_tail)
        state_ref[bb] = state * jnp.exp(last) + jnp.where(own_head, update, 0.0)


def _gla(gq, gk, gv, gates, w_gate, b_gate, g_norm):
    b, s, _ = gq.shape
    t = min(GLA_TOKENS, s)
    nb = GLA_BATCH if b % GLA_BATCH == 0 else 1
    tok = lambda width: pl.BlockSpec((nb, t, width), lambda bi, i: (bi, i, 0))
    block_bytes = nb * t * (2 * GLA_K_WIDTH * 2 + 2 * GLA_V_WIDTH * 2 + GATE_ROWS * 4)
    state_bytes = nb * GLA_V_WIDTH * GLA_K_WIDTH * 4
    return pl.pallas_call(
        _gla_kernel,
        grid=(b // nb, s // t),
        in_specs=[tok(GLA_K_WIDTH), tok(GLA_K_WIDTH), tok(GLA_V_WIDTH),
                  pl.BlockSpec((nb, GATE_ROWS, t), lambda bi, i: (bi, 0, i)),
                  _resident(w_gate.shape), _resident((1, GLA_K_WIDTH)), _resident((1, GLA_V_WIDTH))],
        out_specs=tok(GLA_V_WIDTH),
        out_shape=jax.ShapeDtypeStruct((b, s, GLA_V_WIDTH), BF16),
        scratch_shapes=[pltpu.VMEM((nb, GLA_V_WIDTH, GLA_K_WIDTH), F32)],
        compiler_params=pltpu.CompilerParams(
            dimension_semantics=("parallel", "arbitrary"),
            vmem_limit_bytes=_vmem_limit(block_bytes, state_bytes, 8 * state_bytes + 16 * t * GLA_K_WIDTH * 4)),
        name="gla",
    )(gq, gk, gv, gates, w_gate, b_gate, g_norm)


_B_POOL = 0
_B_Z = _B_POOL + POOL_WIDTH
_B_MERGE = _B_Z + FOX_WIDTH + GLA_V_WIDTH + POOL_WIDTH


def _merge_kernel(x_ref, mod_ref, gpre_ref, gpost_ref, ofox_ref, ogla_ref, wb_ref, wpool_ref, pscale_ref,
                  wbr_ref, wout_ref, o_ref, u_ref, lvl_a_ref, lvl_b_ref):
    t = x_ref.shape[1]
    i = pl.program_id(1)
    mod = mod_ref[0]
    rows = t // 4
    h_parts = [_modulated_norm(x_ref[0, r * rows:(r + 1) * rows, :], gpre_ref[...], mod) for r in range(4)]
    h = jnp.concatenate(h_parts, axis=0)

    @pl.when(i == 0)
    def _():
        u_ref[0:POOL_HALO, :] = jnp.zeros((POOL_HALO, POOL_WIDTH), F32)

    @pl.when(i > 0)
    def _():
        u_ref[0:POOL_HALO, :] = u_ref[t:t + POOL_HALO, :]

    u = jnp.concatenate([_dot(hp, wb_ref[:, _B_POOL:_B_Z]) for hp in h_parts], axis=0)
    u_ref[POOL_HALO:POOL_HALO + t, :] = u

    def branch(br, o_br, merged):
        z = _dot(h, wb_ref[:, _B_Z + br * FOX_WIDTH:_B_Z + (br + 1) * FOX_WIDTH])
        y = _dot((o_br * _silu(z)).astype(BF16), wbr_ref[br])
        m = _dot(h, wb_ref[:, _B_MERGE + br * D_MODEL:_B_MERGE + (br + 1) * D_MODEL])
        return merged + _sigmoid(m) * y

    merged = branch(0, ofox_ref[0].astype(F32), jnp.zeros((t, D_MODEL), F32))
    merged = branch(1, ogla_ref[0].astype(F32), merged)

    count = (i * t + 1 + lax.broadcasted_iota(jnp.int32, (t, 1), 0)).astype(F32)
    levels = (u_ref, lvl_a_ref, lvl_b_ref, lvl_a_ref, lvl_b_ref)
    for n in range(1, len(POOL_WINDOWS) + 1):
        src, dst, back = levels[n - 1], levels[n], 2 ** (n - 1)
        first, rows = 8 * n, POOL_HALO + t - 8 * n
        lanes = slice((n - 1) * POOL_GROUP_DIM, POOL_WIDTH)
        dst[pl.ds(first, rows), lanes] = src[pl.ds(first, rows), lanes] + src[pl.ds(first - back, rows), lanes]
    pooled = []
    for g, w in enumerate(POOL_WINDOWS):
        lo, hi = g * POOL_GROUP_DIM, (g + 1) * POOL_GROUP_DIM
        window = levels[g + 1][pl.ds(POOL_HALO, t), lo:hi]
        mean = window * (1.0 / jnp.minimum(count, float(w)))
        diff = (mean - u_ref[pl.ds(POOL_HALO, t), lo:hi]).astype(BF16)
        pooled.append(_dot(diff, wpool_ref[g]) * pscale_ref[:, lo:hi])
    o_pool = jnp.concatenate(pooled, axis=-1)

    merged = branch(2, o_pool, merged)
    out = _dot(merged.astype(BF16), wout_ref[...])
    gated_gain = mod[:, 2 * D_MODEL:3 * D_MODEL] * gpost_ref[...]
    o_ref[0] = x_ref[0] + out * lax.rsqrt(jnp.mean(out * out, axis=-1, keepdims=True) + RMS_EPS) * gated_gain


def _merge(x, mod, g_pre, g_post, o_fox, o_gla, w_b, w_pool, pool_scale, w_br, w_out):
    b, s, d = x.shape
    t = min(MERGE_TOKENS, s)
    tok = lambda width: pl.BlockSpec((1, t, width), lambda bi, i: (bi, i, 0))
    block_bytes = 2 * t * d * 4 + 2 * t * FOX_WIDTH * 2
    weight_bytes = (w_b.size + w_pool.size + w_br.size + w_out.size) * 2
    return pl.pallas_call(
        _merge_kernel,
        grid=(b, s // t),
        in_specs=[tok(d),
                  pl.BlockSpec((1, 1, 3 * d), lambda bi, i: (bi, 0, 0)),
                  _resident((1, d)), _resident((1, d)),
                  tok(FOX_WIDTH), tok(GLA_V_WIDTH),
                  _resident(w_b.shape), _resident(w_pool.shape), _resident((1, POOL_WIDTH)),
                  _resident(w_br.shape), _resident(w_out.shape)],
        out_specs=tok(d),
        out_shape=jax.ShapeDtypeStruct((b, s, d), F32),
        scratch_shapes=[pltpu.VMEM((POOL_HALO + t, POOL_WIDTH), F32)] * 3,
        compiler_params=pltpu.CompilerParams(
            dimension_semantics=("parallel", "arbitrary"),
            vmem_limit_bytes=_vmem_limit(block_bytes, weight_bytes, 12 * t * d * 4)),
        name="merge",
    )(x, mod, g_pre, g_post, o_fox, o_gla, w_b, w_pool, pool_scale, w_br, w_out)


def _layer(x, mod, g_pre, g_post, w_in, b_forget, w_gla_gate, b_gla_gate, g_gla_norm, w_pool, pool_scale,
           w_br_fox, w_br_gla, w_br_pool, w_out):
    b, s, d = x.shape
    seg = lambda n: w_in[:, _OFF[n]:_OFF[n + 1]]
    w_a = jnp.concatenate([seg(1), seg(4) * GLA_KEY_DIM ** -0.5, seg(5), seg(6)], axis=1).astype(BF16)
    w_t = jnp.concatenate([seg(0) * (LOG2_E * FOX_HEAD_DIM ** -0.5), seg(3),
                           jnp.zeros((d, KN_ROWS - FOX_HEADS), w_in.dtype), seg(7), seg(2)], axis=1).T.astype(BF16)
    w_b = w_in[:, _OFF[8]:_OFF[13]].astype(BF16)
    b_forget_col = jnp.zeros((KN_ROWS, 1), F32).at[:FOX_HEADS, 0].set(b_forget)
    w_br = jnp.stack([w_br_fox, w_br_gla, w_br_pool]).astype(BF16)
    mod3 = mod.reshape(b, 1, 3 * d)
    g_pre = g_pre.reshape(1, d)

    head_sel = (jnp.arange(KN_ROWS)[:, None] == jnp.arange(FOX_WIDTH)[None, :] // FOX_HEAD_DIM).astype(BF16)
    fk, gq, gk, gv, fqt, fvt, gates, kn2 = _in_proj(x, mod3, g_pre, w_a, w_t, head_sel)
    kx, qx = _forget_cumsum(gates, b_forget_col)
    t_fox = min(FOX_TILE, s)
    assert s // t_fox <= LANES, "the tile-skip bound keeps one key tile per lane"
    flast = qx[:, :, t_fox - 1::t_fox]
    flast = jnp.pad(flast, ((0, 0), (0, 0), (0, -flast.shape[2] % LANES)))
    o_fox = _fox_attention(fqt, qx, fk, kx, fvt, kn2, flast)
    o_gla = _gla(gq, gk, gv, gates, w_gla_gate.astype(BF16), b_gla_gate.reshape(1, GLA_K_WIDTH),
                 g_gla_norm.reshape(1, GLA_V_WIDTH))
    return _merge(x, mod3, g_pre, g_post.reshape(1, d), o_fox, o_gla, w_b, w_pool.astype(BF16),
                  pool_scale.reshape(1, POOL_WIDTH), w_br, w_out.astype(BF16))


def kernel(x, c, w_ada, b_ada, g_pre, g_post, w_in, b_forget, w_gla_gate, b_gla_gate, g_gla_norm, w_pool,
           pool_scale, w_br_fox, w_br_gla, w_br_pool, w_out):
    mods = _adaln_mod(c, w_ada, b_ada)
    h = x
    for i in range(w_in.shape[0]):
        h = _layer(h, mods[i], g_pre[i], g_post[i], w_in[i], b_forget[i], w_gla_gate[i], b_gla_gate[i],
                   g_gla_norm[i], w_pool[i], pool_scale[i], w_br_fox[i], w_br_gla[i], w_br_pool[i], w_out[i])
    return h
```
